```python
import jax, jax.numpy as jnp
from jax import lax
import numpy as np

D_MODEL = 1024
BATCH = 1
SEQ = 16384
DEPTH = 1
DEC_BATCH = 16
DEC_SEQ = 64
PAST_LEN = 1024

CHUNK = 64
D_LRU = D_MODEL // 2
D_SB = D_MODEL - D_LRU
N_LRU_BLOCKS = 8
LRU_BLOCK = D_LRU // N_LRU_BLOCKS
N_SB_HEADS = 8
SB_HEAD_DIM = D_SB // N_SB_HEADS
CONV_WIDTH = 4
RG_C = 8.0
D_FF = 4 * D_MODEL
Q_BLOCK = 128
D_IN = 2 * D_LRU + 3 * D_SB
N_MOD = 6
EPS = 1e-6

kernel_name = 'hybrid_rglru_stickbreaking_stream_step'


def rms_norm(x, g):
    xf = x.astype(jnp.float32)
    y = xf * lax.rsqrt(jnp.mean(xf * xf, axis=-1, keepdims=True) + EPS)
    return (y * g.astype(jnp.float32)).astype(x.dtype)


def causal_dwconv(x, buf, w, b):
    t_new = x.shape[1]
    xp = jnp.concatenate([buf.astype(x.dtype), x], axis=1)
    y = b
    for j in range(CONV_WIDTH):
        y = y + xp[:, j:j + t_new] * w[j]
    return y, xp[:, -(CONV_WIDTH - 1):]


def rg_lru(x, h0, w_a, b_a, w_x, b_x, lam):
    bsz, t_new, _ = x.shape
    xb = x.reshape(bsz, t_new, N_LRU_BLOCKS, LRU_BLOCK)
    r = jax.nn.sigmoid(jnp.einsum('btni,nij->btnj', xb, w_a) + b_a).reshape(bsz, t_new, D_LRU)
    i = jax.nn.sigmoid(jnp.einsum('btni,nij->btnj', xb, w_x) + b_x).reshape(bsz, t_new, D_LRU)
    log_a = (-RG_C * r.astype(jnp.float32)) * jax.nn.softplus(-lam.astype(jnp.float32))
    a = jnp.exp(log_a)
    u = jnp.sqrt(-jnp.expm1(2.0 * log_a)) * (i * x).astype(jnp.float32)
    u = u.at[:, 0].add(a[:, 0] * h0.astype(jnp.float32))

    def combine(left, right):
        a1, b1 = left
        a2, b2 = right
        return a1 * a2, a2 * b1 + b2

    _, h = lax.associative_scan(combine, (a, u), axis=1)
    return h.astype(x.dtype), h[:, -1].astype(x.dtype)


def sb_block(q, k, v, q_pos, k_pos):
    z = jnp.einsum('bqhd,bkhd->bhqk', q, k).astype(jnp.float32) * (SB_HEAD_DIM ** -0.5)
    mask = (k_pos[None, :] < q_pos[:, None])[None, None]
    log_beta = jax.nn.log_sigmoid(z)
    log_keep = jnp.where(mask, log_beta - z, 0.0)
    between = lax.cumsum(log_keep, axis=3, reverse=True) - log_keep
    w = jnp.where(mask, jnp.exp(log_beta + between), 0.0)
    return jnp.einsum('bhqk,bkhd->bqhd', w.astype(v.dtype), v)


def stick_breaking_attention(q, k, v, q_pos, k_pos):
    bsz, t_q, n_h, d_h = q.shape
    if t_q <= Q_BLOCK:
        return sb_block(q, k, v, q_pos, k_pos)
    n_blk = t_q // Q_BLOCK
    qb = q.reshape(bsz, n_blk, Q_BLOCK, n_h, d_h).transpose(1, 0, 2, 3, 4)
    pb = q_pos.reshape(n_blk, Q_BLOCK)
    ob = lax.map(lambda qp: sb_block(qp[0], k, v, qp[1], k_pos), (qb, pb))
    return ob.transpose(1, 0, 2, 3, 4).reshape(bsz, t_q, n_h, d_h)


def hybrid_layer(x, c, conv_buf, h0, k_past, v_past, w_ada, b_ada, g_pre_mix, g_post_mix,
                 g_pre_mlp, g_post_mlp, w_in, conv_w, conv_b, w_rg_a, b_rg_a, w_rg_x, b_rg_x,
                 lru_lambda, g_lru_out, g_sb_out, w_out, w_up, w_down):
    bsz, t_new, _ = x.shape
    past = k_past.shape[1]
    mod = jnp.einsum('bc,cm->bm', jax.nn.silu(c), w_ada) + b_ada
    sh_a, sc_a, gt_a, sh_f, sc_f, gt_f = jnp.split(mod[:, None, :], N_MOD, axis=-1)

    h = rms_norm(x, g_pre_mix) * (1 + sc_a) + sh_a
    proj = h @ w_in
    xl, gl, q, k, v = jnp.split(proj, [D_LRU, 2 * D_LRU, 2 * D_LRU + D_SB, 2 * D_LRU + 2 * D_SB], axis=-1)
    xc, new_conv = causal_dwconv(xl, conv_buf, conv_w, conv_b)
    hl, h_last = rg_lru(xc, h0, w_rg_a, b_rg_a, w_rg_x, b_rg_x, lru_lambda)
    o_lru = hl * jax.nn.gelu(gl)
    q = q.reshape(bsz, t_new, N_SB_HEADS, SB_HEAD_DIM)
    k = k.reshape(bsz, t_new, N_SB_HEADS, SB_HEAD_DIM)
    v = v.reshape(bsz, t_new, N_SB_HEADS, SB_HEAD_DIM)
    k_all = jnp.concatenate([k_past.astype(k.dtype), k], axis=1)
    v_all = jnp.concatenate([v_past.astype(v.dtype), v], axis=1)
    q_pos = past + jnp.arange(t_new, dtype=jnp.int32)
    k_pos = jnp.arange(past + t_new, dtype=jnp.int32)
    o_sb = stick_breaking_attention(q, k_all, v_all, q_pos, k_pos).reshape(bsz, t_new, D_SB)
    o = jnp.concatenate([rms_norm(o_lru, g_lru_out), rms_norm(o_sb, g_sb_out)], axis=-1) @ w_out
    x = x + gt_a * rms_norm(o, g_post_mix)

    h = rms_norm(x, g_pre_mlp) * (1 + sc_f) + sh_f
    f = jnp.square(jax.nn.relu(h @ w_up)) @ w_down
    x = x + gt_f * rms_norm(f, g_post_mlp)
    return x, new_conv, h_last, k, v


def setup_inputs(seed: int = 0) -> dict:
    assert DEC_SEQ <= CHUNK
    key = jax.random.key(seed)
    ks = jax.random.split(key, 32)
    f32 = jnp.float32
    nrm = lambda kk, shape, s: jax.random.normal(kk, shape, f32) * s
    gain = lambda kk, n: 1.0 + 0.02 * jax.random.normal(kk, (DEPTH, n), f32)
    a_base = jax.random.uniform(ks[20], (DEPTH, D_LRU), f32, 0.9, 0.999)
    s = a_base ** (1.0 / RG_C)
    lru_lambda = jnp.log(s) - jnp.log1p(-s)
    return {
        'x_prompt': nrm(ks[0], (BATCH, SEQ, D_MODEL), 1.0),
        'x_sample': nrm(ks[1], (DEC_BATCH, DEC_SEQ, D_MODEL), 1.0),
        'c_prompt': nrm(ks[2], (BATCH, D_MODEL), 1.0),
        'c_sample': nrm(ks[3], (DEC_BATCH, D_MODEL), 1.0),
        'cache_conv': nrm(ks[4], (DEPTH, DEC_BATCH, CONV_WIDTH - 1, D_LRU), 1.0),
        'state_lru': nrm(ks[5], (DEPTH, DEC_BATCH, D_LRU), 0.5),
        'cache_k': nrm(ks[6], (DEPTH, DEC_BATCH, PAST_LEN, N_SB_HEADS, SB_HEAD_DIM), 1.0),
        'cache_v': nrm(ks[7], (DEPTH, DEC_BATCH, PAST_LEN, N_SB_HEADS, SB_HEAD_DIM), 1.0),
        'w_ada': nrm(ks[8], (DEPTH, D_MODEL, N_MOD * D_MODEL), 0.5 * D_MODEL ** -0.5),
        'b_ada': nrm(ks[9], (DEPTH, N_MOD * D_MODEL), 0.01),
        'g_pre_mix': gain(ks[10], D_MODEL),
        'g_post_mix': gain(ks[11], D_MODEL),
        'g_pre_mlp': gain(ks[12], D_MODEL),
        'g_post_mlp': gain(ks[13], D_MODEL),
        'w_in': nrm(ks[14], (DEPTH, D_MODEL, D_IN), D_MODEL ** -0.5),
        'conv_w': nrm(ks[15], (DEPTH, CONV_WIDTH, D_LRU), CONV_WIDTH ** -0.5),
        'conv_b': nrm(ks[16], (DEPTH, D_LRU), 0.01),
        'w_rg_a': nrm(ks[17], (DEPTH, N_LRU_BLOCKS, LRU_BLOCK, LRU_BLOCK), LRU_BLOCK ** -0.5),
        'b_rg_a': nrm(ks[18], (DEPTH, N_LRU_BLOCKS, LRU_BLOCK), 0.01),
        'w_rg_x': nrm(ks[19], (DEPTH, N_LRU_BLOCKS, LRU_BLOCK, LRU_BLOCK), LRU_BLOCK ** -0.5),
        'b_rg_x': nrm(ks[21], (DEPTH, N_LRU_BLOCKS, LRU_BLOCK), 0.01),
        'lru_lambda': lru_lambda,
        'g_lru_out': gain(ks[22], D_LRU),
        'g_sb_out': gain(ks[23], D_SB),
        'w_out': nrm(ks[24], (DEPTH, D_MODEL, D_MODEL), D_MODEL ** -0.5),
        'w_up': nrm(ks[25], (DEPTH, D_MODEL, D_FF), D_MODEL ** -0.5),
        'w_down': nrm(ks[26], (DEPTH, D_FF, D_MODEL), D_FF ** -0.5),
    }


def reference(x_prompt, x_sample, c_prompt, c_sample, cache_conv, state_lru, cache_k, cache_v,
              w_ada, b_ada, g_pre_mix, g_post_mix, g_pre_mlp, g_post_mlp, w_in, conv_w, conv_b,
              w_rg_a, b_rg_a, w_rg_x, b_rg_x, lru_lambda, g_lru_out, g_sb_out, w_out, w_up, w_down):
    yp, ys = x_prompt, x_sample
    bp = x_prompt.shape[0]
    conv_ps, h_ps, k_ps, v_ps = [], [], [], []
    conv_ss, h_ss, k_ss, v_ss = [], [], [], []
    for l in range(DEPTH):
        wl = (w_ada[l], b_ada[l], g_pre_mix[l], g_post_mix[l], g_pre_mlp[l], g_post_mlp[l], w_in[l],
              conv_w[l], conv_b[l], w_rg_a[l], b_rg_a[l], w_rg_x[l], b_rg_x[l], lru_lambda[l],
              g_lru_out[l], g_sb_out[l], w_out[l], w_up[l], w_down[l])
        empty_kv = jnp.zeros((bp, 0, N_SB_HEADS, SB_HEAD_DIM), yp.dtype)
        yp, cp, hp, kp, vp = hybrid_layer(
            yp, c_prompt, jnp.zeros((bp, CONV_WIDTH - 1, D_LRU), yp.dtype),
            jnp.zeros((bp, D_LRU), yp.dtype), empty_kv, empty_kv, *wl)
        ys, cs, hs, k_new, v_new = hybrid_layer(
            ys, c_sample, cache_conv[l], state_lru[l], cache_k[l], cache_v[l], *wl)
        conv_ps.append(cp); h_ps.append(hp); k_ps.append(kp); v_ps.append(vp)
        conv_ss.append(cs); h_ss.append(hs); k_ss.append(k_new); v_ss.append(v_new)
    conv_prompt = jnp.stack(conv_ps)
    lru_prompt = jnp.stack(h_ps)
    k_prompt = jnp.stack(k_ps)
    v_prompt = jnp.stack(v_ps)
    conv_sample = jnp.stack(conv_ss)
    lru_sample = jnp.stack(h_ss)
    k_sample = jnp.stack(k_ss)
    v_sample = jnp.stack(v_ss)
    return (yp, ys, conv_prompt, lru_prompt, k_prompt, v_prompt, conv_sample, lru_sample, k_sample, v_sample)
```

```python
import functools

import jax
import jax.numpy as jnp
from jax import lax
from jax.experimental import pallas as pl
from jax.experimental.pallas import tpu as pltpu

D_MODEL = 1024
D_LRU = 512
D_SB = 512
N_SB_HEADS = 8
SB_HEAD_DIM = 64
HEAD_PAIR = 2 * SB_HEAD_DIM
N_PAIRS = N_SB_HEADS // 2
CONV_WIDTH = 4
CONV_CARRY_ROWS = 8
RG_C = 8.0
D_FF = 4 * D_MODEL
D_IN = 2 * D_LRU + 3 * D_SB
N_MOD = 6
EPS = 1e-6

VMEM_LIMIT_BYTES = 56 * 1024 * 1024

BF16 = jnp.bfloat16
F32 = jnp.float32


def _rms(x, g):
    return x * lax.rsqrt(jnp.mean(x * x, axis=-1, keepdims=True) + EPS) * g


def _dot(a, b):
    return jnp.dot(a, b, preferred_element_type=F32)


def _adaln_kernel(c_ref, w_ref, b_ref, o_ref):
    c = c_ref[...]
    s = (c * jax.nn.sigmoid(c)).astype(BF16)
    o_ref[...] = _dot(s, w_ref[...].astype(BF16)) + b_ref[...]


def _adaln(c_all, w_ada, b_ada, *, tn=1536):
    n_rows = c_all.shape[0]
    n_out = w_ada.shape[1]
    return pl.pallas_call(
        _adaln_kernel,
        grid=(n_out // tn,),
        in_specs=[
            pl.BlockSpec((n_rows, D_MODEL), lambda j: (0, 0)),
            pl.BlockSpec((D_MODEL, tn), lambda j: (0, j)),
            pl.BlockSpec((1, tn), lambda j: (0, j)),
        ],
        out_specs=pl.BlockSpec((n_rows, tn), lambda j: (0, j)),
        out_shape=jax.ShapeDtypeStruct((n_rows, n_out), F32),
        compiler_params=pltpu.CompilerParams(
            dimension_semantics=("arbitrary",), vmem_limit_bytes=VMEM_LIMIT_BYTES),
        name="adaln",
    )(c_all, w_ada, b_ada.reshape(1, n_out))


def _shift_rows(x, s, fill, row):
    return jnp.where(row >= s, pltpu.roll(x, s, axis=1), fill)


def _front_kernel(x_ref, mod_ref, conv0_ref, h0_ref, g_pre_ref, w_in_ref, conv_w_ref, conv_b_ref,
                  w_gate_ref, b_gate_ref, lam_ref, g_lru_ref,
                  olru_ref, q_ref, kbf_ref, vbf_ref, k_ref, v_ref, conv_out_ref, h_out_ref,
                  prev_ref, hc_ref, *, n_seq, n_rows):
    S, L = n_seq, n_rows

    @pl.when(pl.program_id(0) == 0)
    def _():
        prev_ref[...] = conv0_ref[...]
        hc_ref[...] = h0_ref[...]

    x = x_ref[...]
    shift = mod_ref[:, 0:1, :]
    scale = mod_ref[:, 1:2, :]
    h = _rms(x, g_pre_ref[...]) * (1.0 + scale) + shift
    proj = _dot(h.reshape(S * L, D_MODEL).astype(BF16), w_in_ref[...])
    xl = proj[:, 0:D_LRU].reshape(S, L, D_LRU)
    gl = proj[:, D_LRU:2 * D_LRU].reshape(S, L, D_LRU)
    q = proj[:, 2 * D_LRU:2 * D_LRU + D_SB]
    k = proj[:, 2 * D_LRU + D_SB:2 * D_LRU + 2 * D_SB]
    v = proj[:, 2 * D_LRU + 2 * D_SB:]

    k_ref[...] = k.reshape(S, L, D_SB)
    v_ref[...] = v.reshape(S, L, D_SB)
    qs = (q * (SB_HEAD_DIM ** -0.5)).astype(BF16)
    kb = k.astype(BF16)
    vb = v.astype(BF16)
    for p in range(N_PAIRS):
        cols = slice(p * HEAD_PAIR, (p + 1) * HEAD_PAIR)
        q_ref[p] = qs[:, cols].reshape(S, L, HEAD_PAIR)
        kbf_ref[p] = kb[:, cols].reshape(S, L, HEAD_PAIR)
        vbf_ref[p] = vb[:, cols].reshape(S, L, HEAD_PAIR)

    ext = jnp.concatenate([prev_ref[...], xl], axis=1)
    cw = conv_w_ref[...]
    xc = conv_b_ref[...]
    for j in range(CONV_WIDTH - 1):
        back = CONV_WIDTH - 1 - j
        xc = xc + pltpu.roll(ext, back, axis=1)[:, CONV_CARRY_ROWS:, :] * cw[j:j + 1, :]
    xc = xc + xl * cw[CONV_WIDTH - 1:CONV_WIDTH, :]
    prev_ref[...] = xl[:, L - CONV_CARRY_ROWS:, :]
    conv_out_ref[...] = ext[:, L + CONV_CARRY_ROWS - (CONV_WIDTH - 1):, :]

    gates = _dot(xc.reshape(S * L, D_LRU).astype(BF16), w_gate_ref[...]) + b_gate_ref[...]
    r = jax.nn.sigmoid(gates[:, 0:D_LRU]).reshape(S, L, D_LRU)
    i = jax.nn.sigmoid(gates[:, D_LRU:]).reshape(S, L, D_LRU)
    neg_lam = -lam_ref[...]
    softplus_neg_lam = jnp.maximum(neg_lam, 0.0) + jnp.log1p(jnp.exp(-jnp.abs(neg_lam)))
    log_a = (-RG_C * r) * softplus_neg_lam
    a = jnp.exp(log_a)
    u = jnp.sqrt(jnp.tanh(-log_a) * (1.0 + a * a)) * (i * xc)
    row = lax.broadcasted_iota(jnp.int32, (S, L, D_LRU), 1)
    u = u + jnp.where(row == 0, a * hc_ref[...], 0.0)

    b = u
    step = 1
    while step < L:
        b = a * _shift_rows(b, step, 0.0, row) + b
        if 2 * step < L:
            a = a * _shift_rows(a, step, 1.0, row)
        step *= 2
    h_last = b[:, L - 1:L, :]
    hc_ref[...] = h_last
    h_out_ref[...] = h_last

    o = b * jax.nn.gelu(gl)
    olru_ref[...] = _rms(o, g_lru_ref[...]).astype(BF16)


def _front(x, mod, conv0, h0, g_pre, w_in_bf, conv_w, conv_b, w_gate_bf, b_gate, lam, g_lru,
           *, n_rows, name):
    S, T, _ = x.shape
    L = n_rows
    n_t = T // L
    const2 = lambda t: (0, 0)
    const3 = lambda t: (0, 0, 0)
    tile3 = lambda t: (0, t, 0)
    tile4 = lambda t: (0, 0, t, 0)
    kernel = functools.partial(_front_kernel, n_seq=S, n_rows=L)
    return pl.pallas_call(
        kernel,
        grid=(n_t,),
        in_specs=[
            pl.BlockSpec((S, L, D_MODEL), tile3),
            pl.BlockSpec((S, N_MOD, D_MODEL), const3),
            pl.BlockSpec((S, CONV_CARRY_ROWS, D_LRU), const3),
            pl.BlockSpec((S, 1, D_LRU), const3),
            pl.BlockSpec((1, D_MODEL), const2),
            pl.BlockSpec((D_MODEL, D_IN), const2),
            pl.BlockSpec((CONV_WIDTH, D_LRU), const2),
            pl.BlockSpec((1, D_LRU), const2),
            pl.BlockSpec((D_LRU, 2 * D_LRU), const2),
            pl.BlockSpec((1, 2 * D_LRU), const2),
            pl.BlockSpec((1, D_LRU), const2),
            pl.BlockSpec((1, D_LRU), const2),
        ],
        out_specs=[
            pl.BlockSpec((S, L, D_LRU), tile3),
            pl.BlockSpec((N_PAIRS, S, L, HEAD_PAIR), tile4),
            pl.BlockSpec((N_PAIRS, S, L, HEAD_PAIR), tile4),
            pl.BlockSpec((N_PAIRS, S, L, HEAD_PAIR), tile4),
            pl.BlockSpec((S, L, D_SB), tile3),
            pl.BlockSpec((S, L, D_SB), tile3),
            pl.BlockSpec((S, CONV_WIDTH - 1, D_LRU), const3),
            pl.BlockSpec((S, 1, D_LRU), const3),
        ],
        out_shape=[
            jax.ShapeDtypeStruct((S, T, D_LRU), BF16),
            jax.ShapeDtypeStruct((N_PAIRS, S, T, HEAD_PAIR), BF16),
            jax.ShapeDtypeStruct((N_PAIRS, S, T, HEAD_PAIR), BF16),
            jax.ShapeDtypeStruct((N_PAIRS, S, T, HEAD_PAIR), BF16),
            jax.ShapeDtypeStruct((S, T, D_SB), F32),
            jax.ShapeDtypeStruct((S, T, D_SB), F32),
            jax.ShapeDtypeStruct((S, CONV_WIDTH - 1, D_LRU), F32),
            jax.ShapeDtypeStruct((S, 1, D_LRU), F32),
        ],
        scratch_shapes=[
            pltpu.VMEM((S, CONV_CARRY_ROWS, D_LRU), F32),
            pltpu.VMEM((S, 1, D_LRU), F32),
        ],
        compiler_params=pltpu.CompilerParams(
            dimension_semantics=("arbitrary",), vmem_limit_bytes=VMEM_LIMIT_BYTES),
        name=name,
    )(x, mod, conv0, h0, g_pre, w_in_bf, conv_w, conv_b, w_gate_bf, b_gate, lam, g_lru)


def _strict_upper_ones(n):
    j = lax.broadcasted_iota(jnp.int32, (n, n), 0)
    s = lax.broadcasted_iota(jnp.int32, (n, n), 1)
    return jnp.where(j > s, 1.0, 0.0).astype(BF16)


def _sb_head_tile(qh, kt, vh, carry, tri, mask):
    z = lax.dot_general(qh, kt, (((1,), (1,)), ((), ())), preferred_element_type=F32)
    log_beta = jnp.minimum(z, 0.0) - jnp.log(1.0 + jnp.exp(-jnp.abs(z)))
    log_keep = log_beta - z
    if mask is not None:
        log_keep = jnp.where(mask, log_keep, 0.0)
    hi = log_keep.astype(BF16)
    lo = (log_keep - hi.astype(F32)).astype(BF16)
    between = _dot(hi, tri) + _dot(lo, tri)
    w = jnp.exp(log_beta + between + carry)
    if mask is not None:
        w = jnp.where(mask, w, 0.0)
    pv = _dot(w.astype(BF16), vh)
    return pv, carry + jnp.sum(log_keep, axis=-1, keepdims=True)


def _attn_kernel(*refs, tq, past_len, past_tile):
    if past_len:
        q_ref, k_ref, v_ref, kp_ref, vp_ref, o_ref = refs
    else:
        q_ref, k_ref, v_ref, o_ref = refs
    i = pl.program_id(2)

    q = q_ref[...]
    q_lane = lax.broadcasted_iota(jnp.int32, (tq, HEAD_PAIR), 1)
    q_heads = (jnp.where(q_lane < SB_HEAD_DIM, q, jnp.zeros_like(q)),
               jnp.where(q_lane >= SB_HEAD_DIM, q, jnp.zeros_like(q)))

    def sweep(kt, vt, state, tri, mask):
        acc, carries = state
        v_lane = lax.broadcasted_iota(jnp.int32, vt.shape, 1)
        v_heads = (jnp.where(v_lane < SB_HEAD_DIM, vt, jnp.zeros_like(vt)),
                   jnp.where(v_lane >= SB_HEAD_DIM, vt, jnp.zeros_like(vt)))
        new_carries = []
        for qh, vh, carry in zip(q_heads, v_heads, carries):
            pv, carry = _sb_head_tile(qh, kt, vh, carry, tri, mask)
            acc = acc + pv
            new_carries.append(carry)
        return acc, tuple(new_carries)

    zero_carry = jnp.zeros((tq, 1), F32)
    state = (jnp.zeros((tq, HEAD_PAIR), F32), (zero_carry, zero_carry))

    tri = _strict_upper_ones(tq)
    t_idx = lax.broadcasted_iota(jnp.int32, (tq, tq), 0)
    s_idx = lax.broadcasted_iota(jnp.int32, (tq, tq), 1)
    diag = pl.ds(pl.multiple_of(i * tq, tq), tq)
    state = sweep(k_ref[diag, :], v_ref[diag, :], state, tri, s_idx < t_idx)

    def new_body(n, state):
        rows = pl.ds(pl.multiple_of((i - 1 - n) * tq, tq), tq)
        return sweep(k_ref[rows, :], v_ref[rows, :], state, tri, None)

    state = lax.fori_loop(0, i, new_body, state)

    if past_len:
        tri_p = _strict_upper_ones(past_tile)

        def past_body(n, state):
            rows = pl.ds(pl.multiple_of(past_len - (n + 1) * past_tile, past_tile), past_tile)
            return sweep(kp_ref[rows, :].astype(BF16), vp_ref[rows, :].astype(BF16), state, tri_p, None)

        state = lax.fori_loop(0, past_len // past_tile, past_body, state)

    o_ref[...] = state[0]


def _attention(q, kbf, vbf, k_past, v_past, *, tq, past_tile, name):
    _, B, T, _ = q.shape
    past_len = 0 if k_past is None else k_past.shape[1]
    n_q = T // tq
    q_spec = pl.BlockSpec((None, None, tq, HEAD_PAIR), lambda p, b, i: (p, b, i, 0))
    kv_spec = pl.BlockSpec((None, None, T, HEAD_PAIR), lambda p, b, i: (p, b, 0, 0))
    in_specs = [q_spec, kv_spec, kv_spec]
    args = [q, kbf, vbf]
    if past_len:
        past_spec = pl.BlockSpec((None, past_len, HEAD_PAIR), lambda p, b, i: (b, 0, p))
        in_specs += [past_spec, past_spec]
        args += [k_past, v_past]
    kernel = functools.partial(_attn_kernel, tq=tq, past_len=past_len, past_tile=past_tile)
    return pl.pallas_call(
        kernel,
        grid=(N_PAIRS, B, n_q),
        in_specs=in_specs,
        out_specs=pl.BlockSpec((None, tq, HEAD_PAIR), lambda p, b, i: (b, i, p)),
        out_shape=jax.ShapeDtypeStruct((B, T, D_SB), F32),
        compiler_params=pltpu.CompilerParams(
            dimension_semantics=("arbitrary", "arbitrary", "arbitrary"),
            vmem_limit_bytes=VMEM_LIMIT_BYTES),
        name=name,
    )(*args)


def _back_kernel(x_ref, olru_ref, osb_ref, mod_ref, g_sb_ref, g_post_mix_ref, g_pre_mlp_ref,
                 g_post_mlp_ref, w_out_ref, w_up_ref, w_down_ref, y_ref, *, n_seq, n_rows, ff_chunk):
    S, L = n_seq, n_rows
    gate_a = mod_ref[:, 2:3, :]
    shift_f = mod_ref[:, 3:4, :]
    scale_f = mod_ref[:, 4:5, :]
    gate_f = mod_ref[:, 5:6, :]

    o_lru = olru_ref[...].reshape(S * L, D_LRU)
    o_sb = _rms(osb_ref[...], g_sb_ref[...]).reshape(S * L, D_SB).astype(BF16)
    o = _dot(o_lru, w_out_ref[0:D_LRU, :]) + _dot(o_sb, w_out_ref[D_LRU:, :])
    x = x_ref[...] + gate_a * _rms(o, g_post_mix_ref[...]).reshape(S, L, D_MODEL)

    h = _rms(x, g_pre_mlp_ref[...]) * (1.0 + scale_f) + shift_f
    hb = h.reshape(S * L, D_MODEL).astype(BF16)
    f = jnp.zeros((S * L, D_MODEL), F32)
    for c in range(D_FF // ff_chunk):
        cols = slice(c * ff_chunk, (c + 1) * ff_chunk)
        up = jnp.maximum(_dot(hb, w_up_ref[:, cols]), 0.0)
        f = f + _dot((up * up).astype(BF16), w_down_ref[cols, :])
    y_ref[...] = x + gate_f * _rms(f, g_post_mlp_ref[...]).reshape(S, L, D_MODEL)


def _back(x, o_lru, o_sb, mod, g_sb, g_post_mix, g_pre_mlp, g_post_mlp, w_out_bf, w_up_bf, w_down_bf,
          *, n_seq, n_rows, name, ff_chunk=1024):
    B, T, _ = x.shape
    S, L = n_seq, n_rows
    const2 = lambda b, t: (0, 0)
    tile3 = lambda b, t: (b, t, 0)
    seq3 = lambda b, t: (b, 0, 0)
    kernel = functools.partial(_back_kernel, n_seq=S, n_rows=L, ff_chunk=ff_chunk)
    return pl.pallas_call(
        kernel,
        grid=(B // S, T // L),
        in_specs=[
            pl.BlockSpec((S, L, D_MODEL), tile3),
            pl.BlockSpec((S, L, D_LRU), tile3),
            pl.BlockSpec((S, L, D_SB), tile3),
            pl.BlockSpec((S, N_MOD, D_MODEL), seq3),
            pl.BlockSpec((1, D_SB), const2),
            pl.BlockSpec((1, D_MODEL), const2),
            pl.BlockSpec((1, D_MODEL), const2),
            pl.BlockSpec((1, D_MODEL), const2),
            pl.BlockSpec((D_MODEL, D_MODEL), const2),
            pl.BlockSpec((D_MODEL, D_FF), const2),
            pl.BlockSpec((D_FF, D_MODEL), const2),
        ],
        out_specs=pl.BlockSpec((S, L, D_MODEL), tile3),
        out_shape=jax.ShapeDtypeStruct((B, T, D_MODEL), F32),
        compiler_params=pltpu.CompilerParams(
            dimension_semantics=("arbitrary", "arbitrary"), vmem_limit_bytes=VMEM_LIMIT_BYTES),
        name=name,
    )(x, o_lru, o_sb, mod, g_sb, g_post_mix, g_pre_mlp, g_post_mlp, w_out_bf, w_up_bf, w_down_bf)


def _block_diag(w):
    n, bi, bj = w.shape
    eye = jnp.eye(n, dtype=w.dtype)
    return (eye[:, None, :, None] * w[:, :, None, :]).reshape(n * bi, n * bj)


def _pad_conv_state(buf):
    return jnp.pad(buf, ((0, 0), (CONV_CARRY_ROWS - (CONV_WIDTH - 1), 0), (0, 0)))


def kernel(x_prompt, x_sample, c_prompt, c_sample, cache_conv, state_lru, cache_k, cache_v, w_ada, b_ada, g_pre_mix, g_post_mix, g_pre_mlp, g_post_mlp, w_in, conv_w, conv_b, w_rg_a, b_rg_a, w_rg_x, b_rg_x, lru_lambda, g_lru_out, g_sb_out, w_out, w_up, w_down):
    depth = w_ada.shape[0]
    bp, seq, _ = x_prompt.shape
    bs, dec_seq, _ = x_sample.shape
    assert depth == 1 and bp == 1, "single-layer, single-prompt configuration"
    past_len = cache_k.shape[2]
    l = 0

    row = lambda a: a.reshape(1, -1)
    w_in_bf = w_in[l].astype(BF16)
    w_gate_bf = jnp.concatenate([_block_diag(w_rg_a[l]), _block_diag(w_rg_x[l])], axis=1).astype(BF16)
    b_gate = jnp.concatenate([b_rg_a[l].reshape(-1), b_rg_x[l].reshape(-1)]).reshape(1, -1)
    w_out_bf = w_out[l].astype(BF16)
    w_up_bf = w_up[l].astype(BF16)
    w_down_bf = w_down[l].astype(BF16)
    front_w = (row(g_pre_mix[l]), w_in_bf, conv_w[l], row(conv_b[l]), w_gate_bf, b_gate,
               row(lru_lambda[l]), row(g_lru_out[l]))
    back_w = (row(g_sb_out[l]), row(g_post_mix[l]), row(g_pre_mlp[l]), row(g_post_mlp[l]),
              w_out_bf, w_up_bf, w_down_bf)

    mod = _adaln(jnp.concatenate([c_prompt, c_sample], axis=0), w_ada[l], b_ada[l])
    mod = mod.reshape(bp + bs, N_MOD, D_MODEL)
    mod_p, mod_s = mod[:bp], mod[bp:]

    zeros_conv = jnp.zeros((bp, CONV_CARRY_ROWS, D_LRU), F32)
    zeros_h = jnp.zeros((bp, 1, D_LRU), F32)
    olru_p, q_p, kbf_p, vbf_p, k_p, v_p, conv_p, h_p = _front(
        x_prompt, mod_p, zeros_conv, zeros_h, *front_w, n_rows=512, name="front_prompt")
    osb_p = _attention(q_p, kbf_p, vbf_p, None, None, tq=256, past_tile=256, name="attn_prompt")
    y_p = _back(x_prompt, olru_p, osb_p, mod_p, *back_w, n_seq=1, n_rows=512, name="back_prompt")

    olru_s, q_s, kbf_s, vbf_s, k_s, v_s, conv_s, h_s = _front(
        x_sample, mod_s, _pad_conv_state(cache_conv[l]), state_lru[l].reshape(bs, 1, D_LRU),
        *front_w, n_rows=dec_seq, name="front_sample")
    osb_s = _attention(q_s, kbf_s, vbf_s, cache_k[l].reshape(bs, past_len, D_SB),
                       cache_v[l].reshape(bs, past_len, D_SB), tq=dec_seq, past_tile=256,
                       name="attn_sample")
    y_s = _back(x_sample, olru_s, osb_s, mod_s, *back_w, n_seq=8, n_rows=dec_seq, name="back_sample")

    heads = lambda a: a.reshape(1, a.shape[0], a.shape[1], N_SB_HEADS, SB_HEAD_DIM)
    return (y_p, y_s, conv_p[None], h_p.reshape(1, bp, D_LRU), heads(k_p), heads(v_p),
            conv_s[None], h_s.reshape(1, bs, D_LRU), heads(k_s), heads(v_s))
```

```python
import functools

import jax
import jax.numpy as jnp
from jax import lax
from jax.experimental import pallas as pl
from jax.experimental.pallas import tpu as pltpu

D_MODEL = 1024
D_LRU = 512
D_SB = 512
N_SB_HEADS = 8
SB_HEAD_DIM = 64
HEAD_PAIR = 2 * SB_HEAD_DIM
N_PAIRS = N_SB_HEADS // 2
CONV_WIDTH = 4
CONV_CARRY_ROWS = 8
RG_C = 8.0
D_FF = 4 * D_MODEL
D_IN = 2 * D_LRU + 3 * D_SB
N_MOD = 6
EPS = 1e-6

KEY_TILE = 256
LOG_WEIGHT_FLOOR = -110.0

VMEM_LIMIT_BYTES = 56 * 1024 * 1024

BF16 = jnp.bfloat16
F32 = jnp.float32


def _rms(x, g):
    return x * lax.rsqrt(jnp.mean(x * x, axis=-1, keepdims=True) + EPS) * g


def _dot(a, b):
    return jnp.dot(a, b, preferred_element_type=F32)


def _adaln_kernel(c_ref, w_ref, b_ref, o_ref):
    c = c_ref[...]
    s = (c * jax.nn.sigmoid(c)).astype(BF16)
    o_ref[...] = _dot(s, w_ref[...].astype(BF16)) + b_ref[...]


def _adaln(c_all, w_ada, b_ada, *, tn=1536):
    n_rows = c_all.shape[0]
    n_out = w_ada.shape[1]
    return pl.pallas_call(
        _adaln_kernel,
        grid=(n_out // tn,),
        in_specs=[
            pl.BlockSpec((n_rows, D_MODEL), lambda j: (0, 0)),
            pl.BlockSpec((D_MODEL, tn), lambda j: (0, j)),
            pl.BlockSpec((1, tn), lambda j: (0, j)),
        ],
        out_specs=pl.BlockSpec((n_rows, tn), lambda j: (0, j)),
        out_shape=jax.ShapeDtypeStruct((n_rows, n_out), F32),
        compiler_params=pltpu.CompilerParams(
            dimension_semantics=("arbitrary",), vmem_limit_bytes=VMEM_LIMIT_BYTES),
        name="adaln",
    )(c_all, w_ada, b_ada.reshape(1, n_out))


def _shift_rows(x, s, fill, row):
    return jnp.where(row >= s, pltpu.roll(x, s, axis=1), fill)


def _front_kernel(x_ref, mod_ref, conv0_ref, h0_ref, g_pre_ref, w_in_ref, conv_w_ref, conv_b_ref,
                  w_gate_ref, b_gate_ref, lam_ref, g_lru_ref,
                  olru_ref, q_ref, kbf_ref, vbf_ref, k_ref, v_ref, conv_out_ref, h_out_ref,
                  prev_ref, hc_ref, *, n_seq, n_rows):
    S, L = n_seq, n_rows

    @pl.when(pl.program_id(0) == 0)
    def _():
        prev_ref[...] = conv0_ref[...]
        hc_ref[...] = h0_ref[...]

    x = x_ref[...]
    shift = mod_ref[:, 0:1, :]
    scale = mod_ref[:, 1:2, :]
    h = _rms(x, g_pre_ref[...]) * (1.0 + scale) + shift
    proj = _dot(h.reshape(S * L, D_MODEL).astype(BF16), w_in_ref[...])
    xl = proj[:, 0:D_LRU].reshape(S, L, D_LRU)
    gl = proj[:, D_LRU:2 * D_LRU].reshape(S, L, D_LRU)
    q = proj[:, 2 * D_LRU:2 * D_LRU + D_SB]
    k = proj[:, 2 * D_LRU + D_SB:2 * D_LRU + 2 * D_SB]
    v = proj[:, 2 * D_LRU + 2 * D_SB:]

    k_ref[...] = k.reshape(S, L, D_SB)
    v_ref[...] = v.reshape(S, L, D_SB)
    q_ref[...] = (q * (SB_HEAD_DIM ** -0.5)).astype(BF16).reshape(S, L, D_SB)
    kbf_ref[...] = k.astype(BF16).reshape(S, L, D_SB)
    vbf_ref[...] = v.astype(BF16).reshape(S, L, D_SB)

    ext = jnp.concatenate([prev_ref[...], xl], axis=1)
    cw = conv_w_ref[...]
    xc = conv_b_ref[...]
    for j in range(CONV_WIDTH - 1):
        back = CONV_WIDTH - 1 - j
        xc = xc + pltpu.roll(ext, back, axis=1)[:, CONV_CARRY_ROWS:, :] * cw[j:j + 1, :]
    xc = xc + xl * cw[CONV_WIDTH - 1:CONV_WIDTH, :]
    prev_ref[...] = xl[:, L - CONV_CARRY_ROWS:, :]
    conv_out_ref[...] = ext[:, L + CONV_CARRY_ROWS - (CONV_WIDTH - 1):, :]

    gates = _dot(xc.reshape(S * L, D_LRU).astype(BF16), w_gate_ref[...]) + b_gate_ref[...]
    r = jax.nn.sigmoid(gates[:, 0:D_LRU]).reshape(S, L, D_LRU)
    i = jax.nn.sigmoid(gates[:, D_LRU:]).reshape(S, L, D_LRU)
    neg_lam = -lam_ref[...]
    softplus_neg_lam = jnp.maximum(neg_lam, 0.0) + jnp.log1p(jnp.exp(-jnp.abs(neg_lam)))
    log_a = (-RG_C * r) * softplus_neg_lam
    a = jnp.exp(log_a)
    u = jnp.sqrt(jnp.tanh(-log_a) * (1.0 + a * a)) * (i * xc)
    row = lax.broadcasted_iota(jnp.int32, (S, L, D_LRU), 1)
    u = u + jnp.where(row == 0, a * hc_ref[...], 0.0)

    b = u
    step = 1
    while step < L:
        b = a * _shift_rows(b, step, 0.0, row) + b
        if 2 * step < L:
            a = a * _shift_rows(a, step, 1.0, row)
        step *= 2
    h_last = b[:, L - 1:L, :]
    hc_ref[...] = h_last
    h_out_ref[...] = h_last

    o = b * jax.nn.gelu(gl)
    olru_ref[...] = _rms(o, g_lru_ref[...]).astype(BF16)


def _front(x, mod, conv0, h0, g_pre, w_in_bf, conv_w, conv_b, w_gate_bf, b_gate, lam, g_lru,
           *, n_rows, name):
    S, T, _ = x.shape
    L = n_rows
    n_t = T // L
    const2 = lambda t: (0, 0)
    const3 = lambda t: (0, 0, 0)
    tile3 = lambda t: (0, t, 0)
    kernel = functools.partial(_front_kernel, n_seq=S, n_rows=L)
    return pl.pallas_call(
        kernel,
        grid=(n_t,),
        in_specs=[
            pl.BlockSpec((S, L, D_MODEL), tile3),
            pl.BlockSpec((S, N_MOD, D_MODEL), const3),
            pl.BlockSpec((S, CONV_CARRY_ROWS, D_LRU), const3),
            pl.BlockSpec((S, 1, D_LRU), const3),
            pl.BlockSpec((1, D_MODEL), const2),
            pl.BlockSpec((D_MODEL, D_IN), const2),
            pl.BlockSpec((CONV_WIDTH, D_LRU), const2),
            pl.BlockSpec((1, D_LRU), const2),
            pl.BlockSpec((D_LRU, 2 * D_LRU), const2),
            pl.BlockSpec((1, 2 * D_LRU), const2),
            pl.BlockSpec((1, D_LRU), const2),
            pl.BlockSpec((1, D_LRU), const2),
        ],
        out_specs=[
            pl.BlockSpec((S, L, D_LRU), tile3),
            pl.BlockSpec((S, L, D_SB), tile3),
            pl.BlockSpec((S, L, D_SB), tile3),
            pl.BlockSpec((S, L, D_SB), tile3),
            pl.BlockSpec((S, L, D_SB), tile3),
            pl.BlockSpec((S, L, D_SB), tile3),
            pl.BlockSpec((S, CONV_WIDTH - 1, D_LRU), const3),
            pl.BlockSpec((S, 1, D_LRU), const3),
        ],
        out_shape=[
            jax.ShapeDtypeStruct((S, T, D_LRU), BF16),
            jax.ShapeDtypeStruct((S, T, D_SB), BF16),
            jax.ShapeDtypeStruct((S, T, D_SB), BF16),
            jax.ShapeDtypeStruct((S, T, D_SB), BF16),
            jax.ShapeDtypeStruct((S, T, D_SB), F32),
            jax.ShapeDtypeStruct((S, T, D_SB), F32),
            jax.ShapeDtypeStruct((S, CONV_WIDTH - 1, D_LRU), F32),
            jax.ShapeDtypeStruct((S, 1, D_LRU), F32),
        ],
        scratch_shapes=[
            pltpu.VMEM((S, CONV_CARRY_ROWS, D_LRU), F32),
            pltpu.VMEM((S, 1, D_LRU), F32),
        ],
        compiler_params=pltpu.CompilerParams(
            dimension_semantics=("arbitrary",), vmem_limit_bytes=VMEM_LIMIT_BYTES),
        name=name,
    )(x, mod, conv0, h0, g_pre, w_in_bf, conv_w, conv_b, w_gate_bf, b_gate, lam, g_lru)


def _strict_upper_ones(n):
    j = lax.broadcasted_iota(jnp.int32, (n, n), 0)
    s = lax.broadcasted_iota(jnp.int32, (n, n), 1)
    return jnp.where(j > s, 1.0, 0.0).astype(BF16)


def _head_lanes(x):
    lane = lax.broadcasted_iota(jnp.int32, x.shape, 1)
    zero = jnp.zeros_like(x)
    return jnp.where(lane < SB_HEAD_DIM, x, zero), jnp.where(lane >= SB_HEAD_DIM, x, zero)


def _sb_head_tile(qh, kt, vh, carry, tri, mask):
    z = lax.dot_general(qh, kt, (((1,), (1,)), ((), ())), preferred_element_type=F32)
    log_beta = jnp.minimum(z, 0.0) - jnp.log(1.0 + jnp.exp(-jnp.abs(z)))
    log_keep = log_beta - z
    if mask is not None:
        log_keep = jnp.where(mask, log_keep, 0.0)
    hi = log_keep.astype(BF16)
    lo = (log_keep - hi.astype(F32)).astype(BF16)
    between = _dot(hi, tri) + _dot(lo, tri)
    w = jnp.exp(log_beta + between + carry)
    if mask is not None:
        w = jnp.where(mask, w, 0.0)
    pv = _dot(w.astype(BF16), vh)
    return pv, carry + jnp.sum(log_keep, axis=-1, keepdims=True)


def _sweep_tile(q_heads, kt, vt, state, tri, mask):
    accs, carries = state
    new_accs, new_carries = [], []
    for p in range(N_PAIRS):
        cols = slice(p * HEAD_PAIR, (p + 1) * HEAD_PAIR)
        k_pair = kt[:, cols]
        acc = accs[p]
        for h, vh in enumerate(_head_lanes(vt[:, cols])):
            head = 2 * p + h
            pv, carry = _sb_head_tile(q_heads[head], k_pair, vh, carries[head], tri, mask)
            acc = acc + pv
            new_carries.append(carry)
        new_accs.append(acc)
    return tuple(new_accs), tuple(new_carries)


def _attn_kernel(q_ref, kd_ref, vd_ref, kp_ref, vp_ref, ksrc_ref, vsrc_ref, o_ref,
                 kt_ref, vt_ref, sem, *, tq, n_before):
    b = pl.program_id(0)
    i = pl.program_id(1)
    before = i * tq if n_before is None else n_before

    q = q_ref[...]
    q_heads = []
    for p in range(N_PAIRS):
        q_heads += _head_lanes(q[:, p * HEAD_PAIR:(p + 1) * HEAD_PAIR])

    zero_carry = jnp.zeros((tq, 1), F32)
    state = (tuple(jnp.zeros((tq, HEAD_PAIR), F32) for _ in range(N_PAIRS)),
             tuple(zero_carry for _ in range(N_SB_HEADS)))
    t_idx = lax.broadcasted_iota(jnp.int32, (tq, tq), 0)
    s_idx = lax.broadcasted_iota(jnp.int32, (tq, tq), 1)
    state = _sweep_tile(q_heads, kd_ref[...], vd_ref[...], state, _strict_upper_ones(tq), s_idx < t_idx)

    def write(state):
        o_ref[...] = jnp.concatenate(state[0], axis=1)

    def earlier_keys(state):
        tri = _strict_upper_ones(KEY_TILE)
        state = _sweep_tile(q_heads, kp_ref[...].astype(BF16), vp_ref[...].astype(BF16), state, tri, None)
        n_tail = before // KEY_TILE - 1

        def live(carries):
            return jnp.max(functools.reduce(jnp.maximum, carries))

        def cond(c):
            n, top, _ = c
            return jnp.logical_and(n < n_tail, top > LOG_WEIGHT_FLOOR)

        def body(c):
            n, _, state = c
            rows = pl.ds(pl.multiple_of(before - (n + 2) * KEY_TILE, KEY_TILE), KEY_TILE)
            copies = (pltpu.make_async_copy(ksrc_ref.at[b, rows, :], kt_ref, sem.at[0]),
                      pltpu.make_async_copy(vsrc_ref.at[b, rows, :], vt_ref, sem.at[1]))
            for cp in copies:
                cp.start()
            for cp in copies:
                cp.wait()
            state = _sweep_tile(q_heads, kt_ref[...].astype(BF16), vt_ref[...].astype(BF16), state, tri, None)
            return n + 1, live(state[1]), state

        return lax.while_loop(cond, body, (jnp.int32(0), live(state[1]), state))[2]

    if n_before is None:
        @pl.when(i == 0)
        def _():
            write(state)

        @pl.when(i > 0)
        def _():
            write(earlier_keys(state))
    else:
        write(earlier_keys(state))


def _attention(q, k_new, v_new, k_before, v_before, *, tq, name):
    B, T, _ = q.shape
    n_q = T // tq
    if k_before is None:
        assert tq == KEY_TILE
        k_before, v_before, n_before = k_new, v_new, None
        prev_idx = lambda b, i: (b, jnp.maximum(i - 1, 0), 0)
    else:
        assert n_q == 1
        n_before = k_before.shape[1]
        prev_idx = lambda b, i: (b, n_before // KEY_TILE - 1, 0)
    tile = lambda b, i: (b, i, 0)
    new_spec = pl.BlockSpec((None, tq, D_SB), tile)
    prev_spec = pl.BlockSpec((None, KEY_TILE, D_SB), prev_idx)
    hbm_spec = pl.BlockSpec(memory_space=pl.ANY)
    kernel = functools.partial(_attn_kernel, tq=tq, n_before=n_before)
    return pl.pallas_call(
        kernel,
        grid=(B, n_q),
        in_specs=[new_spec, new_spec, new_spec, prev_spec, prev_spec, hbm_spec, hbm_spec],
        out_specs=pl.BlockSpec((None, tq, D_SB), tile),
        out_shape=jax.ShapeDtypeStruct((B, T, D_SB), F32),
        scratch_shapes=[
            pltpu.VMEM((KEY_TILE, D_SB), k_before.dtype),
            pltpu.VMEM((KEY_TILE, D_SB), v_before.dtype),
            pltpu.SemaphoreType.DMA((2,)),
        ],
        compiler_params=pltpu.CompilerParams(
            dimension_semantics=("arbitrary", "arbitrary"), vmem_limit_bytes=VMEM_LIMIT_BYTES),
        name=name,
    )(q, k_new, v_new, k_before, v_before, k_before, v_before)


def _back_kernel(x_ref, olru_ref, osb_ref, mod_ref, g_sb_ref, g_post_mix_ref, g_pre_mlp_ref,
                 g_post_mlp_ref, w_out_ref, w_up_ref, w_down_ref, y_ref, *, n_seq, n_rows, ff_chunk):
    S, L = n_seq, n_rows
    gate_a = mod_ref[:, 2:3, :]
    shift_f = mod_ref[:, 3:4, :]
    scale_f = mod_ref[:, 4:5, :]
    gate_f = mod_ref[:, 5:6, :]

    o_lru = olru_ref[...].reshape(S * L, D_LRU)
    o_sb = _rms(osb_ref[...], g_sb_ref[...]).reshape(S * L, D_SB).astype(BF16)
    o = _dot(o_lru, w_out_ref[0:D_LRU, :]) + _dot(o_sb, w_out_ref[D_LRU:, :])
    x = x_ref[...] + gate_a * _rms(o, g_post_mix_ref[...]).reshape(S, L, D_MODEL)

    h = _rms(x, g_pre_mlp_ref[...]) * (1.0 + scale_f) + shift_f
    hb = h.reshape(S * L, D_MODEL).astype(BF16)
    f = jnp.zeros((S * L, D_MODEL), F32)
    for c in range(D_FF // ff_chunk):
        cols = slice(c * ff_chunk, (c + 1) * ff_chunk)
        up = jnp.maximum(_dot(hb, w_up_ref[:, cols]), 0.0)
        f = f + _dot((up * up).astype(BF16), w_down_ref[cols, :])
    y_ref[...] = x + gate_f * _rms(f, g_post_mlp_ref[...]).reshape(S, L, D_MODEL)


def _back(x, o_lru, o_sb, mod, g_sb, g_post_mix, g_pre_mlp, g_post_mlp, w_out_bf, w_up_bf, w_down_bf,
          *, n_seq, n_rows, name, ff_chunk=1024):
    B, T, _ = x.shape
    S, L = n_seq, n_rows
    const2 = lambda b, t: (0, 0)
    tile3 = lambda b, t: (b, t, 0)
    seq3 = lambda b, t: (b, 0, 0)
    kernel = functools.partial(_back_kernel, n_seq=S, n_rows=L, ff_chunk=ff_chunk)
    return pl.pallas_call(
        kernel,
        grid=(B // S, T // L),
        in_specs=[
            pl.BlockSpec((S, L, D_MODEL), tile3),
            pl.BlockSpec((S, L, D_LRU), tile3),
            pl.BlockSpec((S, L, D_SB), tile3),
            pl.BlockSpec((S, N_MOD, D_MODEL), seq3),
            pl.BlockSpec((1, D_SB), const2),
            pl.BlockSpec((1, D_MODEL), const2),
            pl.BlockSpec((1, D_MODEL), const2),
            pl.BlockSpec((1, D_MODEL), const2),
            pl.BlockSpec((D_MODEL, D_MODEL), const2),
            pl.BlockSpec((D_MODEL, D_FF), const2),
            pl.BlockSpec((D_FF, D_MODEL), const2),
        ],
        out_specs=pl.BlockSpec((S, L, D_MODEL), tile3),
        out_shape=jax.ShapeDtypeStruct((B, T, D_MODEL), F32),
        compiler_params=pltpu.CompilerParams(
            dimension_semantics=("arbitrary", "arbitrary"), vmem_limit_bytes=VMEM_LIMIT_BYTES),
        name=name,
    )(x, o_lru, o_sb, mod, g_sb, g_post_mix, g_pre_mlp, g_post_mlp, w_out_bf, w_up_bf, w_down_bf)


def _block_diag(w):
    n, bi, bj = w.shape
    eye = jnp.eye(n, dtype=w.dtype)
    return (eye[:, None, :, None] * w[:, :, None, :]).reshape(n * bi, n * bj)


def _pad_conv_state(buf):
    return jnp.pad(buf, ((0, 0), (CONV_CARRY_ROWS - (CONV_WIDTH - 1), 0), (0, 0)))


def kernel(x_prompt, x_sample, c_prompt, c_sample, cache_conv, state_lru, cache_k, cache_v, w_ada, b_ada, g_pre_mix, g_post_mix, g_pre_mlp, g_post_mlp, w_in, conv_w, conv_b, w_rg_a, b_rg_a, w_rg_x, b_rg_x, lru_lambda, g_lru_out, g_sb_out, w_out, w_up, w_down):
    depth = w_ada.shape[0]
    bp, seq, _ = x_prompt.shape
    bs, dec_seq, _ = x_sample.shape
    assert depth == 1 and bp == 1, "single-layer, single-prompt configuration"
    past_len = cache_k.shape[2]
    l = 0

    row = lambda a: a.reshape(1, -1)
    w_in_bf = w_in[l].astype(BF16)
    w_gate_bf = jnp.concatenate([_block_diag(w_rg_a[l]), _block_diag(w_rg_x[l])], axis=1).astype(BF16)
    b_gate = jnp.concatenate([b_rg_a[l].reshape(-1), b_rg_x[l].reshape(-1)]).reshape(1, -1)
    w_out_bf = w_out[l].astype(BF16)
    w_up_bf = w_up[l].astype(BF16)
    w_down_bf = w_down[l].astype(BF16)
    front_w = (row(g_pre_mix[l]), w_in_bf, conv_w[l], row(conv_b[l]), w_gate_bf, b_gate,
               row(lru_lambda[l]), row(g_lru_out[l]))
    back_w = (row(g_sb_out[l]), row(g_post_mix[l]), row(g_pre_mlp[l]), row(g_post_mlp[l]),
              w_out_bf, w_up_bf, w_down_bf)

    mod = _adaln(jnp.concatenate([c_prompt, c_sample], axis=0), w_ada[l], b_ada[l])
    mod = mod.reshape(bp + bs, N_MOD, D_MODEL)
    mod_p, mod_s = mod[:bp], mod[bp:]

    zeros_conv = jnp.zeros((bp, CONV_CARRY_ROWS, D_LRU), F32)
    zeros_h = jnp.zeros((bp, 1, D_LRU), F32)
    olru_p, q_p, kbf_p, vbf_p, k_p, v_p, conv_p, h_p = _front(
        x_prompt, mod_p, zeros_conv, zeros_h, *front_w, n_rows=512, name="front_prompt")
    osb_p = _attention(q_p, kbf_p, vbf_p, None, None, tq=KEY_TILE, name="attn_prompt")
    y_p = _back(x_prompt, olru_p, osb_p, mod_p, *back_w, n_seq=1, n_rows=512, name="back_prompt")

    olru_s, q_s, kbf_s, vbf_s, k_s, v_s, conv_s, h_s = _front(
        x_sample, mod_s, _pad_conv_state(cache_conv[l]), state_lru[l].reshape(bs, 1, D_LRU),
        *front_w, n_rows=dec_seq, name="front_sample")
    osb_s = _attention(q_s, kbf_s, vbf_s, cache_k[l].reshape(bs, past_len, D_SB),
                       cache_v[l].reshape(bs, past_len, D_SB), tq=dec_seq, name="attn_sample")
    y_s = _back(x_sample, olru_s, osb_s, mod_s, *back_w, n_seq=8, n_rows=dec_seq, name="back_sample")

    heads = lambda a: a.reshape(1, a.shape[0], a.shape[1], N_SB_HEADS, SB_HEAD_DIM)
    return (y_p, y_s, conv_p[None], h_p.reshape(1, bp, D_LRU), heads(k_p), heads(v_p),
            conv_s[None], h_s.reshape(1, bs, D_LRU), heads(k_s), heads(v_s))
```

```python
import functools

import jax
import jax.numpy as jnp
from jax import lax
from jax.experimental import pallas as pl
from jax.experimental.pallas import tpu as pltpu

D_MODEL = 1024
D_LRU = 512
D_SB = 512
N_SB_HEADS = 8
SB_HEAD_DIM = 64
HEAD_PAIR = 2 * SB_HEAD_DIM
N_PAIRS = N_SB_HEADS // 2
CONV_WIDTH = 4
CONV_CARRY_ROWS = 8
RG_C = 8.0
D_FF = 4 * D_MODEL
D_IN = 2 * D_LRU + 3 * D_SB
N_MOD = 6
EPS = 1e-6
LOG2_E = 1.4426950408889634

KEY_TILE = 256
LOG_WEIGHT_FLOOR = -110.0

VMEM_LIMIT_BYTES = 56 * 1024 * 1024

BF16 = jnp.bfloat16
F32 = jnp.float32


def _rms(x, g):
    return x * lax.rsqrt(jnp.mean(x * x, axis=-1, keepdims=True) + EPS) * g


def _dot(a, b):
    return jnp.dot(a, b, preferred_element_type=F32)


def _adaln_kernel(c_ref, w_ref, b_ref, o_ref):
    c = c_ref[...]
    s = (c * jax.nn.sigmoid(c)).astype(BF16)
    o_ref[...] = _dot(s, w_ref[...].astype(BF16)) + b_ref[...]


def _adaln(c_all, w_ada, b_ada, *, tn=1536):
    n_rows = c_all.shape[0]
    n_out = w_ada.shape[1]
    return pl.pallas_call(
        _adaln_kernel,
        grid=(n_out // tn,),
        in_specs=[
            pl.BlockSpec((n_rows, D_MODEL), lambda j: (0, 0)),
            pl.BlockSpec((D_MODEL, tn), lambda j: (0, j)),
            pl.BlockSpec((1, tn), lambda j: (0, j)),
        ],
        out_specs=pl.BlockSpec((n_rows, tn), lambda j: (0, j)),
        out_shape=jax.ShapeDtypeStruct((n_rows, n_out), F32),
        compiler_params=pltpu.CompilerParams(
            dimension_semantics=("arbitrary",), vmem_limit_bytes=VMEM_LIMIT_BYTES),
        name="adaln",
    )(c_all, w_ada, b_ada.reshape(1, n_out))


def _shift_rows(x, s, fill, row):
    return jnp.where(row >= s, pltpu.roll(x, s, axis=1), fill)


def _front_kernel(x_ref, mod_ref, conv0_ref, h0_ref, g_pre_ref, w_in_ref, conv_w_ref, conv_b_ref,
                  w_gate_ref, b_gate_ref, lam_ref, g_lru_ref,
                  olru_ref, q_ref, kbf_ref, vbf_ref, k_ref, v_ref, conv_out_ref, h_out_ref,
                  prev_ref, hc_ref, *, n_seq, n_rows):
    S, L = n_seq, n_rows

    @pl.when(pl.program_id(0) == 0)
    def _():
        prev_ref[...] = conv0_ref[...]
        hc_ref[...] = h0_ref[...]

    x = x_ref[...]
    shift = mod_ref[:, 0:1, :]
    scale = mod_ref[:, 1:2, :]
    h = _rms(x, g_pre_ref[...]) * (1.0 + scale) + shift
    proj = _dot(h.reshape(S * L, D_MODEL).astype(BF16), w_in_ref[...])
    xl = proj[:, 0:D_LRU].reshape(S, L, D_LRU)
    gl = proj[:, D_LRU:2 * D_LRU].reshape(S, L, D_LRU)
    q = proj[:, 2 * D_LRU:2 * D_LRU + D_SB]
    k = proj[:, 2 * D_LRU + D_SB:2 * D_LRU + 2 * D_SB]
    v = proj[:, 2 * D_LRU + 2 * D_SB:]

    k_ref[...] = k.reshape(S, L, D_SB)
    v_ref[...] = v.reshape(S, L, D_SB)
    q_ref[...] = (q * (SB_HEAD_DIM ** -0.5)).astype(BF16).reshape(S, L, D_SB)
    kbf_ref[...] = k.astype(BF16).reshape(S, L, D_SB)
    vbf_ref[...] = v.astype(BF16).reshape(S, L, D_SB)

    ext = jnp.concatenate([prev_ref[...], xl], axis=1)
    cw = conv_w_ref[...]
    xc = conv_b_ref[...]
    for j in range(CONV_WIDTH - 1):
        back = CONV_WIDTH - 1 - j
        xc = xc + pltpu.roll(ext, back, axis=1)[:, CONV_CARRY_ROWS:, :] * cw[j:j + 1, :]
    xc = xc + xl * cw[CONV_WIDTH - 1:CONV_WIDTH, :]
    prev_ref[...] = xl[:, L - CONV_CARRY_ROWS:, :]
    conv_out_ref[...] = ext[:, L + CONV_CARRY_ROWS - (CONV_WIDTH - 1):, :]

    gates = _dot(xc.reshape(S * L, D_LRU).astype(BF16), w_gate_ref[...]) + b_gate_ref[...]
    r = jax.nn.sigmoid(gates[:, 0:D_LRU]).reshape(S, L, D_LRU)
    i = jax.nn.sigmoid(gates[:, D_LRU:]).reshape(S, L, D_LRU)
    neg_lam = -lam_ref[...]
    softplus_neg_lam = jnp.maximum(neg_lam, 0.0) + jnp.log1p(jnp.exp(-jnp.abs(neg_lam)))
    log_a = (-RG_C * r) * softplus_neg_lam
    a = jnp.exp(log_a)
    u = jnp.sqrt(jnp.tanh(-log_a) * (1.0 + a * a)) * (i * xc)
    row = lax.broadcasted_iota(jnp.int32, (S, L, D_LRU), 1)
    u = u + jnp.where(row == 0, a * hc_ref[...], 0.0)

    b = u
    step = 1
    while step < L:
        b = a * _shift_rows(b, step, 0.0, row) + b
        if 2 * step < L:
            a = a * _shift_rows(a, step, 1.0, row)
        step *= 2
    h_last = b[:, L - 1:L, :]
    hc_ref[...] = h_last
    h_out_ref[...] = h_last

    o = b * jax.nn.gelu(gl)
    olru_ref[...] = _rms(o, g_lru_ref[...]).astype(BF16)


def _front(x, mod, conv0, h0, g_pre, w_in_bf, conv_w, conv_b, w_gate_bf, b_gate, lam, g_lru,
           *, n_rows, name):
    S, T, _ = x.shape
    L = n_rows
    n_t = T // L
    const2 = lambda t: (0, 0)
    const3 = lambda t: (0, 0, 0)
    tile3 = lambda t: (0, t, 0)
    kernel = functools.partial(_front_kernel, n_seq=S, n_rows=L)
    return pl.pallas_call(
        kernel,
        grid=(n_t,),
        in_specs=[
            pl.BlockSpec((S, L, D_MODEL), tile3),
            pl.BlockSpec((S, N_MOD, D_MODEL), const3),
            pl.BlockSpec((S, CONV_CARRY_ROWS, D_LRU), const3),
            pl.BlockSpec((S, 1, D_LRU), const3),
            pl.BlockSpec((1, D_MODEL), const2),
            pl.BlockSpec((D_MODEL, D_IN), const2),
            pl.BlockSpec((CONV_WIDTH, D_LRU), const2),
            pl.BlockSpec((1, D_LRU), const2),
            pl.BlockSpec((D_LRU, 2 * D_LRU), const2),
            pl.BlockSpec((1, 2 * D_LRU), const2),
            pl.BlockSpec((1, D_LRU), const2),
            pl.BlockSpec((1, D_LRU), const2),
        ],
        out_specs=[
            pl.BlockSpec((S, L, D_LRU), tile3),
            pl.BlockSpec((S, L, D_SB), tile3),
            pl.BlockSpec((S, L, D_SB), tile3),
            pl.BlockSpec((S, L, D_SB), tile3),
            pl.BlockSpec((S, L, D_SB), tile3),
            pl.BlockSpec((S, L, D_SB), tile3),
            pl.BlockSpec((S, CONV_WIDTH - 1, D_LRU), const3),
            pl.BlockSpec((S, 1, D_LRU), const3),
        ],
        out_shape=[
            jax.ShapeDtypeStruct((S, T, D_LRU), BF16),
            jax.ShapeDtypeStruct((S, T, D_SB), BF16),
            jax.ShapeDtypeStruct((S, T, D_SB), BF16),
            jax.ShapeDtypeStruct((S, T, D_SB), BF16),
            jax.ShapeDtypeStruct((S, T, D_SB), F32),
            jax.ShapeDtypeStruct((S, T, D_SB), F32),
            jax.ShapeDtypeStruct((S, CONV_WIDTH - 1, D_LRU), F32),
            jax.ShapeDtypeStruct((S, 1, D_LRU), F32),
        ],
        scratch_shapes=[
            pltpu.VMEM((S, CONV_CARRY_ROWS, D_LRU), F32),
            pltpu.VMEM((S, 1, D_LRU), F32),
        ],
        compiler_params=pltpu.CompilerParams(
            dimension_semantics=("arbitrary",), vmem_limit_bytes=VMEM_LIMIT_BYTES),
        name=name,
    )(x, mod, conv0, h0, g_pre, w_in_bf, conv_w, conv_b, w_gate_bf, b_gate, lam, g_lru)


def _prefix_ones(n):
    j = lax.broadcasted_iota(jnp.int32, (n, n), 0)
    s = lax.broadcasted_iota(jnp.int32, (n, n), 1)
    upper = jnp.where(j >= s, -1.0, 0.0).astype(BF16)
    return jnp.concatenate([upper, upper], axis=0)


def _stack_heads(x):
    lane = lax.broadcasted_iota(jnp.int32, x.shape, 1)
    zero = jnp.zeros_like(x)
    return jnp.concatenate([jnp.where(lane < SB_HEAD_DIM, x, zero),
                            jnp.where(lane >= SB_HEAD_DIM, x, zero)], axis=0)


def _pair_cols(p):
    return slice(p * HEAD_PAIR, (p + 1) * HEAD_PAIR)


def _sweep_tiles(q_pairs, tiles, state):
    accs, carries = state
    tq = accs[0].shape[0]
    pairs = range(N_PAIRS)
    lane = lax.broadcasted_iota(jnp.int32, (tq, HEAD_PAIR), 1)

    def scores(kt):
        return [lax.dot_general(q_pairs[p], kt[:, _pair_cols(p)], (((1,), (1,)), ((), ())),
                                preferred_element_type=F32) for p in pairs]

    def drop_terms(z, mask):
        drop, split = [], []
        for p in pairs:
            sp = jnp.maximum(z[p], 0.0) + jnp.log(1.0 + jnp.exp2(jnp.abs(z[p]) * -LOG2_E))
            if mask is not None:
                sp = jnp.where(mask, sp, 0.0)
            hi = sp.astype(BF16)
            lo = (sp - hi.astype(F32)).astype(BF16)
            drop.append(sp)
            split.append(jnp.concatenate([hi, lo], axis=1))
        return drop, jnp.concatenate(split, axis=0)

    def weigh(z, log_keep_from, carries, mask, vt, accs):
        new_accs = []
        for p in pairs:
            w = jnp.exp(z[p] + log_keep_from[2 * tq * p:2 * tq * (p + 1)] + carries[p])
            if mask is not None:
                w = jnp.where(mask, w, 0.0)
            pv = _dot(w.astype(BF16), vt[:, _pair_cols(p)])
            new_accs.append(accs[p] + jnp.where(lane < SB_HEAD_DIM, pv[:tq], pv[tq:]))
        return new_accs

    z = [scores(kt) for kt, _, _, _ in tiles]
    log_keep_from, tile_carries = [], []
    for n, (_, _, neg_tri2, mask) in enumerate(tiles):
        drop, split = drop_terms(z[n], mask)
        log_keep_from.append(_dot(split, neg_tri2))
        tile_carries.append(carries)
        carries = tuple(carries[p] - jnp.sum(drop[p], axis=-1, keepdims=True) for p in pairs)
    for n, (_, vt, _, mask) in enumerate(tiles):
        accs = weigh(z[n], log_keep_from[n], tile_carries[n], mask, vt, accs)
    return tuple(accs), carries


def _attn_kernel(q_ref, kd_ref, vd_ref, kp_ref, vp_ref, ksrc_ref, vsrc_ref, o_ref,
                 kt_ref, vt_ref, sem, *, tq, n_before):
    b = pl.program_id(0)
    i = pl.program_id(1)
    before = i * tq if n_before is None else n_before

    q = q_ref[...]
    q_pairs = [_stack_heads(q[:, p * HEAD_PAIR:(p + 1) * HEAD_PAIR]) for p in range(N_PAIRS)]

    state = (tuple(jnp.zeros((tq, HEAD_PAIR), F32) for _ in range(N_PAIRS)),
             tuple(jnp.zeros((2 * tq, 1), F32) for _ in range(N_PAIRS)))
    t_idx = lax.broadcasted_iota(jnp.int32, (tq, tq), 0)
    s_idx = lax.broadcasted_iota(jnp.int32, (tq, tq), 1)
    causal = s_idx < t_idx
    own_tile = (kd_ref[...], vd_ref[...], _prefix_ones(tq), jnp.concatenate([causal, causal], axis=0))

    def write(state):
        o_ref[...] = jnp.concatenate(state[0], axis=1)

    def own_and_earlier_keys(state):
        tri = _prefix_ones(KEY_TILE)
        prev_tile = (kp_ref[...].astype(BF16), vp_ref[...].astype(BF16), tri, None)
        state = _sweep_tiles(q_pairs, [own_tile, prev_tile], state)
        n_tail = before // KEY_TILE - 1

        def live(carries):
            return jnp.max(functools.reduce(jnp.maximum, carries))

        def cond(c):
            n, top, _ = c
            return jnp.logical_and(n < n_tail, top > LOG_WEIGHT_FLOOR)

        def body(c):
            n, _, state = c
            rows = pl.ds(pl.multiple_of(before - (n + 2) * KEY_TILE, KEY_TILE), KEY_TILE)
            copies = (pltpu.make_async_copy(ksrc_ref.at[b, rows, :], kt_ref, sem.at[0]),
                      pltpu.make_async_copy(vsrc_ref.at[b, rows, :], vt_ref, sem.at[1]))
            for cp in copies:
                cp.start()
            for cp in copies:
                cp.wait()
            tile = (kt_ref[...].astype(BF16), vt_ref[...].astype(BF16), tri, None)
            state = _sweep_tiles(q_pairs, [tile], state)
            return n + 1, live(state[1]), state

        return lax.while_loop(cond, body, (jnp.int32(0), live(state[1]), state))[2]

    if n_before is None:
        @pl.when(i == 0)
        def _():
            write(_sweep_tiles(q_pairs, [own_tile], state))

        @pl.when(i > 0)
        def _():
            write(own_and_earlier_keys(state))
    else:
        write(own_and_earlier_keys(state))


def _attention(q, k_new, v_new, k_before, v_before, *, tq, name):
    B, T, _ = q.shape
    n_q = T // tq
    if k_before is None:
        assert tq == KEY_TILE
        k_before, v_before, n_before = k_new, v_new, None
        prev_idx = lambda b, i: (b, jnp.maximum(i - 1, 0), 0)
    else:
        assert n_q == 1
        n_before = k_before.shape[1]
        prev_idx = lambda b, i: (b, n_before // KEY_TILE - 1, 0)
    tile = lambda b, i: (b, i, 0)
    new_spec = pl.BlockSpec((None, tq, D_SB), tile)
    prev_spec = pl.BlockSpec((None, KEY_TILE, D_SB), prev_idx)
    hbm_spec = pl.BlockSpec(memory_space=pl.ANY)
    kernel = functools.partial(_attn_kernel, tq=tq, n_before=n_before)
    return pl.pallas_call(
        kernel,
        grid=(B, n_q),
        in_specs=[new_spec, new_spec, new_spec, prev_spec, prev_spec, hbm_spec, hbm_spec],
        out_specs=pl.BlockSpec((None, tq, D_SB), tile),
        out_shape=jax.ShapeDtypeStruct((B, T, D_SB), F32),
        scratch_shapes=[
            pltpu.VMEM((KEY_TILE, D_SB), k_before.dtype),
            pltpu.VMEM((KEY_TILE, D_SB), v_before.dtype),
            pltpu.SemaphoreType.DMA((2,)),
        ],
        compiler_params=pltpu.CompilerParams(
            dimension_semantics=("arbitrary", "arbitrary"), vmem_limit_bytes=VMEM_LIMIT_BYTES),
        name=name,
    )(q, k_new, v_new, k_before, v_before, k_before, v_before)


def _back_kernel(x_ref, olru_ref, osb_ref, mod_ref, g_sb_ref, g_post_mix_ref, g_pre_mlp_ref,
                 g_post_mlp_ref, w_out_ref, w_up_ref, w_down_ref, y_ref, *, n_seq, n_rows, ff_chunk):
    S, L = n_seq, n_rows
    gate_a = mod_ref[:, 2:3, :]
    shift_f = mod_ref[:, 3:4, :]
    scale_f = mod_ref[:, 4:5, :]
    gate_f = mod_ref[:, 5:6, :]

    o_lru = olru_ref[...].reshape(S * L, D_LRU)
    o_sb = _rms(osb_ref[...], g_sb_ref[...]).reshape(S * L, D_SB).astype(BF16)
    o = _dot(o_lru, w_out_ref[0:D_LRU, :]) + _dot(o_sb, w_out_ref[D_LRU:, :])
    x = x_ref[...] + gate_a * _rms(o, g_post_mix_ref[...]).reshape(S, L, D_MODEL)

    h = _rms(x, g_pre_mlp_ref[...]) * (1.0 + scale_f) + shift_f
    hb = h.reshape(S * L, D_MODEL).astype(BF16)
    f = jnp.zeros((S * L, D_MODEL), F32)
    for c in range(D_FF // ff_chunk):
        cols = slice(c * ff_chunk, (c + 1) * ff_chunk)
        up = jnp.maximum(_dot(hb, w_up_ref[:, cols]), 0.0)
        f = f + _dot((up * up).astype(BF16), w_down_ref[cols, :])
    y_ref[...] = x + gate_f * _rms(f, g_post_mlp_ref[...]).reshape(S, L, D_MODEL)


def _back(x, o_lru, o_sb, mod, g_sb, g_post_mix, g_pre_mlp, g_post_mlp, w_out_bf, w_up_bf, w_down_bf,
          *, n_seq, n_rows, name, ff_chunk=1024):
    B, T, _ = x.shape
    S, L = n_seq, n_rows
    const2 = lambda b, t: (0, 0)
    tile3 = lambda b, t: (b, t, 0)
    seq3 = lambda b, t: (b, 0, 0)
    kernel = functools.partial(_back_kernel, n_seq=S, n_rows=L, ff_chunk=ff_chunk)
    return pl.pallas_call(
        kernel,
        grid=(B // S, T // L),
        in_specs=[
            pl.BlockSpec((S, L, D_MODEL), tile3),
            pl.BlockSpec((S, L, D_LRU), tile3),
            pl.BlockSpec((S, L, D_SB), tile3),
            pl.BlockSpec((S, N_MOD, D_MODEL), seq3),
            pl.BlockSpec((1, D_SB), const2),
            pl.BlockSpec((1, D_MODEL), const2),
            pl.BlockSpec((1, D_MODEL), const2),
            pl.BlockSpec((1, D_MODEL), const2),
            pl.BlockSpec((D_MODEL, D_MODEL), const2),
            pl.BlockSpec((D_MODEL, D_FF), const2),
            pl.BlockSpec((D_FF, D_MODEL), const2),
        ],
        out_specs=pl.BlockSpec((S, L, D_MODEL), tile3),
        out_shape=jax.ShapeDtypeStruct((B, T, D_MODEL), F32),
        compiler_params=pltpu.CompilerParams(
            dimension_semantics=("arbitrary", "arbitrary"), vmem_limit_bytes=VMEM_LIMIT_BYTES),
        name=name,
    )(x, o_lru, o_sb, mod, g_sb, g_post_mix, g_pre_mlp, g_post_mlp, w_out_bf, w_up_bf, w_down_bf)


def _block_diag(w):
    n, bi, bj = w.shape
    eye = jnp.eye(n, dtype=w.dtype)
    return (eye[:, None, :, None] * w[:, :, None, :]).reshape(n * bi, n * bj)


def _pad_conv_state(buf):
    return jnp.pad(buf, ((0, 0), (CONV_CARRY_ROWS - (CONV_WIDTH - 1), 0), (0, 0)))


def kernel(x_prompt, x_sample, c_prompt, c_sample, cache_conv, state_lru, cache_k, cache_v, w_ada, b_ada, g_pre_mix, g_post_mix, g_pre_mlp, g_post_mlp, w_in, conv_w, conv_b, w_rg_a, b_rg_a, w_rg_x, b_rg_x, lru_lambda, g_lru_out, g_sb_out, w_out, w_up, w_down):
    depth = w_ada.shape[0]
    bp, seq, _ = x_prompt.shape
    bs, dec_seq, _ = x_sample.shape
    assert depth == 1 and bp == 1, "single-layer, single-prompt configuration"
    past_len = cache_k.shape[2]
    l = 0

    row = lambda a: a.reshape(1, -1)
    w_in_bf = w_in[l].astype(BF16)
    w_gate_bf = jnp.concatenate([_block_diag(w_rg_a[l]), _block_diag(w_rg_x[l])], axis=1).astype(BF16)
    b_gate = jnp.concatenate([b_rg_a[l].reshape(-1), b_rg_x[l].reshape(-1)]).reshape(1, -1)
    w_out_bf = w_out[l].astype(BF16)
    w_up_bf = w_up[l].astype(BF16)
    w_down_bf = w_down[l].astype(BF16)
    front_w = (row(g_pre_mix[l]), w_in_bf, conv_w[l], row(conv_b[l]), w_gate_bf, b_gate,
               row(lru_lambda[l]), row(g_lru_out[l]))
    back_w = (row(g_sb_out[l]), row(g_post_mix[l]), row(g_pre_mlp[l]), row(g_post_mlp[l]),
              w_out_bf, w_up_bf, w_down_bf)

    mod = _adaln(jnp.concatenate([c_prompt, c_sample], axis=0), w_ada[l], b_ada[l])
    mod = mod.reshape(bp + bs, N_MOD, D_MODEL)
    mod_p, mod_s = mod[:bp], mod[bp:]

    zeros_conv = jnp.zeros((bp, CONV_CARRY_ROWS, D_LRU), F32)
    zeros_h = jnp.zeros((bp, 1, D_LRU), F32)
    olru_p, q_p, kbf_p, vbf_p, k_p, v_p, conv_p, h_p = _front(
        x_prompt, mod_p, zeros_conv, zeros_h, *front_w, n_rows=512, name="front_prompt")
    osb_p = _attention(q_p, kbf_p, vbf_p, None, None, tq=KEY_TILE, name="attn_prompt")
    y_p = _back(x_prompt, olru_p, osb_p, mod_p, *back_w, n_seq=1, n_rows=512, name="back_prompt")

    olru_s, q_s, kbf_s, vbf_s, k_s, v_s, conv_s, h_s = _front(
        x_sample, mod_s, _pad_conv_state(cache_conv[l]), state_lru[l].reshape(bs, 1, D_LRU),
        *front_w, n_rows=dec_seq, name="front_sample")
    osb_s = _attention(q_s, kbf_s, vbf_s, cache_k[l].reshape(bs, past_len, D_SB),
                       cache_v[l].reshape(bs, past_len, D_SB), tq=dec_seq, name="attn_sample")
    y_s = _back(x_sample, olru_s, osb_s, mod_s, *back_w, n_seq=8, n_rows=dec_seq, name="back_sample")

    heads = lambda a: a.reshape(1, a.shape[0], a.shape[1], N_SB_HEADS, SB_HEAD_DIM)
    return (y_p, y_s, conv_p[None], h_p.reshape(1, bp, D_LRU), heads(k_p), heads(v_p),
            conv_s[None], h_s.reshape(1, bs, D_LRU), heads(k_s), heads(v_s))
```

```python
import functools

import jax
import jax.numpy as jnp
from jax import lax
from jax.experimental import pallas as pl
from jax.experimental.pallas import tpu as pltpu

D_MODEL = 1024
D_LRU = 512
D_SB = 512
N_SB_HEADS = 8
SB_HEAD_DIM = 64
HEAD_PAIR = 2 * SB_HEAD_DIM
N_PAIRS = N_SB_HEADS // 2
CONV_WIDTH = 4
CONV_CARRY_ROWS = 8
RG_C = 8.0
D_FF = 4 * D_MODEL
D_IN = 2 * D_LRU + 3 * D_SB
N_MOD = 6
EPS = 1e-6
LOG2_E = 1.4426950408889634
F32_TINY = 1.1754943508222875e-38

KEY_TILE = 256
LOG_WEIGHT_FLOOR = -110.0

VMEM_LIMIT_BYTES = 56 * 1024 * 1024

BF16 = jnp.bfloat16
F32 = jnp.float32


def _rms(x, g):
    return x * lax.rsqrt(jnp.mean(x * x, axis=-1, keepdims=True) + EPS) * g


def _dot(a, b):
    return jnp.dot(a, b, preferred_element_type=F32)


def _adaln_kernel(c_ref, w_ref, b_ref, o_ref):
    c = c_ref[...]
    s = (c * jax.nn.sigmoid(c)).astype(BF16)
    o_ref[...] = _dot(s, w_ref[...].astype(BF16)) + b_ref[...]


def _adaln(c_all, w_ada, b_ada, *, tn=1536):
    n_rows = c_all.shape[0]
    n_out = w_ada.shape[1]
    return pl.pallas_call(
        _adaln_kernel,
        grid=(n_out // tn,),
        in_specs=[
            pl.BlockSpec((n_rows, D_MODEL), lambda j: (0, 0)),
            pl.BlockSpec((D_MODEL, tn), lambda j: (0, j)),
            pl.BlockSpec((1, tn), lambda j: (0, j)),
        ],
        out_specs=pl.BlockSpec((n_rows, tn), lambda j: (0, j)),
        out_shape=jax.ShapeDtypeStruct((n_rows, n_out), F32),
        compiler_params=pltpu.CompilerParams(
            dimension_semantics=("arbitrary",), vmem_limit_bytes=VMEM_LIMIT_BYTES),
        name="adaln",
    )(c_all, w_ada, b_ada.reshape(1, n_out))


def _shift_rows(x, s, fill, row):
    return jnp.where(row >= s, pltpu.roll(x, s, axis=1), fill)


SUBLANES = 8
LANES = 128


def _lru_scan(a, u, h_in, a_scr, b_scr):
    S, L, C = a.shape
    G = L // SUBLANES
    ag = a.reshape(S * G, SUBLANES, C)
    bg = u.reshape(S * G, SUBLANES, C)
    sub = lax.broadcasted_iota(jnp.int32, ag.shape, 1)
    step = 1
    while step < SUBLANES:
        bg = ag * _shift_rows(bg, step, 0.0, sub) + bg
        ag = ag * _shift_rows(ag, step, 1.0, sub)
        step *= 2
    last_rows = pl.ds(SUBLANES - 1, S * G, stride=SUBLANES)

    def group_ends(x, scr):
        x = x.reshape(S * L, C)
        for c in range(C // LANES):
            scr[c] = x[:, c * LANES:(c + 1) * LANES]
        ends = [scr[c, last_rows, :] for c in range(C // LANES)]
        return jnp.concatenate(ends, axis=1).reshape(S, G, C)

    a_sum = group_ends(ag, a_scr)
    b_sum = group_ends(bg, b_scr)
    grp = lax.broadcasted_iota(jnp.int32, a_sum.shape, 1)
    step = 1
    while step < G:
        b_sum = a_sum * _shift_rows(b_sum, step, 0.0, grp) + b_sum
        a_sum = a_sum * _shift_rows(a_sum, step, 1.0, grp)
        step *= 2
    h_end = b_sum + a_sum * h_in
    h_enter = _shift_rows(h_end, 1, h_in, grp)
    h = bg + ag * h_enter.reshape(S * G, 1, C)
    return h.reshape(S, L, C), h_end[:, G - 1:G, :]


def _front_kernel(x_ref, mod_ref, conv0_ref, h0_ref, g_pre_ref, w_in_ref, conv_w_ref, conv_b_ref,
                  w_gate_ref, b_gate_ref, lam_ref, g_lru_ref,
                  olru_ref, q_ref, kbf_ref, vbf_ref, k_ref, v_ref, conv_out_ref, h_out_ref,
                  ext_ref, hc_ref, a_scr, b_scr, *, n_seq, n_rows):
    S, L = n_seq, n_rows

    @pl.when(pl.program_id(0) == 0)
    def _():
        ext_ref[:, 0:CONV_CARRY_ROWS, :] = conv0_ref[...]
        hc_ref[...] = h0_ref[...]

    x = x_ref[...]
    shift = mod_ref[:, 0:1, :]
    scale = mod_ref[:, 1:2, :]
    h = _rms(x, g_pre_ref[...] * (1.0 + scale)) + shift
    hb = h.reshape(S * L, D_MODEL).astype(BF16)

    def project(first_col, n_cols):
        return _dot(hb, w_in_ref[:, first_col:first_col + n_cols])

    lru_in = project(0, 2 * D_LRU)
    xl = lru_in[:, 0:D_LRU].reshape(S, L, D_LRU)
    gl = lru_in[:, D_LRU:].reshape(S, L, D_LRU)

    k = project(2 * D_LRU + D_SB, D_SB)
    k_ref[...] = k.reshape(S, L, D_SB)
    kbf_ref[...] = k.astype(BF16).reshape(S, L, D_SB)

    ext_ref[:, CONV_CARRY_ROWS:, :] = xl
    cw = conv_w_ref[...]
    xc = conv_b_ref[...]
    for j in range(CONV_WIDTH - 1):
        start = CONV_CARRY_ROWS - (CONV_WIDTH - 1 - j)
        xc = xc + ext_ref[:, start:start + L, :] * cw[j:j + 1, :]
    xc = xc + xl * cw[CONV_WIDTH - 1:CONV_WIDTH, :]
    conv_out_ref[...] = ext_ref[:, L + CONV_CARRY_ROWS - (CONV_WIDTH - 1):, :]
    ext_ref[:, 0:CONV_CARRY_ROWS, :] = xl[:, L - CONV_CARRY_ROWS:, :]

    gates = _dot(xc.reshape(S * L, D_LRU).astype(BF16), w_gate_ref[...]) + b_gate_ref[...]
    v = project(2 * D_LRU + 2 * D_SB, D_SB)
    v_ref[...] = v.reshape(S, L, D_SB)
    vbf_ref[...] = v.astype(BF16).reshape(S, L, D_SB)

    r = jax.nn.sigmoid(gates[:, 0:D_LRU]).reshape(S, L, D_LRU)
    i = jax.nn.sigmoid(gates[:, D_LRU:]).reshape(S, L, D_LRU)
    neg_lam = -lam_ref[...]
    decay = RG_C * (jnp.maximum(neg_lam, 0.0) + jnp.log1p(jnp.exp(-jnp.abs(neg_lam))))
    neg_log_a = r * decay
    a = jnp.exp2(r * (decay * -LOG2_E))
    m = jnp.tanh(neg_log_a) * (1.0 + a * a)
    u = (m * lax.rsqrt(jnp.maximum(m, F32_TINY))) * (i * xc)

    q_ref[...] = project(2 * D_LRU, D_SB).astype(BF16).reshape(S, L, D_SB)

    hl, h_last = _lru_scan(a, u, hc_ref[...], a_scr, b_scr)
    hc_ref[...] = h_last
    h_out_ref[...] = h_last

    o = hl * jax.nn.gelu(gl)
    olru_ref[...] = _rms(o, g_lru_ref[...]).astype(BF16)


def _front(x, mod, conv0, h0, g_pre, w_in_bf, conv_w, conv_b, w_gate_bf, b_gate, lam, g_lru,
           *, n_rows, name):
    S, T, _ = x.shape
    L = n_rows
    n_t = T // L
    const2 = lambda t: (0, 0)
    const3 = lambda t: (0, 0, 0)
    tile3 = lambda t: (0, t, 0)
    kernel = functools.partial(_front_kernel, n_seq=S, n_rows=L)
    return pl.pallas_call(
        kernel,
        grid=(n_t,),
        in_specs=[
            pl.BlockSpec((S, L, D_MODEL), tile3),
            pl.BlockSpec((S, N_MOD, D_MODEL), const3),
            pl.BlockSpec((S, CONV_CARRY_ROWS, D_LRU), const3),
            pl.BlockSpec((S, 1, D_LRU), const3),
            pl.BlockSpec((1, D_MODEL), const2),
            pl.BlockSpec((D_MODEL, D_IN), const2),
            pl.BlockSpec((CONV_WIDTH, D_LRU), const2),
            pl.BlockSpec((1, D_LRU), const2),
            pl.BlockSpec((D_LRU, 2 * D_LRU), const2),
            pl.BlockSpec((1, 2 * D_LRU), const2),
            pl.BlockSpec((1, D_LRU), const2),
            pl.BlockSpec((1, D_LRU), const2),
        ],
        out_specs=[
            pl.BlockSpec((S, L, D_LRU), tile3),
            pl.BlockSpec((S, L, D_SB), tile3),
            pl.BlockSpec((S, L, D_SB), tile3),
            pl.BlockSpec((S, L, D_SB), tile3),
            pl.BlockSpec((S, L, D_SB), tile3),
            pl.BlockSpec((S, L, D_SB), tile3),
            pl.BlockSpec((S, CONV_WIDTH - 1, D_LRU), const3),
            pl.BlockSpec((S, 1, D_LRU), const3),
        ],
        out_shape=[
            jax.ShapeDtypeStruct((S, T, D_LRU), BF16),
            jax.ShapeDtypeStruct((S, T, D_SB), BF16),
            jax.ShapeDtypeStruct((S, T, D_SB), BF16),
            jax.ShapeDtypeStruct((S, T, D_SB), BF16),
            jax.ShapeDtypeStruct((S, T, D_SB), F32),
            jax.ShapeDtypeStruct((S, T, D_SB), F32),
            jax.ShapeDtypeStruct((S, CONV_WIDTH - 1, D_LRU), F32),
            jax.ShapeDtypeStruct((S, 1, D_LRU), F32),
        ],
        scratch_shapes=[
            pltpu.VMEM((S, CONV_CARRY_ROWS + L, D_LRU), F32),
            pltpu.VMEM((S, 1, D_LRU), F32),
            pltpu.VMEM((D_LRU // LANES, S * L, LANES), F32),
            pltpu.VMEM((D_LRU // LANES, S * L, LANES), F32),
        ],
        compiler_params=pltpu.CompilerParams(
            dimension_semantics=("arbitrary",), vmem_limit_bytes=VMEM_LIMIT_BYTES),
        name=name,
    )(x, mod, conv0, h0, g_pre, w_in_bf, conv_w, conv_b, w_gate_bf, b_gate, lam, g_lru)


def _prefix_ones(n):
    j = lax.broadcasted_iota(jnp.int32, (n, n), 0)
    s = lax.broadcasted_iota(jnp.int32, (n, n), 1)
    upper = jnp.where(j >= s, -1.0, 0.0).astype(BF16)
    return jnp.concatenate([upper, upper], axis=0)


def _stack_heads(x):
    lane = lax.broadcasted_iota(jnp.int32, x.shape, 1)
    zero = jnp.zeros_like(x)
    return jnp.concatenate([jnp.where(lane < SB_HEAD_DIM, x, zero),
                            jnp.where(lane >= SB_HEAD_DIM, x, zero)], axis=0)


def _pair_cols(p):
    return slice(p * HEAD_PAIR, (p + 1) * HEAD_PAIR)


def _sweep_tiles(q_pairs, tiles, state):
    accs, carries = state
    tq = accs[0].shape[0]
    pairs = range(N_PAIRS)
    lane = lax.broadcasted_iota(jnp.int32, (tq, HEAD_PAIR), 1)

    def scores(kt):
        return [lax.dot_general(q_pairs[p], kt[:, _pair_cols(p)], (((1,), (1,)), ((), ())),
                                preferred_element_type=F32) for p in pairs]

    def drop_terms(z, mask):
        drop, split = [], []
        for p in pairs:
            sp = jnp.maximum(z[p], 0.0) + jnp.log(1.0 + jnp.exp2(jnp.abs(z[p]) * -LOG2_E))
            if mask is not None:
                sp = jnp.where(mask, sp, 0.0)
            hi = sp.astype(BF16)
            lo = (sp - hi.astype(F32)).astype(BF16)
            drop.append(sp)
            split.append(jnp.concatenate([hi, lo], axis=1))
        return drop, jnp.concatenate(split, axis=0)

    def weigh(z, log_keep_from, carries, mask, vt, accs):
        new_accs = []
        for p in pairs:
            w = jnp.exp(z[p] + log_keep_from[2 * tq * p:2 * tq * (p + 1)] + carries[p])
            if mask is not None:
                w = jnp.where(mask, w, 0.0)
            pv = _dot(w.astype(BF16), vt[:, _pair_cols(p)])
            new_accs.append(accs[p] + jnp.where(lane < SB_HEAD_DIM, pv[:tq], pv[tq:]))
        return new_accs

    z = [scores(kt) for kt, _, _, _ in tiles]
    log_keep_from, tile_carries = [], []
    for n, (_, _, neg_tri2, mask) in enumerate(tiles):
        drop, split = drop_terms(z[n], mask)
        log_keep_from.append(_dot(split, neg_tri2))
        tile_carries.append(carries)
        carries = tuple(carries[p] - jnp.sum(drop[p], axis=-1, keepdims=True) for p in pairs)
    for n, (_, vt, _, mask) in enumerate(tiles):
        accs = weigh(z[n], log_keep_from[n], tile_carries[n], mask, vt, accs)
    return tuple(accs), carries


def _attn_kernel(q_ref, kd_ref, vd_ref, kp_ref, vp_ref, ksrc_ref, vsrc_ref, o_ref,
                 kt_ref, vt_ref, sem, *, tq, n_before):
    b = pl.program_id(0)
    i = pl.program_id(1)
    before = i * tq if n_before is None else n_before

    q = q_ref[...]
    q_pairs = [_stack_heads(q[:, p * HEAD_PAIR:(p + 1) * HEAD_PAIR]) for p in range(N_PAIRS)]

    state = (tuple(jnp.zeros((tq, HEAD_PAIR), F32) for _ in range(N_PAIRS)),
             tuple(jnp.zeros((2 * tq, 1), F32) for _ in range(N_PAIRS)))
    t_idx = lax.broadcasted_iota(jnp.int32, (tq, tq), 0)
    s_idx = lax.broadcasted_iota(jnp.int32, (tq, tq), 1)
    causal = s_idx < t_idx
    own_tile = (kd_ref[...], vd_ref[...], _prefix_ones(tq), jnp.concatenate([causal, causal], axis=0))

    def write(state):
        o_ref[...] = jnp.concatenate(state[0], axis=1)

    def own_and_earlier_keys(state):
        tri = _prefix_ones(KEY_TILE)
        prev_tile = (kp_ref[...].astype(BF16), vp_ref[...].astype(BF16), tri, None)
        state = _sweep_tiles(q_pairs, [own_tile, prev_tile], state)
        n_tail = before // KEY_TILE - 1

        def live(carries):
            return jnp.max(functools.reduce(jnp.maximum, carries))

        def cond(c):
            n, top, _ = c
            return jnp.logical_and(n < n_tail, top > LOG_WEIGHT_FLOOR)

        def body(c):
            n, _, state = c
            rows = pl.ds(pl.multiple_of(before - (n + 2) * KEY_TILE, KEY_TILE), KEY_TILE)
            copies = (pltpu.make_async_copy(ksrc_ref.at[b, rows, :], kt_ref, sem.at[0]),
                      pltpu.make_async_copy(vsrc_ref.at[b, rows, :], vt_ref, sem.at[1]))
            for cp in copies:
                cp.start()
            for cp in copies:
                cp.wait()
            tile = (kt_ref[...].astype(BF16), vt_ref[...].astype(BF16), tri, None)
            state = _sweep_tiles(q_pairs, [tile], state)
            return n + 1, live(state[1]), state

        return lax.while_loop(cond, body, (jnp.int32(0), live(state[1]), state))[2]

    if n_before is None:
        @pl.when(i == 0)
        def _():
            write(_sweep_tiles(q_pairs, [own_tile], state))

        @pl.when(i > 0)
        def _():
            write(own_and_earlier_keys(state))
    else:
        write(own_and_earlier_keys(state))


def _attention(q, k_new, v_new, k_before, v_before, *, tq, name):
    B, T, _ = q.shape
    n_q = T // tq
    if k_before is None:
        assert tq == KEY_TILE
        k_before, v_before, n_before = k_new, v_new, None
        prev_idx = lambda b, i: (b, jnp.maximum(i - 1, 0), 0)
    else:
        assert n_q == 1
        n_before = k_before.shape[1]
        prev_idx = lambda b, i: (b, n_before // KEY_TILE - 1, 0)
    tile = lambda b, i: (b, i, 0)
    new_spec = pl.BlockSpec((None, tq, D_SB), tile)
    prev_spec = pl.BlockSpec((None, KEY_TILE, D_SB), prev_idx)
    hbm_spec = pl.BlockSpec(memory_space=pl.ANY)
    kernel = functools.partial(_attn_kernel, tq=tq, n_before=n_before)
    return pl.pallas_call(
        kernel,
        grid=(B, n_q),
        in_specs=[new_spec, new_spec, new_spec, prev_spec, prev_spec, hbm_spec, hbm_spec],
        out_specs=pl.BlockSpec((None, tq, D_SB), tile),
        out_shape=jax.ShapeDtypeStruct((B, T, D_SB), F32),
        scratch_shapes=[
            pltpu.VMEM((KEY_TILE, D_SB), k_before.dtype),
            pltpu.VMEM((KEY_TILE, D_SB), v_before.dtype),
            pltpu.SemaphoreType.DMA((2,)),
        ],
        compiler_params=pltpu.CompilerParams(
            dimension_semantics=("arbitrary", "arbitrary"), vmem_limit_bytes=VMEM_LIMIT_BYTES),
        name=name,
    )(q, k_new, v_new, k_before, v_before, k_before, v_before)


def _back_kernel(x_ref, olru_ref, osb_ref, mod_ref, g_sb_ref, g_post_mix_ref, g_pre_mlp_ref,
                 g_post_mlp_ref, w_out_ref, w_up_ref, w_down_ref, y_ref, *, n_seq, n_rows, ff_chunk):
    S, L = n_seq, n_rows
    if S > 1:
        halves = [(slice(0, S // 2), slice(None)), (slice(S // 2, S), slice(None))]
        hs, hl = S // 2, L
    else:
        halves = [(slice(None), slice(0, L // 2)), (slice(None), slice(L // 2, L))]
        hs, hl = S, L // 2
    n_chunks = D_FF // ff_chunk

    def mod(idx, row):
        return mod_ref[idx[0], row:row + 1, :]

    def attn_out(idx):
        o_lru = olru_ref[idx[0], idx[1], :].reshape(hs * hl, D_LRU)
        o_sb = _rms(osb_ref[idx[0], idx[1], :], g_sb_ref[...]).reshape(hs * hl, D_SB).astype(BF16)
        return _dot(o_lru, w_out_ref[0:D_LRU, :]) + _dot(o_sb, w_out_ref[D_LRU:, :])

    def mix_residual(idx, o):
        x = x_ref[idx[0], idx[1], :] + mod(idx, 2) * _rms(o, g_post_mix_ref[...]).reshape(hs, hl, D_MODEL)
        h = _rms(x, g_pre_mlp_ref[...] * (1.0 + mod(idx, 4))) + mod(idx, 3)
        return x, h.reshape(hs * hl, D_MODEL).astype(BF16)

    def mlp_chunk(hb, f, c):
        cols = slice(c * ff_chunk, (c + 1) * ff_chunk)
        up = jnp.maximum(_dot(hb, w_up_ref[:, cols]), 0.0)
        return f + _dot((up * up).astype(BF16), w_down_ref[cols, :])

    def mlp_residual(idx, x, f):
        y_ref[idx[0], idx[1], :] = x + mod(idx, 5) * _rms(f, g_post_mlp_ref[...]).reshape(hs, hl, D_MODEL)

    first, second = halves
    zero = jnp.zeros((hs * hl, D_MODEL), F32)
    o1 = attn_out(first)
    o2 = attn_out(second)
    x1, hb1 = mix_residual(first, o1)
    f1 = mlp_chunk(hb1, zero, 0)
    x2, hb2 = mix_residual(second, o2)
    for c in range(1, n_chunks):
        f1 = mlp_chunk(hb1, f1, c)
    f2 = mlp_chunk(hb2, zero, 0)
    mlp_residual(first, x1, f1)
    for c in range(1, n_chunks):
        f2 = mlp_chunk(hb2, f2, c)
    mlp_residual(second, x2, f2)


def _back(x, o_lru, o_sb, mod, g_sb, g_post_mix, g_pre_mlp, g_post_mlp, w_out_bf, w_up_bf, w_down_bf,
          *, n_seq, n_rows, name, ff_chunk=1024):
    B, T, _ = x.shape
    S, L = n_seq, n_rows
    const2 = lambda b, t: (0, 0)
    tile3 = lambda b, t: (b, t, 0)
    seq3 = lambda b, t: (b, 0, 0)
    kernel = functools.partial(_back_kernel, n_seq=S, n_rows=L, ff_chunk=ff_chunk)
    return pl.pallas_call(
        kernel,
        grid=(B // S, T // L),
        in_specs=[
            pl.BlockSpec((S, L, D_MODEL), tile3),
            pl.BlockSpec((S, L, D_LRU), tile3),
            pl.BlockSpec((S, L, D_SB), tile3),
            pl.BlockSpec((S, N_MOD, D_MODEL), seq3),
            pl.BlockSpec((1, D_SB), const2),
            pl.BlockSpec((1, D_MODEL), const2),
            pl.BlockSpec((1, D_MODEL), const2),
            pl.BlockSpec((1, D_MODEL), const2),
            pl.BlockSpec((D_MODEL, D_MODEL), const2),
            pl.BlockSpec((D_MODEL, D_FF), const2),
            pl.BlockSpec((D_FF, D_MODEL), const2),
        ],
        out_specs=pl.BlockSpec((S, L, D_MODEL), tile3),
        out_shape=jax.ShapeDtypeStruct((B, T, D_MODEL), F32),
        compiler_params=pltpu.CompilerParams(
            dimension_semantics=("arbitrary", "arbitrary"), vmem_limit_bytes=VMEM_LIMIT_BYTES),
        name=name,
    )(x, o_lru, o_sb, mod, g_sb, g_post_mix, g_pre_mlp, g_post_mlp, w_out_bf, w_up_bf, w_down_bf)


def _block_diag(w):
    n, bi, bj = w.shape
    eye = jnp.eye(n, dtype=w.dtype)
    return (eye[:, None, :, None] * w[:, :, None, :]).reshape(n * bi, n * bj)


def _pad_conv_state(buf):
    return jnp.pad(buf, ((0, 0), (CONV_CARRY_ROWS - (CONV_WIDTH - 1), 0), (0, 0)))


def kernel(x_prompt, x_sample, c_prompt, c_sample, cache_conv, state_lru, cache_k, cache_v, w_ada, b_ada, g_pre_mix, g_post_mix, g_pre_mlp, g_post_mlp, w_in, conv_w, conv_b, w_rg_a, b_rg_a, w_rg_x, b_rg_x, lru_lambda, g_lru_out, g_sb_out, w_out, w_up, w_down):
    depth = w_ada.shape[0]
    bp, seq, _ = x_prompt.shape
    bs, dec_seq, _ = x_sample.shape
    assert depth == 1 and bp == 1, "single-layer, single-prompt configuration"
    past_len = cache_k.shape[2]
    l = 0

    row = lambda a: a.reshape(1, -1)
    q_cols = jnp.arange(D_IN) // D_SB == 2 * D_LRU // D_SB
    w_in_bf = (w_in[l] * jnp.where(q_cols, SB_HEAD_DIM ** -0.5, 1.0)).astype(BF16)
    w_gate_bf = jnp.concatenate([_block_diag(w_rg_a[l]), _block_diag(w_rg_x[l])], axis=1).astype(BF16)
    b_gate = jnp.concatenate([b_rg_a[l].reshape(-1), b_rg_x[l].reshape(-1)]).reshape(1, -1)
    w_out_bf = w_out[l].astype(BF16)
    w_up_bf = w_up[l].astype(BF16)
    w_down_bf = w_down[l].astype(BF16)
    front_w = (row(g_pre_mix[l]), w_in_bf, conv_w[l], row(conv_b[l]), w_gate_bf, b_gate,
               row(lru_lambda[l]), row(g_lru_out[l]))
    back_w = (row(g_sb_out[l]), row(g_post_mix[l]), row(g_pre_mlp[l]), row(g_post_mlp[l]),
              w_out_bf, w_up_bf, w_down_bf)

    mod = _adaln(jnp.concatenate([c_prompt, c_sample], axis=0), w_ada[l], b_ada[l])
    mod = mod.reshape(bp + bs, N_MOD, D_MODEL)
    mod_p, mod_s = mod[:bp], mod[bp:]

    zeros_conv = jnp.zeros((bp, CONV_CARRY_ROWS, D_LRU), F32)
    zeros_h = jnp.zeros((bp, 1, D_LRU), F32)
    olru_p, q_p, kbf_p, vbf_p, k_p, v_p, conv_p, h_p = _front(
        x_prompt, mod_p, zeros_conv, zeros_h, *front_w, n_rows=512, name="front_prompt")
    osb_p = _attention(q_p, kbf_p, vbf_p, None, None, tq=KEY_TILE, name="attn_prompt")
    y_p = _back(x_prompt, olru_p, osb_p, mod_p, *back_w, n_seq=1, n_rows=512, name="back_prompt")

    olru_s, q_s, kbf_s, vbf_s, k_s, v_s, conv_s, h_s = _front(
        x_sample, mod_s, _pad_conv_state(cache_conv[l]), state_lru[l].reshape(bs, 1, D_LRU),
        *front_w, n_rows=dec_seq, name="front_sample")
    osb_s = _attention(q_s, kbf_s, vbf_s, cache_k[l].reshape(bs, past_len, D_SB),
                       cache_v[l].reshape(bs, past_len, D_SB), tq=dec_seq, name="attn_sample")
    y_s = _back(x_sample, olru_s, osb_s, mod_s, *back_w, n_seq=8, n_rows=dec_seq, name="back_sample")

    heads = lambda a: a.reshape(1, a.shape[0], a.shape[1], N_SB_HEADS, SB_HEAD_DIM)
    return (y_p, y_s, conv_p[None], h_p.reshape(1, bp, D_LRU), heads(k_p), heads(v_p),
            conv_s[None], h_s.reshape(1, bs, D_LRU), heads(k_s), heads(v_s))
```

```python
import functools

import jax
import jax.numpy as jnp
from jax import lax
from jax.experimental import pallas as pl
from jax.experimental.pallas import tpu as pltpu

D_MODEL = 1024
D_LRU = 512
D_SB = 512
N_SB_HEADS = 8
SB_HEAD_DIM = 64
HEAD_PAIR = 2 * SB_HEAD_DIM
N_PAIRS = N_SB_HEADS // 2
CONV_WIDTH = 4
CONV_CARRY_ROWS = 8
RG_C = 8.0
D_FF = 4 * D_MODEL
D_IN = 2 * D_LRU + 3 * D_SB
N_MOD = 6
EPS = 1e-6
LOG2_E = 1.4426950408889634
F32_TINY = 1.1754943508222875e-38

KEY_TILE = 256
LOG_WEIGHT_FLOOR = -110.0

VMEM_LIMIT_BYTES = 56 * 1024 * 1024

BF16 = jnp.bfloat16
F32 = jnp.float32


def _rms(x, g):
    return x * lax.rsqrt(jnp.mean(x * x, axis=-1, keepdims=True) + EPS) * g


def _dot(a, b):
    return jnp.dot(a, b, preferred_element_type=F32)


def _adaln_kernel(c_ref, w_ref, b_ref, o_ref):
    c = c_ref[...]
    s = (c * jax.nn.sigmoid(c)).astype(BF16)
    o_ref[...] = _dot(s, w_ref[...].astype(BF16)) + b_ref[...]


def _adaln(c_all, w_ada, b_ada, *, tn=1536):
    n_rows = c_all.shape[0]
    n_out = w_ada.shape[1]
    return pl.pallas_call(
        _adaln_kernel,
        grid=(n_out // tn,),
        in_specs=[
            pl.BlockSpec((n_rows, D_MODEL), lambda j: (0, 0)),
            pl.BlockSpec((D_MODEL, tn), lambda j: (0, j)),
            pl.BlockSpec((1, tn), lambda j: (0, j)),
        ],
        out_specs=pl.BlockSpec((n_rows, tn), lambda j: (0, j)),
        out_shape=jax.ShapeDtypeStruct((n_rows, n_out), F32),
        compiler_params=pltpu.CompilerParams(
            dimension_semantics=("arbitrary",), vmem_limit_bytes=VMEM_LIMIT_BYTES),
        name="adaln",
    )(c_all, w_ada, b_ada.reshape(1, n_out))


def _shift_rows(x, s, fill, row):
    return jnp.where(row >= s, pltpu.roll(x, s, axis=1), fill)


SUBLANES = 8
LANES = 128


def _lru_scan(a, u, h_in, a_scr, b_scr):
    S, L, C = a.shape
    G = L // SUBLANES
    ag = a.reshape(S * G, SUBLANES, C)
    bg = u.reshape(S * G, SUBLANES, C)
    sub = lax.broadcasted_iota(jnp.int32, ag.shape, 1)
    step = 1
    while step < SUBLANES:
        bg = ag * _shift_rows(bg, step, 0.0, sub) + bg
        ag = ag * _shift_rows(ag, step, 1.0, sub)
        step *= 2
    last_rows = pl.ds(SUBLANES - 1, S * G, stride=SUBLANES)

    def group_ends(x, scr):
        x = x.reshape(S * L, C)
        for c in range(C // LANES):
            scr[c] = x[:, c * LANES:(c + 1) * LANES]
        ends = [scr[c, last_rows, :] for c in range(C // LANES)]
        return jnp.concatenate(ends, axis=1).reshape(S, G, C)

    a_sum = group_ends(ag, a_scr)
    b_sum = group_ends(bg, b_scr)
    grp = lax.broadcasted_iota(jnp.int32, a_sum.shape, 1)
    step = 1
    while step < G:
        b_sum = a_sum * _shift_rows(b_sum, step, 0.0, grp) + b_sum
        a_sum = a_sum * _shift_rows(a_sum, step, 1.0, grp)
        step *= 2
    h_end = b_sum + a_sum * h_in
    h_enter = _shift_rows(h_end, 1, h_in, grp)
    h = bg + ag * h_enter.reshape(S * G, 1, C)
    return h.reshape(S, L, C), h_end[:, G - 1:G, :]


def _front_kernel(x_ref, mod_ref, conv0_ref, h0_ref, g_pre_ref, w_in_ref, conv_w_ref, conv_b_ref,
                  w_gate_ref, b_gate_ref, lam_ref, g_lru_ref,
                  olru_ref, q_ref, kbf_ref, vbf_ref, k_ref, v_ref, conv_out_ref, h_out_ref,
                  ext_ref, hc_ref, a_scr, b_scr, *, n_seq, n_rows):
    S, L = n_seq, n_rows

    @pl.when(pl.program_id(0) == 0)
    def _():
        ext_ref[:, 0:CONV_CARRY_ROWS, :] = conv0_ref[...]
        hc_ref[...] = h0_ref[...]

    x = x_ref[...]
    shift = mod_ref[:, 0:1, :]
    scale = mod_ref[:, 1:2, :]
    h = _rms(x, g_pre_ref[...] * (1.0 + scale)) + shift
    hb = h.reshape(S * L, D_MODEL).astype(BF16)

    def project(first_col, n_cols):
        return _dot(hb, w_in_ref[:, first_col:first_col + n_cols])

    lru_in = project(0, 2 * D_LRU)
    xl = lru_in[:, 0:D_LRU].reshape(S, L, D_LRU)
    gl = lru_in[:, D_LRU:].reshape(S, L, D_LRU)

    k = project(2 * D_LRU + D_SB, D_SB)
    k_ref[...] = k.reshape(S, L, D_SB)
    kbf_ref[...] = k.astype(BF16).reshape(S, L, D_SB)

    ext_ref[:, CONV_CARRY_ROWS:, :] = xl
    cw = conv_w_ref[...]
    xc = conv_b_ref[...]
    for j in range(CONV_WIDTH - 1):
        start = CONV_CARRY_ROWS - (CONV_WIDTH - 1 - j)
        xc = xc + ext_ref[:, start:start + L, :] * cw[j:j + 1, :]
    xc = xc + xl * cw[CONV_WIDTH - 1:CONV_WIDTH, :]
    conv_out_ref[...] = ext_ref[:, L + CONV_CARRY_ROWS - (CONV_WIDTH - 1):, :]
    ext_ref[:, 0:CONV_CARRY_ROWS, :] = xl[:, L - CONV_CARRY_ROWS:, :]

    gates = _dot(xc.reshape(S * L, D_LRU).astype(BF16), w_gate_ref[...]) + b_gate_ref[...]
    v = project(2 * D_LRU + 2 * D_SB, D_SB)
    v_ref[...] = v.reshape(S, L, D_SB)
    vbf_ref[...] = v.astype(BF16).reshape(S, L, D_SB)

    r = jax.nn.sigmoid(gates[:, 0:D_LRU]).reshape(S, L, D_LRU)
    i = jax.nn.sigmoid(gates[:, D_LRU:]).reshape(S, L, D_LRU)
    neg_lam = -lam_ref[...]
    decay = RG_C * (jnp.maximum(neg_lam, 0.0) + jnp.log1p(jnp.exp(-jnp.abs(neg_lam))))
    neg_log_a = r * decay
    a = jnp.exp2(r * (decay * -LOG2_E))
    m = jnp.tanh(neg_log_a) * (1.0 + a * a)
    u = (m * lax.rsqrt(jnp.maximum(m, F32_TINY))) * (i * xc)

    q_ref[...] = project(2 * D_LRU, D_SB).astype(BF16).reshape(S, L, D_SB)

    hl, h_last = _lru_scan(a, u, hc_ref[...], a_scr, b_scr)
    hc_ref[...] = h_last
    h_out_ref[...] = h_last

    o = hl * jax.nn.gelu(gl)
    olru_ref[...] = _rms(o, g_lru_ref[...]).astype(BF16)


def _front(x, mod, conv0, h0, g_pre, w_in_bf, conv_w, conv_b, w_gate_bf, b_gate, lam, g_lru,
           *, n_rows, name):
    S, T, _ = x.shape
    L = n_rows
    n_t = T // L
    const2 = lambda t: (0, 0)
    const3 = lambda t: (0, 0, 0)
    tile3 = lambda t: (0, t, 0)
    kernel = functools.partial(_front_kernel, n_seq=S, n_rows=L)
    return pl.pallas_call(
        kernel,
        grid=(n_t,),
        in_specs=[
            pl.BlockSpec((S, L, D_MODEL), tile3),
            pl.BlockSpec((S, N_MOD, D_MODEL), const3),
            pl.BlockSpec((S, CONV_CARRY_ROWS, D_LRU), const3),
            pl.BlockSpec((S, 1, D_LRU), const3),
            pl.BlockSpec((1, D_MODEL), const2),
            pl.BlockSpec((D_MODEL, D_IN), const2),
            pl.BlockSpec((CONV_WIDTH, D_LRU), const2),
            pl.BlockSpec((1, D_LRU), const2),
            pl.BlockSpec((D_LRU, 2 * D_LRU), const2),
            pl.BlockSpec((1, 2 * D_LRU), const2),
            pl.BlockSpec((1, D_LRU), const2),
            pl.BlockSpec((1, D_LRU), const2),
        ],
        out_specs=[
            pl.BlockSpec((S, L, D_LRU), tile3),
            pl.BlockSpec((S, L, D_SB), tile3),
            pl.BlockSpec((S, L, D_SB), tile3),
            pl.BlockSpec((S, L, D_SB), tile3),
            pl.BlockSpec((S, L, D_SB), tile3),
            pl.BlockSpec((S, L, D_SB), tile3),
            pl.BlockSpec((S, CONV_WIDTH - 1, D_LRU), const3),
            pl.BlockSpec((S, 1, D_LRU), const3),
        ],
        out_shape=[
            jax.ShapeDtypeStruct((S, T, D_LRU), BF16),
            jax.ShapeDtypeStruct((S, T, D_SB), BF16),
            jax.ShapeDtypeStruct((S, T, D_SB), BF16),
            jax.ShapeDtypeStruct((S, T, D_SB), BF16),
            jax.ShapeDtypeStruct((S, T, D_SB), F32),
            jax.ShapeDtypeStruct((S, T, D_SB), F32),
            jax.ShapeDtypeStruct((S, CONV_WIDTH - 1, D_LRU), F32),
            jax.ShapeDtypeStruct((S, 1, D_LRU), F32),
        ],
        scratch_shapes=[
            pltpu.VMEM((S, CONV_CARRY_ROWS + L, D_LRU), F32),
            pltpu.VMEM((S, 1, D_LRU), F32),
            pltpu.VMEM((D_LRU // LANES, S * L, LANES), F32),
            pltpu.VMEM((D_LRU // LANES, S * L, LANES), F32),
        ],
        compiler_params=pltpu.CompilerParams(
            dimension_semantics=("arbitrary",), vmem_limit_bytes=VMEM_LIMIT_BYTES),
        name=name,
    )(x, mod, conv0, h0, g_pre, w_in_bf, conv_w, conv_b, w_gate_bf, b_gate, lam, g_lru)


def _prefix_ones(n):
    j = lax.broadcasted_iota(jnp.int32, (n, n), 0)
    s = lax.broadcasted_iota(jnp.int32, (n, n), 1)
    return jnp.where(j > s, -1.0, 0.0).astype(BF16)


def _stack_heads(x):
    lane = lax.broadcasted_iota(jnp.int32, x.shape, 1)
    zero = jnp.zeros_like(x)
    return jnp.concatenate([jnp.where(lane < SB_HEAD_DIM, x, zero),
                            jnp.where(lane >= SB_HEAD_DIM, x, zero)], axis=0)


def _pair_cols(p):
    return slice(p * HEAD_PAIR, (p + 1) * HEAD_PAIR)


def _sweep_tiles(q_pairs, tiles, state):
    accs, carries = state
    tq = accs[0].shape[0]
    pairs = range(N_PAIRS)
    lane = lax.broadcasted_iota(jnp.int32, (tq, HEAD_PAIR), 1)

    def scores(kt):
        return [lax.dot_general(q_pairs[p], kt[:, _pair_cols(p)], (((1,), (1,)), ((), ())),
                                preferred_element_type=F32) for p in pairs]

    def drop_terms(z, mask):
        log_beta, drop, stacked = [], [], []
        for p in pairs:
            sp = jnp.maximum(z[p], 0.0) + jnp.log(1.0 + jnp.exp2(jnp.abs(z[p]) * -LOG2_E))
            log_beta.append(z[p] - sp)
            if mask is not None:
                sp = jnp.where(mask, sp, 0.0)
            drop.append(sp)
            stacked.append(sp.astype(BF16))
        return log_beta, drop, jnp.concatenate(stacked, axis=0)

    def weigh(log_beta, log_keep_after, carries, mask, vt, accs):
        new_accs = []
        for p in pairs:
            w = jnp.exp(log_beta[p] + log_keep_after[2 * tq * p:2 * tq * (p + 1)] + carries[p])
            if mask is not None:
                w = jnp.where(mask, w, 0.0)
            pv = _dot(w.astype(BF16), vt[:, _pair_cols(p)])
            new_accs.append(accs[p] + jnp.where(lane < SB_HEAD_DIM, pv[:tq], pv[tq:]))
        return new_accs

    z = [scores(kt) for kt, _, _, _ in tiles]
    log_beta, log_keep_after, tile_carries = [], [], []
    for n, (_, _, neg_ones, mask) in enumerate(tiles):
        lb, drop, stacked = drop_terms(z[n], mask)
        log_beta.append(lb)
        log_keep_after.append(_dot(stacked, neg_ones))
        tile_carries.append(carries)
        carries = tuple(carries[p] - jnp.sum(drop[p], axis=-1, keepdims=True) for p in pairs)
    for n, (_, vt, _, mask) in enumerate(tiles):
        accs = weigh(log_beta[n], log_keep_after[n], tile_carries[n], mask, vt, accs)
    return tuple(accs), carries


def _attn_kernel(q_ref, kd_ref, vd_ref, kp_ref, vp_ref, ksrc_ref, vsrc_ref, o_ref,
                 kt_ref, vt_ref, sem, *, tq, n_before):
    b = pl.program_id(0)
    i = pl.program_id(1)
    before = i * tq if n_before is None else n_before

    q = q_ref[...]
    q_pairs = [_stack_heads(q[:, p * HEAD_PAIR:(p + 1) * HEAD_PAIR]) for p in range(N_PAIRS)]

    state = (tuple(jnp.zeros((tq, HEAD_PAIR), F32) for _ in range(N_PAIRS)),
             tuple(jnp.zeros((2 * tq, 1), F32) for _ in range(N_PAIRS)))
    t_idx = lax.broadcasted_iota(jnp.int32, (tq, tq), 0)
    s_idx = lax.broadcasted_iota(jnp.int32, (tq, tq), 1)
    causal = s_idx < t_idx
    own_tile = (kd_ref[...], vd_ref[...], _prefix_ones(tq), jnp.concatenate([causal, causal], axis=0))

    def write(state):
        o_ref[...] = jnp.concatenate(state[0], axis=1)

    def own_and_earlier_keys(state):
        tri = _prefix_ones(KEY_TILE)
        prev_tile = (kp_ref[...].astype(BF16), vp_ref[...].astype(BF16), tri, None)
        state = _sweep_tiles(q_pairs, [own_tile, prev_tile], state)
        n_tail = before // KEY_TILE - 1

        def live(carries):
            return jnp.max(functools.reduce(jnp.maximum, carries))

        def cond(c):
            n, top, _ = c
            return jnp.logical_and(n < n_tail, top > LOG_WEIGHT_FLOOR)

        def body(c):
            n, _, state = c
            rows = pl.ds(pl.multiple_of(before - (n + 2) * KEY_TILE, KEY_TILE), KEY_TILE)
            copies = (pltpu.make_async_copy(ksrc_ref.at[b, rows, :], kt_ref, sem.at[0]),
                      pltpu.make_async_copy(vsrc_ref.at[b, rows, :], vt_ref, sem.at[1]))
            for cp in copies:
                cp.start()
            for cp in copies:
                cp.wait()
            tile = (kt_ref[...].astype(BF16), vt_ref[...].astype(BF16), tri, None)
            state = _sweep_tiles(q_pairs, [tile], state)
            return n + 1, live(state[1]), state

        return lax.while_loop(cond, body, (jnp.int32(0), live(state[1]), state))[2]

    if n_before is None:
        @pl.when(i == 0)
        def _():
            write(_sweep_tiles(q_pairs, [own_tile], state))

        @pl.when(i > 0)
        def _():
            write(own_and_earlier_keys(state))
    else:
        write(own_and_earlier_keys(state))


def _attention(q, k_new, v_new, k_before, v_before, *, tq, name):
    B, T, _ = q.shape
    n_q = T // tq
    if k_before is None:
        assert tq == KEY_TILE
        k_before, v_before, n_before = k_new, v_new, None
        prev_idx = lambda b, i: (b, jnp.maximum(i - 1, 0), 0)
    else:
        assert n_q == 1
        n_before = k_before.shape[1]
        prev_idx = lambda b, i: (b, n_before // KEY_TILE - 1, 0)
    tile = lambda b, i: (b, i, 0)
    new_spec = pl.BlockSpec((None, tq, D_SB), tile)
    prev_spec = pl.BlockSpec((None, KEY_TILE, D_SB), prev_idx)
    hbm_spec = pl.BlockSpec(memory_space=pl.ANY)
    kernel = functools.partial(_attn_kernel, tq=tq, n_before=n_before)
    return pl.pallas_call(
        kernel,
        grid=(B, n_q),
        in_specs=[new_spec, new_spec, new_spec, prev_spec, prev_spec, hbm_spec, hbm_spec],
        out_specs=pl.BlockSpec((None, tq, D_SB), tile),
        out_shape=jax.ShapeDtypeStruct((B, T, D_SB), F32),
        scratch_shapes=[
            pltpu.VMEM((KEY_TILE, D_SB), k_before.dtype),
            pltpu.VMEM((KEY_TILE, D_SB), v_before.dtype),
            pltpu.SemaphoreType.DMA((2,)),
        ],
        compiler_params=pltpu.CompilerParams(
            dimension_semantics=("arbitrary", "arbitrary"), vmem_limit_bytes=VMEM_LIMIT_BYTES),
        name=name,
    )(q, k_new, v_new, k_before, v_before, k_before, v_before)


def _back_kernel(x_ref, olru_ref, osb_ref, mod_ref, g_sb_ref, g_post_mix_ref, g_pre_mlp_ref,
                 g_post_mlp_ref, w_out_ref, w_up_ref, w_down_ref, y_ref, *, n_seq, n_rows, ff_chunk):
    S, L = n_seq, n_rows
    if S > 1:
        halves = [(slice(0, S // 2), slice(None)), (slice(S // 2, S), slice(None))]
        hs, hl = S // 2, L
    else:
        halves = [(slice(None), slice(0, L // 2)), (slice(None), slice(L // 2, L))]
        hs, hl = S, L // 2
    n_chunks = D_FF // ff_chunk

    def mod(idx, row):
        return mod_ref[idx[0], row:row + 1, :]

    def attn_out(idx):
        o_lru = olru_ref[idx[0], idx[1], :].reshape(hs * hl, D_LRU)
        o_sb = _rms(osb_ref[idx[0], idx[1], :], g_sb_ref[...]).reshape(hs * hl, D_SB).astype(BF16)
        return _dot(o_lru, w_out_ref[0:D_LRU, :]) + _dot(o_sb, w_out_ref[D_LRU:, :])

    def mix_residual(idx, o):
        x = x_ref[idx[0], idx[1], :] + mod(idx, 2) * _rms(o, g_post_mix_ref[...]).reshape(hs, hl, D_MODEL)
        h = _rms(x, g_pre_mlp_ref[...] * (1.0 + mod(idx, 4))) + mod(idx, 3)
        return x, h.reshape(hs * hl, D_MODEL).astype(BF16)

    def mlp_chunk(hb, f, c):
        cols = slice(c * ff_chunk, (c + 1) * ff_chunk)
        up = jnp.maximum(_dot(hb, w_up_ref[:, cols]), 0.0)
        return f + _dot((up * up).astype(BF16), w_down_ref[cols, :])

    def mlp_residual(idx, x, f):
        y_ref[idx[0], idx[1], :] = x + mod(idx, 5) * _rms(f, g_post_mlp_ref[...]).reshape(hs, hl, D_MODEL)

    first, second = halves
    zero = jnp.zeros((hs * hl, D_MODEL), F32)
    o1 = attn_out(first)
    o2 = attn_out(second)
    x1, hb1 = mix_residual(first, o1)
    f1 = mlp_chunk(hb1, zero, 0)
    x2, hb2 = mix_residual(second, o2)
    for c in range(1, n_chunks):
        f1 = mlp_chunk(hb1, f1, c)
    f2 = mlp_chunk(hb2, zero, 0)
    mlp_residual(first, x1, f1)
    for c in range(1, n_chunks):
        f2 = mlp_chunk(hb2, f2, c)
    mlp_residual(second, x2, f2)


def _back(x, o_lru, o_sb, mod, g_sb, g_post_mix, g_pre_mlp, g_post_mlp, w_out_bf, w_up_bf, w_down_bf,
          *, n_seq, n_rows, name, ff_chunk=1024):
    B, T, _ = x.shape
    S, L = n_seq, n_rows
    const2 = lambda b, t: (0, 0)
    tile3 = lambda b, t: (b, t, 0)
    seq3 = lambda b, t: (b, 0, 0)
    kernel = functools.partial(_back_kernel, n_seq=S, n_rows=L, ff_chunk=ff_chunk)
    return pl.pallas_call(
        kernel,
        grid=(B // S, T // L),
        in_specs=[
            pl.BlockSpec((S, L, D_MODEL), tile3),
            pl.BlockSpec((S, L, D_LRU), tile3),
            pl.BlockSpec((S, L, D_SB), tile3),
            pl.BlockSpec((S, N_MOD, D_MODEL), seq3),
            pl.BlockSpec((1, D_SB), const2),
            pl.BlockSpec((1, D_MODEL), const2),
            pl.BlockSpec((1, D_MODEL), const2),
            pl.BlockSpec((1, D_MODEL), const2),
            pl.BlockSpec((D_MODEL, D_MODEL), const2),
            pl.BlockSpec((D_MODEL, D_FF), const2),
            pl.BlockSpec((D_FF, D_MODEL), const2),
        ],
        out_specs=pl.BlockSpec((S, L, D_MODEL), tile3),
        out_shape=jax.ShapeDtypeStruct((B, T, D_MODEL), F32),
        compiler_params=pltpu.CompilerParams(
            dimension_semantics=("arbitrary", "arbitrary"), vmem_limit_bytes=VMEM_LIMIT_BYTES),
        name=name,
    )(x, o_lru, o_sb, mod, g_sb, g_post_mix, g_pre_mlp, g_post_mlp, w_out_bf, w_up_bf, w_down_bf)


def _block_diag(w):
    n, bi, bj = w.shape
    eye = jnp.eye(n, dtype=w.dtype)
    return (eye[:, None, :, None] * w[:, :, None, :]).reshape(n * bi, n * bj)


def _pad_conv_state(buf):
    return jnp.pad(buf, ((0, 0), (CONV_CARRY_ROWS - (CONV_WIDTH - 1), 0), (0, 0)))


def kernel(x_prompt, x_sample, c_prompt, c_sample, cache_conv, state_lru, cache_k, cache_v, w_ada, b_ada, g_pre_mix, g_post_mix, g_pre_mlp, g_post_mlp, w_in, conv_w, conv_b, w_rg_a, b_rg_a, w_rg_x, b_rg_x, lru_lambda, g_lru_out, g_sb_out, w_out, w_up, w_down):
    depth = w_ada.shape[0]
    bp, seq, _ = x_prompt.shape
    bs, dec_seq, _ = x_sample.shape
    assert depth == 1 and bp == 1, "single-layer, single-prompt configuration"
    past_len = cache_k.shape[2]
    l = 0

    row = lambda a: a.reshape(1, -1)
    q_cols = jnp.arange(D_IN) // D_SB == 2 * D_LRU // D_SB
    w_in_bf = (w_in[l] * jnp.where(q_cols, SB_HEAD_DIM ** -0.5, 1.0)).astype(BF16)
    w_gate_bf = jnp.concatenate([_block_diag(w_rg_a[l]), _block_diag(w_rg_x[l])], axis=1).astype(BF16)
    b_gate = jnp.concatenate([b_rg_a[l].reshape(-1), b_rg_x[l].reshape(-1)]).reshape(1, -1)
    w_out_bf = w_out[l].astype(BF16)
    w_up_bf = w_up[l].astype(BF16)
    w_down_bf = w_down[l].astype(BF16)
    front_w = (row(g_pre_mix[l]), w_in_bf, conv_w[l], row(conv_b[l]), w_gate_bf, b_gate,
               row(lru_lambda[l]), row(g_lru_out[l]))
    back_w = (row(g_sb_out[l]), row(g_post_mix[l]), row(g_pre_mlp[l]), row(g_post_mlp[l]),
              w_out_bf, w_up_bf, w_down_bf)

    mod = _adaln(jnp.concatenate([c_prompt, c_sample], axis=0), w_ada[l], b_ada[l])
    mod = mod.reshape(bp + bs, N_MOD, D_MODEL)
    mod_p, mod_s = mod[:bp], mod[bp:]

    zeros_conv = jnp.zeros((bp, CONV_CARRY_ROWS, D_LRU), F32)
    zeros_h = jnp.zeros((bp, 1, D_LRU), F32)
    olru_p, q_p, kbf_p, vbf_p, k_p, v_p, conv_p, h_p = _front(
        x_prompt, mod_p, zeros_conv, zeros_h, *front_w, n_rows=512, name="front_prompt")
    osb_p = _attention(q_p, kbf_p, vbf_p, None, None, tq=KEY_TILE, name="attn_prompt")
    y_p = _back(x_prompt, olru_p, osb_p, mod_p, *back_w, n_seq=1, n_rows=512, name="back_prompt")

    olru_s, q_s, kbf_s, vbf_s, k_s, v_s, conv_s, h_s = _front(
        x_sample, mod_s, _pad_conv_state(cache_conv[l]), state_lru[l].reshape(bs, 1, D_LRU),
        *front_w, n_rows=dec_seq, name="front_sample")
    osb_s = _attention(q_s, kbf_s, vbf_s, cache_k[l].reshape(bs, past_len, D_SB),
                       cache_v[l].reshape(bs, past_len, D_SB), tq=dec_seq, name="attn_sample")
    y_s = _back(x_sample, olru_s, osb_s, mod_s, *back_w, n_seq=8, n_rows=dec_seq, name="back_sample")

    heads = lambda a: a.reshape(1, a.shape[0], a.shape[1], N_SB_HEADS, SB_HEAD_DIM)
    return (y_p, y_s, conv_p[None], h_p.reshape(1, bp, D_LRU), heads(k_p), heads(v_p),
            conv_s[None], h_s.reshape(1, bs, D_LRU), heads(k_s), heads(v_s))
```

```python
import functools

import jax
import jax.numpy as jnp
from jax import lax
from jax.experimental import pallas as pl
from jax.experimental.pallas import tpu as pltpu

D_MODEL = 1024
D_LRU = 512
D_SB = 512
N_SB_HEADS = 8
SB_HEAD_DIM = 64
MXU_DEPTH = 256
HEADS_PER_GROUP = MXU_DEPTH // SB_HEAD_DIM
GROUP_LANES = HEADS_PER_GROUP * SB_HEAD_DIM
N_GROUPS = N_SB_HEADS // HEADS_PER_GROUP
CONV_WIDTH = 4
CONV_CARRY_ROWS = 8
RG_C = 8.0
D_FF = 4 * D_MODEL
D_IN = 2 * D_LRU + 3 * D_SB
N_MOD = 6
EPS = 1e-6
LOG2_E = 1.4426950408889634
F32_TINY = 1.1754943508222875e-38

KEY_TILE = 256
LOG_WEIGHT_FLOOR = -110.0

VMEM_LIMIT_BYTES = 56 * 1024 * 1024

BF16 = jnp.bfloat16
F32 = jnp.float32


def _rms(x, g):
    return x * lax.rsqrt(jnp.mean(x * x, axis=-1, keepdims=True) + EPS) * g


def _dot(a, b):
    return jnp.dot(a, b, preferred_element_type=F32)


def _adaln_kernel(c_ref, w_ref, b_ref, o_ref):
    c = c_ref[...]
    s = (c * jax.nn.sigmoid(c)).astype(BF16)
    o_ref[...] = _dot(s, w_ref[...].astype(BF16)) + b_ref[...]


def _adaln(c_all, w_ada, b_ada, *, tn=1536):
    n_rows = c_all.shape[0]
    n_out = w_ada.shape[1]
    return pl.pallas_call(
        _adaln_kernel,
        grid=(n_out // tn,),
        in_specs=[
            pl.BlockSpec((n_rows, D_MODEL), lambda j: (0, 0)),
            pl.BlockSpec((D_MODEL, tn), lambda j: (0, j)),
            pl.BlockSpec((1, tn), lambda j: (0, j)),
        ],
        out_specs=pl.BlockSpec((n_rows, tn), lambda j: (0, j)),
        out_shape=jax.ShapeDtypeStruct((n_rows, n_out), F32),
        compiler_params=pltpu.CompilerParams(
            dimension_semantics=("arbitrary",), vmem_limit_bytes=VMEM_LIMIT_BYTES),
        name="adaln",
    )(c_all, w_ada, b_ada.reshape(1, n_out))


def _shift_rows(x, s, fill, row):
    return jnp.where(row >= s, pltpu.roll(x, s, axis=1), fill)


SUBLANES = 8
LANES = 128


def _lru_scan(a, u, h_in, a_scr, b_scr):
    S, L, C = a.shape
    G = L // SUBLANES
    ag = a.reshape(S * G, SUBLANES, C)
    bg = u.reshape(S * G, SUBLANES, C)
    sub = lax.broadcasted_iota(jnp.int32, ag.shape, 1)
    step = 1
    while step < SUBLANES:
        bg = ag * _shift_rows(bg, step, 0.0, sub) + bg
        ag = ag * _shift_rows(ag, step, 1.0, sub)
        step *= 2
    last_rows = pl.ds(SUBLANES - 1, S * G, stride=SUBLANES)

    def group_ends(x, scr):
        x = x.reshape(S * L, C)
        for c in range(C // LANES):
            scr[c] = x[:, c * LANES:(c + 1) * LANES]
        ends = [scr[c, last_rows, :] for c in range(C // LANES)]
        return jnp.concatenate(ends, axis=1).reshape(S, G, C)

    a_sum = group_ends(ag, a_scr)
    b_sum = group_ends(bg, b_scr)
    grp = lax.broadcasted_iota(jnp.int32, a_sum.shape, 1)
    step = 1
    while step < G:
        b_sum = a_sum * _shift_rows(b_sum, step, 0.0, grp) + b_sum
        a_sum = a_sum * _shift_rows(a_sum, step, 1.0, grp)
        step *= 2
    h_end = b_sum + a_sum * h_in
    h_enter = _shift_rows(h_end, 1, h_in, grp)
    h = bg + ag * h_enter.reshape(S * G, 1, C)
    return h.reshape(S, L, C), h_end[:, G - 1:G, :]


def _front_kernel(x_ref, mod_ref, conv0_ref, h0_ref, g_pre_ref, w_in_ref, conv_w_ref, conv_b_ref,
                  w_gate_ref, b_gate_ref, lam_ref, g_lru_ref,
                  olru_ref, q_ref, kbf_ref, vbf_ref, k_ref, v_ref, conv_out_ref, h_out_ref,
                  ext_ref, hc_ref, a_scr, b_scr, *, n_seq, n_rows):
    S, L = n_seq, n_rows

    @pl.when(pl.program_id(0) == 0)
    def _():
        ext_ref[:, 0:CONV_CARRY_ROWS, :] = conv0_ref[...]
        hc_ref[...] = h0_ref[...]

    if S > 1:
        parts = [(slice(0, S // 2), 0, L), (slice(S // 2, S), 0, L)]
    else:
        parts = [(slice(0, S), 0, L // 2), (slice(0, S), L // 2, L // 2)]
    neg_lam = -lam_ref[...]
    decay = RG_C * (jnp.maximum(neg_lam, 0.0) + jnp.log1p(jnp.exp(-jnp.abs(neg_lam))))
    cw = conv_w_ref[...]

    def part_stages(index, seqs, row0, n):
        ns = seqs.stop - seqs.start
        rows = slice(row0, row0 + n)
        flat = lambda y: y.reshape(ns * n, y.shape[-1])
        unflat = lambda y: y.reshape(ns, n, y.shape[-1])

        shift = mod_ref[seqs, 0:1, :]
        scale = mod_ref[seqs, 1:2, :]
        h = _rms(x_ref[seqs, rows, :], g_pre_ref[...] * (1.0 + scale)) + shift
        hb = flat(h).astype(BF16)

        def project(first_col, n_cols):
            return _dot(hb, w_in_ref[:, first_col:first_col + n_cols])

        lru_in = project(0, 2 * D_LRU)
        xl = unflat(lru_in[:, 0:D_LRU])
        gl = unflat(lru_in[:, D_LRU:])
        ext_ref[seqs, CONV_CARRY_ROWS + row0:CONV_CARRY_ROWS + row0 + n, :] = xl
        yield

        k = project(2 * D_LRU + D_SB, D_SB)
        k_ref[seqs, rows, :] = unflat(k)
        kbf_ref[seqs, rows, :] = unflat(k.astype(BF16))
        xc = conv_b_ref[...]
        for j in range(CONV_WIDTH - 1):
            start = CONV_CARRY_ROWS + row0 - (CONV_WIDTH - 1 - j)
            xc = xc + ext_ref[seqs, start:start + n, :] * cw[j:j + 1, :]
        xc = xc + xl * cw[CONV_WIDTH - 1:CONV_WIDTH, :]
        gates = _dot(flat(xc).astype(BF16), w_gate_ref[...]) + b_gate_ref[...]
        yield

        v = project(2 * D_LRU + 2 * D_SB, D_SB)
        v_ref[seqs, rows, :] = unflat(v)
        vbf_ref[seqs, rows, :] = unflat(v.astype(BF16))
        r = unflat(jax.nn.sigmoid(gates[:, 0:D_LRU]))
        i = unflat(jax.nn.sigmoid(gates[:, D_LRU:]))
        neg_log_a = r * decay
        a = jnp.exp2(r * (decay * -LOG2_E))
        m = jnp.tanh(neg_log_a) * (1.0 + a * a)
        u = (m * lax.rsqrt(jnp.maximum(m, F32_TINY))) * (i * xc)
        yield

        q_ref[seqs, rows, :] = unflat(project(2 * D_LRU, D_SB).astype(BF16))
        scr_rows = slice(index * ns * n, (index + 1) * ns * n)
        hl, h_last = _lru_scan(a, u, hc_ref[seqs], a_scr.at[:, scr_rows, :], b_scr.at[:, scr_rows, :])
        hc_ref[seqs] = h_last
        h_out_ref[seqs] = h_last
        o = hl * jax.nn.gelu(gl)
        olru_ref[seqs, rows, :] = _rms(o, g_lru_ref[...]).astype(BF16)
        yield

    pipelines = [part_stages(index, *part) for index, part in enumerate(parts)]
    for _ in range(4):
        for pipeline in pipelines:
            next(pipeline)

    conv_out_ref[...] = ext_ref[:, L + CONV_CARRY_ROWS - (CONV_WIDTH - 1):, :]
    ext_ref[:, 0:CONV_CARRY_ROWS, :] = ext_ref[:, L:L + CONV_CARRY_ROWS, :]


def _front(x, mod, conv0, h0, g_pre, w_in_bf, conv_w, conv_b, w_gate_bf, b_gate, lam, g_lru,
           *, n_rows, name):
    S, T, _ = x.shape
    L = n_rows
    n_t = T // L
    const2 = lambda t: (0, 0)
    const3 = lambda t: (0, 0, 0)
    tile3 = lambda t: (0, t, 0)
    kernel = functools.partial(_front_kernel, n_seq=S, n_rows=L)
    return pl.pallas_call(
        kernel,
        grid=(n_t,),
        in_specs=[
            pl.BlockSpec((S, L, D_MODEL), tile3),
            pl.BlockSpec((S, N_MOD, D_MODEL), const3),
            pl.BlockSpec((S, CONV_CARRY_ROWS, D_LRU), const3),
            pl.BlockSpec((S, 1, D_LRU), const3),
            pl.BlockSpec((1, D_MODEL), const2),
            pl.BlockSpec((D_MODEL, D_IN), const2),
            pl.BlockSpec((CONV_WIDTH, D_LRU), const2),
            pl.BlockSpec((1, D_LRU), const2),
            pl.BlockSpec((D_LRU, 2 * D_LRU), const2),
            pl.BlockSpec((1, 2 * D_LRU), const2),
            pl.BlockSpec((1, D_LRU), const2),
            pl.BlockSpec((1, D_LRU), const2),
        ],
        out_specs=[
            pl.BlockSpec((S, L, D_LRU), tile3),
            pl.BlockSpec((S, L, D_SB), tile3),
            pl.BlockSpec((S, L, D_SB), tile3),
            pl.BlockSpec((S, L, D_SB), tile3),
            pl.BlockSpec((S, L, D_SB), tile3),
            pl.BlockSpec((S, L, D_SB), tile3),
            pl.BlockSpec((S, CONV_WIDTH - 1, D_LRU), const3),
            pl.BlockSpec((S, 1, D_LRU), const3),
        ],
        out_shape=[
            jax.ShapeDtypeStruct((S, T, D_LRU), BF16),
            jax.ShapeDtypeStruct((S, T, D_SB), BF16),
            jax.ShapeDtypeStruct((S, T, D_SB), BF16),
            jax.ShapeDtypeStruct((S, T, D_SB), BF16),
            jax.ShapeDtypeStruct((S, T, D_SB), F32),
            jax.ShapeDtypeStruct((S, T, D_SB), F32),
            jax.ShapeDtypeStruct((S, CONV_WIDTH - 1, D_LRU), F32),
            jax.ShapeDtypeStruct((S, 1, D_LRU), F32),
        ],
        scratch_shapes=[
            pltpu.VMEM((S, CONV_CARRY_ROWS + L, D_LRU), F32),
            pltpu.VMEM((S, 1, D_LRU), F32),
            pltpu.VMEM((D_LRU // LANES, S * L, LANES), F32),
            pltpu.VMEM((D_LRU // LANES, S * L, LANES), F32),
        ],
        compiler_params=pltpu.CompilerParams(
            dimension_semantics=("arbitrary",), vmem_limit_bytes=VMEM_LIMIT_BYTES),
        name=name,
    )(x, mod, conv0, h0, g_pre, w_in_bf, conv_w, conv_b, w_gate_bf, b_gate, lam, g_lru)


def _prefix_ones(n):
    j = lax.broadcasted_iota(jnp.int32, (n, n), 0)
    s = lax.broadcasted_iota(jnp.int32, (n, n), 1)
    return jnp.where(j > s, -1.0, 0.0).astype(BF16)


def _head_of_lane(shape):
    return lax.broadcasted_iota(jnp.int32, shape, 1) // SB_HEAD_DIM


def _stack_heads(x):
    head = _head_of_lane(x.shape)
    zero = jnp.zeros_like(x)
    return jnp.concatenate([jnp.where(head == h, x, zero) for h in range(HEADS_PER_GROUP)], axis=0)


def _group_cols(g):
    return slice(g * GROUP_LANES, (g + 1) * GROUP_LANES)


def _sweep_tiles(q_groups, tiles, state):
    accs, carries = state
    tq = accs[0].shape[0]
    rows = HEADS_PER_GROUP * tq
    groups = range(N_GROUPS)
    out_head = _head_of_lane((tq, GROUP_LANES))

    def scores(kt):
        return [lax.dot_general(q_groups[g], kt[:, _group_cols(g)], (((1,), (1,)), ((), ())),
                                preferred_element_type=F32) for g in groups]

    def drop_terms(z, mask):
        log_beta, drop, stacked = [], [], []
        for g in groups:
            sp = jnp.maximum(z[g], 0.0) + jnp.log(1.0 + jnp.exp2(jnp.abs(z[g]) * -LOG2_E))
            log_beta.append(z[g] - sp)
            if mask is not None:
                sp = jnp.where(mask, sp, 0.0)
            drop.append(sp)
            stacked.append(sp.astype(BF16))
        return log_beta, drop, jnp.concatenate(stacked, axis=0)

    def weigh(log_beta, log_keep_after, carries, mask, vt, accs):
        new_accs = []
        for g in groups:
            w = jnp.exp(log_beta[g] + log_keep_after[rows * g:rows * (g + 1)] + carries[g])
            if mask is not None:
                w = jnp.where(mask, w, 0.0)
            pv = _dot(w.astype(BF16), vt[:, _group_cols(g)])
            own = pv[0:tq]
            for h in range(1, HEADS_PER_GROUP):
                own = jnp.where(out_head == h, pv[h * tq:(h + 1) * tq], own)
            new_accs.append(accs[g] + own)
        return new_accs

    z = [scores(kt) for kt, _, _, _ in tiles]
    log_beta, log_keep_after, tile_carries = [], [], []
    for n, (_, _, neg_ones, mask) in enumerate(tiles):
        lb, drop, stacked = drop_terms(z[n], mask)
        log_beta.append(lb)
        log_keep_after.append(_dot(stacked, neg_ones))
        tile_carries.append(carries)
        carries = tuple(carries[g] - jnp.sum(drop[g], axis=-1, keepdims=True) for g in groups)
    for n, (_, vt, _, mask) in enumerate(tiles):
        accs = weigh(log_beta[n], log_keep_after[n], tile_carries[n], mask, vt, accs)
    return tuple(accs), carries


def _attn_kernel(q_ref, kd_ref, vd_ref, kp_ref, vp_ref, ksrc_ref, vsrc_ref, o_ref,
                 kt_ref, vt_ref, sem, *, tq, n_before):
    b = pl.program_id(0)
    i = pl.program_id(1)
    before = i * tq if n_before is None else n_before

    q = q_ref[...]
    q_groups = [_stack_heads(q[:, _group_cols(g)]) for g in range(N_GROUPS)]

    state = (tuple(jnp.zeros((tq, GROUP_LANES), F32) for _ in range(N_GROUPS)),
             tuple(jnp.zeros((HEADS_PER_GROUP * tq, 1), F32) for _ in range(N_GROUPS)))
    t_idx = lax.broadcasted_iota(jnp.int32, (tq, tq), 0)
    s_idx = lax.broadcasted_iota(jnp.int32, (tq, tq), 1)
    causal = jnp.concatenate([s_idx < t_idx] * HEADS_PER_GROUP, axis=0)
    own_tile = (kd_ref[...], vd_ref[...], _prefix_ones(tq), causal)

    def write(state):
        o_ref[...] = jnp.concatenate(state[0], axis=1)

    def own_and_earlier_keys(state):
        tri = _prefix_ones(KEY_TILE)
        prev_tile = (kp_ref[...].astype(BF16), vp_ref[...].astype(BF16), tri, None)
        state = _sweep_tiles(q_groups, [own_tile, prev_tile], state)
        n_tail = before // KEY_TILE - 1

        def live(carries):
            return jnp.max(functools.reduce(jnp.maximum, carries))

        def cond(c):
            n, top, _ = c
            return jnp.logical_and(n < n_tail, top > LOG_WEIGHT_FLOOR)

        def body(c):
            n, _, state = c
            rows = pl.ds(pl.multiple_of(before - (n + 2) * KEY_TILE, KEY_TILE), KEY_TILE)
            copies = (pltpu.make_async_copy(ksrc_ref.at[b, rows, :], kt_ref, sem.at[0]),
                      pltpu.make_async_copy(vsrc_ref.at[b, rows, :], vt_ref, sem.at[1]))
            for cp in copies:
                cp.start()
            for cp in copies:
                cp.wait()
            tile = (kt_ref[...].astype(BF16), vt_ref[...].astype(BF16), tri, None)
            state = _sweep_tiles(q_groups, [tile], state)
            return n + 1, live(state[1]), state

        return lax.while_loop(cond, body, (jnp.int32(0), live(state[1]), state))[2]

    if n_before is None:
        @pl.when(i == 0)
        def _():
            write(_sweep_tiles(q_groups, [own_tile], state))

        @pl.when(i > 0)
        def _():
            write(own_and_earlier_keys(state))
    else:
        write(own_and_earlier_keys(state))


def _attention(q, k_new, v_new, k_before, v_before, *, tq, name):
    B, T, _ = q.shape
    n_q = T // tq
    if k_before is None:
        assert tq == KEY_TILE
        k_before, v_before, n_before = k_new, v_new, None
        prev_idx = lambda b, i: (b, jnp.maximum(i - 1, 0), 0)
    else:
        assert n_q == 1
        n_before = k_before.shape[1]
        prev_idx = lambda b, i: (b, n_before // KEY_TILE - 1, 0)
    tile = lambda b, i: (b, i, 0)
    new_spec = pl.BlockSpec((None, tq, D_SB), tile)
    prev_spec = pl.BlockSpec((None, KEY_TILE, D_SB), prev_idx)
    hbm_spec = pl.BlockSpec(memory_space=pl.ANY)
    kernel = functools.partial(_attn_kernel, tq=tq, n_before=n_before)
    return pl.pallas_call(
        kernel,
        grid=(B, n_q),
        in_specs=[new_spec, new_spec, new_spec, prev_spec, prev_spec, hbm_spec, hbm_spec],
        out_specs=pl.BlockSpec((None, tq, D_SB), tile),
        out_shape=jax.ShapeDtypeStruct((B, T, D_SB), F32),
        scratch_shapes=[
            pltpu.VMEM((KEY_TILE, D_SB), k_before.dtype),
            pltpu.VMEM((KEY_TILE, D_SB), v_before.dtype),
            pltpu.SemaphoreType.DMA((2,)),
        ],
        compiler_params=pltpu.CompilerParams(
            dimension_semantics=("arbitrary", "arbitrary"), vmem_limit_bytes=VMEM_LIMIT_BYTES),
        name=name,
    )(q, k_new, v_new, k_before, v_before, k_before, v_before)


def _back_kernel(x_ref, olru_ref, osb_ref, mod_ref, g_sb_ref, g_post_mix_ref, g_pre_mlp_ref,
                 g_post_mlp_ref, w_out_ref, w_up_ref, w_down_ref, y_ref, *, n_seq, n_rows, ff_chunk):
    S, L = n_seq, n_rows
    if S > 1:
        halves = [(slice(0, S // 2), slice(None)), (slice(S // 2, S), slice(None))]
        hs, hl = S // 2, L
    else:
        halves = [(slice(None), slice(0, L // 2)), (slice(None), slice(L // 2, L))]
        hs, hl = S, L // 2
    n_chunks = D_FF // ff_chunk

    def mod(idx, row):
        return mod_ref[idx[0], row:row + 1, :]

    def attn_out(idx):
        o_lru = olru_ref[idx[0], idx[1], :].reshape(hs * hl, D_LRU)
        o_sb = _rms(osb_ref[idx[0], idx[1], :], g_sb_ref[...]).reshape(hs * hl, D_SB).astype(BF16)
        return _dot(o_lru, w_out_ref[0:D_LRU, :]) + _dot(o_sb, w_out_ref[D_LRU:, :])

    def mix_residual(idx, o):
        x = x_ref[idx[0], idx[1], :] + mod(idx, 2) * _rms(o, g_post_mix_ref[...]).reshape(hs, hl, D_MODEL)
        h = _rms(x, g_pre_mlp_ref[...] * (1.0 + mod(idx, 4))) + mod(idx, 3)
        return x, h.reshape(hs * hl, D_MODEL).astype(BF16)

    def mlp_chunk(hb, f, c):
        cols = slice(c * ff_chunk, (c + 1) * ff_chunk)
        up = jnp.maximum(_dot(hb, w_up_ref[:, cols]), 0.0)
        return f + _dot((up * up).astype(BF16), w_down_ref[cols, :])

    def mlp_residual(idx, x, f):
        y_ref[idx[0], idx[1], :] = x + mod(idx, 5) * _rms(f, g_post_mlp_ref[...]).reshape(hs, hl, D_MODEL)

    first, second = halves
    zero = jnp.zeros((hs * hl, D_MODEL), F32)
    o1 = attn_out(first)
    o2 = attn_out(second)
    x1, hb1 = mix_residual(first, o1)
    f1 = mlp_chunk(hb1, zero, 0)
    x2, hb2 = mix_residual(second, o2)
    for c in range(1, n_chunks):
        f1 = mlp_chunk(hb1, f1, c)
    f2 = mlp_chunk(hb2, zero, 0)
    mlp_residual(first, x1, f1)
    for c in range(1, n_chunks):
        f2 = mlp_chunk(hb2, f2, c)
    mlp_residual(second, x2, f2)


def _back(x, o_lru, o_sb, mod, g_sb, g_post_mix, g_pre_mlp, g_post_mlp, w_out_bf, w_up_bf, w_down_bf,
          *, n_seq, n_rows, name, ff_chunk=1024):
    B, T, _ = x.shape
    S, L = n_seq, n_rows
    const2 = lambda b, t: (0, 0)
    tile3 = lambda b, t: (b, t, 0)
    seq3 = lambda b, t: (b, 0, 0)
    kernel = functools.partial(_back_kernel, n_seq=S, n_rows=L, ff_chunk=ff_chunk)
    return pl.pallas_call(
        kernel,
        grid=(B // S, T // L),
        in_specs=[
            pl.BlockSpec((S, L, D_MODEL), tile3),
            pl.BlockSpec((S, L, D_LRU), tile3),
            pl.BlockSpec((S, L, D_SB), tile3),
            pl.BlockSpec((S, N_MOD, D_MODEL), seq3),
            pl.BlockSpec((1, D_SB), const2),
            pl.BlockSpec((1, D_MODEL), const2),
            pl.BlockSpec((1, D_MODEL), const2),
            pl.BlockSpec((1, D_MODEL), const2),
            pl.BlockSpec((D_MODEL, D_MODEL), const2),
            pl.BlockSpec((D_MODEL, D_FF), const2),
            pl.BlockSpec((D_FF, D_MODEL), const2),
        ],
        out_specs=pl.BlockSpec((S, L, D_MODEL), tile3),
        out_shape=jax.ShapeDtypeStruct((B, T, D_MODEL), F32),
        compiler_params=pltpu.CompilerParams(
            dimension_semantics=("arbitrary", "arbitrary"), vmem_limit_bytes=VMEM_LIMIT_BYTES),
        name=name,
    )(x, o_lru, o_sb, mod, g_sb, g_post_mix, g_pre_mlp, g_post_mlp, w_out_bf, w_up_bf, w_down_bf)


def _block_diag(w):
    n, bi, bj = w.shape
    eye = jnp.eye(n, dtype=w.dtype)
    return (eye[:, None, :, None] * w[:, :, None, :]).reshape(n * bi, n * bj)


def _pad_conv_state(buf):
    return jnp.pad(buf, ((0, 0), (CONV_CARRY_ROWS - (CONV_WIDTH - 1), 0), (0, 0)))


def kernel(x_prompt, x_sample, c_prompt, c_sample, cache_conv, state_lru, cache_k, cache_v, w_ada, b_ada, g_pre_mix, g_post_mix, g_pre_mlp, g_post_mlp, w_in, conv_w, conv_b, w_rg_a, b_rg_a, w_rg_x, b_rg_x, lru_lambda, g_lru_out, g_sb_out, w_out, w_up, w_down):
    depth = w_ada.shape[0]
    bp, seq, _ = x_prompt.shape
    bs, dec_seq, _ = x_sample.shape
    assert depth == 1 and bp == 1, "single-layer, single-prompt configuration"
    past_len = cache_k.shape[2]
    l = 0

    row = lambda a: a.reshape(1, -1)
    q_cols = jnp.arange(D_IN) // D_SB == 2 * D_LRU // D_SB
    w_in_bf = (w_in[l] * jnp.where(q_cols, SB_HEAD_DIM ** -0.5, 1.0)).astype(BF16)
    w_gate_bf = jnp.concatenate([_block_diag(w_rg_a[l]), _block_diag(w_rg_x[l])], axis=1).astype(BF16)
    b_gate = jnp.concatenate([b_rg_a[l].reshape(-1), b_rg_x[l].reshape(-1)]).reshape(1, -1)
    w_out_bf = w_out[l].astype(BF16)
    w_up_bf = w_up[l].astype(BF16)
    w_down_bf = w_down[l].astype(BF16)
    front_w = (row(g_pre_mix[l]), w_in_bf, conv_w[l], row(conv_b[l]), w_gate_bf, b_gate,
               row(lru_lambda[l]), row(g_lru_out[l]))
    back_w = (row(g_sb_out[l]), row(g_post_mix[l]), row(g_pre_mlp[l]), row(g_post_mlp[l]),
              w_out_bf, w_up_bf, w_down_bf)

    mod = _adaln(jnp.concatenate([c_prompt, c_sample], axis=0), w_ada[l], b_ada[l])
    mod = mod.reshape(bp + bs, N_MOD, D_MODEL)
    mod_p, mod_s = mod[:bp], mod[bp:]

    zeros_conv = jnp.zeros((bp, CONV_CARRY_ROWS, D_LRU), F32)
    zeros_h = jnp.zeros((bp, 1, D_LRU), F32)
    olru_p, q_p, kbf_p, vbf_p, k_p, v_p, conv_p, h_p = _front(
        x_prompt, mod_p, zeros_conv, zeros_h, *front_w, n_rows=512, name="front_prompt")
    osb_p = _attention(q_p, kbf_p, vbf_p, None, None, tq=KEY_TILE, name="attn_prompt")
    y_p = _back(x_prompt, olru_p, osb_p, mod_p, *back_w, n_seq=1, n_rows=512, name="back_prompt")

    olru_s, q_s, kbf_s, vbf_s, k_s, v_s, conv_s, h_s = _front(
        x_sample, mod_s, _pad_conv_state(cache_conv[l]), state_lru[l].reshape(bs, 1, D_LRU),
        *front_w, n_rows=dec_seq, name="front_sample")
    osb_s = _attention(q_s, kbf_s, vbf_s, cache_k[l].reshape(bs, past_len, D_SB),
                       cache_v[l].reshape(bs, past_len, D_SB), tq=dec_seq, name="attn_sample")
    y_s = _back(x_sample, olru_s, osb_s, mod_s, *back_w, n_seq=8, n_rows=dec_seq, name="back_sample")

    heads = lambda a: a.reshape(1, a.shape[0], a.shape[1], N_SB_HEADS, SB_HEAD_DIM)
    return (y_p, y_s, conv_p[None], h_p.reshape(1, bp, D_LRU), heads(k_p), heads(v_p),
            conv_s[None], h_s.reshape(1, bs, D_LRU), heads(k_s), heads(v_s))
```

```python
import functools

import jax
import jax.numpy as jnp
from jax import lax
from jax.experimental import pallas as pl
from jax.experimental.pallas import tpu as pltpu

D_MODEL = 1024
D_LRU = 512
D_SB = 512
N_SB_HEADS = 8
SB_HEAD_DIM = 64
HEAD_PAIR = 2 * SB_HEAD_DIM
N_PAIRS = N_SB_HEADS // 2
CONV_WIDTH = 4
CONV_CARRY_ROWS = 8
RG_C = 8.0
D_FF = 4 * D_MODEL
D_IN = 2 * D_LRU + 3 * D_SB
N_MOD = 6
EPS = 1e-6
LOG2_E = 1.4426950408889634
F32_TINY = 1.1754943508222875e-38

KEY_TILE = 256
LOG_WEIGHT_FLOOR = -110.0

VMEM_LIMIT_BYTES = 56 * 1024 * 1024

BF16 = jnp.bfloat16
F32 = jnp.float32


def _rms(x, g):
    return x * lax.rsqrt(jnp.mean(x * x, axis=-1, keepdims=True) + EPS) * g


def _dot(a, b):
    return jnp.dot(a, b, preferred_element_type=F32)


def _adaln_kernel(c_ref, w_ref, b_ref, o_ref):
    c = c_ref[...]
    s = (c * jax.nn.sigmoid(c)).astype(BF16)
    o_ref[...] = _dot(s, w_ref[...].astype(BF16)) + b_ref[...]


def _adaln(c_all, w_ada, b_ada, *, tn=1536):
    n_rows = c_all.shape[0]
    n_out = w_ada.shape[1]
    return pl.pallas_call(
        _adaln_kernel,
        grid=(n_out // tn,),
        in_specs=[
            pl.BlockSpec((n_rows, D_MODEL), lambda j: (0, 0)),
            pl.BlockSpec((D_MODEL, tn), lambda j: (0, j)),
            pl.BlockSpec((1, tn), lambda j: (0, j)),
        ],
        out_specs=pl.BlockSpec((n_rows, tn), lambda j: (0, j)),
        out_shape=jax.ShapeDtypeStruct((n_rows, n_out), F32),
        compiler_params=pltpu.CompilerParams(
            dimension_semantics=("arbitrary",), vmem_limit_bytes=VMEM_LIMIT_BYTES),
        name="adaln",
    )(c_all, w_ada, b_ada.reshape(1, n_out))


def _shift_rows(x, s, fill, row):
    return jnp.where(row >= s, pltpu.roll(x, s, axis=1), fill)


SUBLANES = 8
LANES = 128


def _lru_scan(a, u, h_in, a_scr, b_scr):
    S, L, C = a.shape
    G = L // SUBLANES
    ag = a.reshape(S * G, SUBLANES, C)
    bg = u.reshape(S * G, SUBLANES, C)
    sub = lax.broadcasted_iota(jnp.int32, ag.shape, 1)
    step = 1
    while step < SUBLANES:
        bg = ag * _shift_rows(bg, step, 0.0, sub) + bg
        ag = ag * _shift_rows(ag, step, 1.0, sub)
        step *= 2
    last_rows = pl.ds(SUBLANES - 1, S * G, stride=SUBLANES)

    def group_ends(x, scr):
        x = x.reshape(S * L, C)
        for c in range(C // LANES):
            scr[c] = x[:, c * LANES:(c + 1) * LANES]
        ends = [scr[c, last_rows, :] for c in range(C // LANES)]
        return jnp.concatenate(ends, axis=1).reshape(S, G, C)

    a_sum = group_ends(ag, a_scr)
    b_sum = group_ends(bg, b_scr)
    grp = lax.broadcasted_iota(jnp.int32, a_sum.shape, 1)
    step = 1
    while step < G:
        b_sum = a_sum * _shift_rows(b_sum, step, 0.0, grp) + b_sum
        a_sum = a_sum * _shift_rows(a_sum, step, 1.0, grp)
        step *= 2
    h_end = b_sum + a_sum * h_in
    h_enter = _shift_rows(h_end, 1, h_in, grp)
    h = bg + ag * h_enter.reshape(S * G, 1, C)
    return h.reshape(S, L, C), h_end[:, G - 1:G, :]


def _front_kernel(x_ref, mod_ref, conv0_ref, h0_ref, g_pre_ref, w_in_ref, conv_w_ref, conv_b_ref,
                  w_gate_ref, b_gate_ref, lam_ref, g_lru_ref,
                  olru_ref, q_ref, kbf_ref, vbf_ref, k_ref, v_ref, conv_out_ref, h_out_ref,
                  ext_ref, hc_ref, a_scr, b_scr, *, n_seq, n_rows):
    S, L = n_seq, n_rows

    @pl.when(pl.program_id(0) == 0)
    def _():
        ext_ref[:, 0:CONV_CARRY_ROWS, :] = conv0_ref[...]
        hc_ref[...] = h0_ref[...]

    x = x_ref[...]
    shift = mod_ref[:, 0:1, :]
    scale = mod_ref[:, 1:2, :]
    h = _rms(x, g_pre_ref[...] * (1.0 + scale)) + shift
    hb = h.reshape(S * L, D_MODEL).astype(BF16)

    def project(first_col, n_cols):
        return _dot(hb, w_in_ref[:, first_col:first_col + n_cols])

    lru_in = project(0, 2 * D_LRU)
    xl = lru_in[:, 0:D_LRU].reshape(S, L, D_LRU)
    gl = lru_in[:, D_LRU:].reshape(S, L, D_LRU)

    k = project(2 * D_LRU + D_SB, D_SB)
    k_ref[...] = k.reshape(S, L, D_SB)
    kbf_ref[...] = k.astype(BF16).reshape(S, L, D_SB)

    ext_ref[:, CONV_CARRY_ROWS:, :] = xl
    cw = conv_w_ref[...]
    xc = conv_b_ref[...]
    for j in range(CONV_WIDTH - 1):
        start = CONV_CARRY_ROWS - (CONV_WIDTH - 1 - j)
        xc = xc + ext_ref[:, start:start + L, :] * cw[j:j + 1, :]
    xc = xc + xl * cw[CONV_WIDTH - 1:CONV_WIDTH, :]
    conv_out_ref[...] = ext_ref[:, L + CONV_CARRY_ROWS - (CONV_WIDTH - 1):, :]
    ext_ref[:, 0:CONV_CARRY_ROWS, :] = xl[:, L - CONV_CARRY_ROWS:, :]

    gates = _dot(xc.reshape(S * L, D_LRU).astype(BF16), w_gate_ref[...]) + b_gate_ref[...]
    v = project(2 * D_LRU + 2 * D_SB, D_SB)
    v_ref[...] = v.reshape(S, L, D_SB)
    vbf_ref[...] = v.astype(BF16).reshape(S, L, D_SB)

    r = jax.nn.sigmoid(gates[:, 0:D_LRU]).reshape(S, L, D_LRU)
    i = jax.nn.sigmoid(gates[:, D_LRU:]).reshape(S, L, D_LRU)
    neg_lam = -lam_ref[...]
    decay = RG_C * (jnp.maximum(neg_lam, 0.0) + jnp.log1p(jnp.exp(-jnp.abs(neg_lam))))
    neg_log_a = r * decay
    a = jnp.exp2(r * (decay * -LOG2_E))
    m = jnp.tanh(neg_log_a) * (1.0 + a * a)
    u = (m * lax.rsqrt(jnp.maximum(m, F32_TINY))) * (i * xc)

    q_ref[...] = project(2 * D_LRU, D_SB).astype(BF16).reshape(S, L, D_SB)

    hl, h_last = _lru_scan(a, u, hc_ref[...], a_scr, b_scr)
    hc_ref[...] = h_last
    h_out_ref[...] = h_last

    o = hl * jax.nn.gelu(gl)
    olru_ref[...] = _rms(o, g_lru_ref[...]).astype(BF16)


def _front(x, mod, conv0, h0, g_pre, w_in_bf, conv_w, conv_b, w_gate_bf, b_gate, lam, g_lru,
           *, n_rows, name):
    S, T, _ = x.shape
    L = n_rows
    n_t = T // L
    const2 = lambda t: (0, 0)
    const3 = lambda t: (0, 0, 0)
    tile3 = lambda t: (0, t, 0)
    kernel = functools.partial(_front_kernel, n_seq=S, n_rows=L)
    return pl.pallas_call(
        kernel,
        grid=(n_t,),
        in_specs=[
            pl.BlockSpec((S, L, D_MODEL), tile3),
            pl.BlockSpec((S, N_MOD, D_MODEL), const3),
            pl.BlockSpec((S, CONV_CARRY_ROWS, D_LRU), const3),
            pl.BlockSpec((S, 1, D_LRU), const3),
            pl.BlockSpec((1, D_MODEL), const2),
            pl.BlockSpec((D_MODEL, D_IN), const2),
            pl.BlockSpec((CONV_WIDTH, D_LRU), const2),
            pl.BlockSpec((1, D_LRU), const2),
            pl.BlockSpec((D_LRU, 2 * D_LRU), const2),
            pl.BlockSpec((1, 2 * D_LRU), const2),
            pl.BlockSpec((1, D_LRU), const2),
            pl.BlockSpec((1, D_LRU), const2),
        ],
        out_specs=[
            pl.BlockSpec((S, L, D_LRU), tile3),
            pl.BlockSpec((S, L, D_SB), tile3),
            pl.BlockSpec((S, L, D_SB), tile3),
            pl.BlockSpec((S, L, D_SB), tile3),
            pl.BlockSpec((S, L, D_SB), tile3),
            pl.BlockSpec((S, L, D_SB), tile3),
            pl.BlockSpec((S, CONV_WIDTH - 1, D_LRU), const3),
            pl.BlockSpec((S, 1, D_LRU), const3),
        ],
        out_shape=[
            jax.ShapeDtypeStruct((S, T, D_LRU), BF16),
            jax.ShapeDtypeStruct((S, T, D_SB), BF16),
            jax.ShapeDtypeStruct((S, T, D_SB), BF16),
            jax.ShapeDtypeStruct((S, T, D_SB), BF16),
            jax.ShapeDtypeStruct((S, T, D_SB), F32),
            jax.ShapeDtypeStruct((S, T, D_SB), F32),
            jax.ShapeDtypeStruct((S, CONV_WIDTH - 1, D_LRU), F32),
            jax.ShapeDtypeStruct((S, 1, D_LRU), F32),
        ],
        scratch_shapes=[
            pltpu.VMEM((S, CONV_CARRY_ROWS + L, D_LRU), F32),
            pltpu.VMEM((S, 1, D_LRU), F32),
            pltpu.VMEM((D_LRU // LANES, S * L, LANES), F32),
            pltpu.VMEM((D_LRU // LANES, S * L, LANES), F32),
        ],
        compiler_params=pltpu.CompilerParams(
            dimension_semantics=("arbitrary",), vmem_limit_bytes=VMEM_LIMIT_BYTES),
        name=name,
    )(x, mod, conv0, h0, g_pre, w_in_bf, conv_w, conv_b, w_gate_bf, b_gate, lam, g_lru)


def _prefix_ones(n):
    j = lax.broadcasted_iota(jnp.int32, (n, n), 0)
    s = lax.broadcasted_iota(jnp.int32, (n, n), 1)
    return jnp.where(j > s, -1.0, 0.0).astype(BF16)


def _stack_heads(x):
    lane = lax.broadcasted_iota(jnp.int32, x.shape, 1)
    zero = jnp.zeros_like(x)
    return jnp.concatenate([jnp.where(lane < SB_HEAD_DIM, x, zero),
                            jnp.where(lane >= SB_HEAD_DIM, x, zero)], axis=0)


def _pair_cols(p):
    return slice(p * HEAD_PAIR, (p + 1) * HEAD_PAIR)


def _sweep_tiles(q_pairs, tiles, state):
    accs, carries = state
    tq = accs[0].shape[0]
    pairs = range(N_PAIRS)
    lane = lax.broadcasted_iota(jnp.int32, (tq, HEAD_PAIR), 1)

    def scores(kt):
        return [lax.dot_general(q_pairs[p], kt[:, _pair_cols(p)], (((1,), (1,)), ((), ())),
                                preferred_element_type=F32) for p in pairs]

    def drop_terms(z, mask):
        log_beta, drop, stacked = [], [], []
        for p in pairs:
            sp = jnp.maximum(z[p], 0.0) + jnp.log(1.0 + jnp.exp2(jnp.abs(z[p]) * -LOG2_E))
            log_beta.append(z[p] - sp)
            if mask is not None:
                sp = jnp.where(mask, sp, 0.0)
            drop.append(sp)
            stacked.append(sp.astype(BF16))
        return log_beta, drop, jnp.concatenate(stacked, axis=0)

    def weigh(log_beta, log_keep_after, carries, mask, vt, accs):
        new_accs = []
        for p in pairs:
            w = jnp.exp(log_beta[p] + log_keep_after[2 * tq * p:2 * tq * (p + 1)] + carries[p])
            if mask is not None:
                w = jnp.where(mask, w, 0.0)
            pv = _dot(w.astype(BF16), vt[:, _pair_cols(p)])
            new_accs.append(accs[p] + jnp.where(lane < SB_HEAD_DIM, pv[:tq], pv[tq:]))
        return new_accs

    z = [scores(kt) for kt, _, _, _ in tiles]
    log_beta, log_keep_after, tile_carries = [], [], []
    for n, (_, _, neg_ones, mask) in enumerate(tiles):
        lb, drop, stacked = drop_terms(z[n], mask)
        log_beta.append(lb)
        log_keep_after.append(_dot(stacked, neg_ones))
        tile_carries.append(carries)
        carries = tuple(carries[p] - jnp.sum(drop[p], axis=-1, keepdims=True) for p in pairs)
    for n, (_, vt, _, mask) in enumerate(tiles):
        accs = weigh(log_beta[n], log_keep_after[n], tile_carries[n], mask, vt, accs)
    return tuple(accs), carries


def _attn_kernel(q_ref, kd_ref, vd_ref, kp_ref, vp_ref, ksrc_ref, vsrc_ref, o_ref,
                 kt_ref, vt_ref, sem, *, tq, n_before):
    b = pl.program_id(0)
    i = pl.program_id(1)
    before = i * tq if n_before is None else n_before

    q = q_ref[...]
    q_pairs = [_stack_heads(q[:, p * HEAD_PAIR:(p + 1) * HEAD_PAIR]) for p in range(N_PAIRS)]

    state = (tuple(jnp.zeros((tq, HEAD_PAIR), F32) for _ in range(N_PAIRS)),
             tuple(jnp.zeros((2 * tq, 1), F32) for _ in range(N_PAIRS)))
    t_idx = lax.broadcasted_iota(jnp.int32, (tq, tq), 0)
    s_idx = lax.broadcasted_iota(jnp.int32, (tq, tq), 1)
    causal = s_idx < t_idx
    own_tile = (kd_ref[...], vd_ref[...], _prefix_ones(tq), jnp.concatenate([causal, causal], axis=0))

    def write(state):
        o_ref[...] = jnp.concatenate(state[0], axis=1)

    def own_and_earlier_keys(state):
        tri = _prefix_ones(KEY_TILE)
        prev_tile = (kp_ref[...].astype(BF16), vp_ref[...].astype(BF16), tri, None)
        state = _sweep_tiles(q_pairs, [own_tile, prev_tile], state)
        n_tail = before // KEY_TILE - 1

        def live(carries):
            return jnp.max(functools.reduce(jnp.maximum, carries))

        def cond(c):
            n, top, _ = c
            return jnp.logical_and(n < n_tail, top > LOG_WEIGHT_FLOOR)

        def body(c):
            n, _, state = c
            rows = pl.ds(pl.multiple_of(before - (n + 2) * KEY_TILE, KEY_TILE), KEY_TILE)
            copies = (pltpu.make_async_copy(ksrc_ref.at[b, rows, :], kt_ref, sem.at[0]),
                      pltpu.make_async_copy(vsrc_ref.at[b, rows, :], vt_ref, sem.at[1]))
            for cp in copies:
                cp.start()
            for cp in copies:
                cp.wait()
            tile = (kt_ref[...].astype(BF16), vt_ref[...].astype(BF16), tri, None)
            state = _sweep_tiles(q_pairs, [tile], state)
            return n + 1, live(state[1]), state

        return lax.while_loop(cond, body, (jnp.int32(0), live(state[1]), state))[2]

    if n_before is None:
        @pl.when(i == 0)
        def _():
            write(_sweep_tiles(q_pairs, [own_tile], state))

        @pl.when(i > 0)
        def _():
            write(own_and_earlier_keys(state))
    else:
        write(own_and_earlier_keys(state))


def _attention(q, k_new, v_new, k_before, v_before, *, tq, name):
    B, T, _ = q.shape
    n_q = T // tq
    if k_before is None:
        assert tq == KEY_TILE
        k_before, v_before, n_before = k_new, v_new, None
        prev_idx = lambda b, i: (b, jnp.maximum(i - 1, 0), 0)
    else:
        assert n_q == 1
        n_before = k_before.shape[1]
        prev_idx = lambda b, i: (b, n_before // KEY_TILE - 1, 0)
    tile = lambda b, i: (b, i, 0)
    new_spec = pl.BlockSpec((None, tq, D_SB), tile)
    prev_spec = pl.BlockSpec((None, KEY_TILE, D_SB), prev_idx)
    hbm_spec = pl.BlockSpec(memory_space=pl.ANY)
    kernel = functools.partial(_attn_kernel, tq=tq, n_before=n_before)
    return pl.pallas_call(
        kernel,
        grid=(B, n_q),
        in_specs=[new_spec, new_spec, new_spec, prev_spec, prev_spec, hbm_spec, hbm_spec],
        out_specs=pl.BlockSpec((None, tq, D_SB), tile),
        out_shape=jax.ShapeDtypeStruct((B, T, D_SB), F32),
        scratch_shapes=[
            pltpu.VMEM((KEY_TILE, D_SB), k_before.dtype),
            pltpu.VMEM((KEY_TILE, D_SB), v_before.dtype),
            pltpu.SemaphoreType.DMA((2,)),
        ],
        compiler_params=pltpu.CompilerParams(
            dimension_semantics=("arbitrary", "arbitrary"), vmem_limit_bytes=VMEM_LIMIT_BYTES),
        name=name,
    )(q, k_new, v_new, k_before, v_before, k_before, v_before)


def _back_kernel(x_ref, olru_ref, osb_ref, mod_ref, g_sb_ref, g_post_mix_ref, g_pre_mlp_ref,
                 g_post_mlp_ref, w_out_ref, w_up_ref, w_down_ref, y_ref, *, n_seq, n_rows, ff_chunk):
    S, L = n_seq, n_rows
    if S > 1:
        halves = [(slice(0, S // 2), slice(None)), (slice(S // 2, S), slice(None))]
        hs, hl = S // 2, L
    else:
        halves = [(slice(None), slice(0, L // 2)), (slice(None), slice(L // 2, L))]
        hs, hl = S, L // 2
    n_chunks = D_FF // ff_chunk

    def mod(idx, row):
        return mod_ref[idx[0], row:row + 1, :]

    def attn_out(idx):
        o_lru = olru_ref[idx[0], idx[1], :].reshape(hs * hl, D_LRU)
        o_sb = _rms(osb_ref[idx[0], idx[1], :], g_sb_ref[...]).reshape(hs * hl, D_SB).astype(BF16)
        return _dot(o_lru, w_out_ref[0:D_LRU, :]) + _dot(o_sb, w_out_ref[D_LRU:, :])

    def mix_residual(idx, o):
        x = x_ref[idx[0], idx[1], :] + mod(idx, 2) * _rms(o, g_post_mix_ref[...]).reshape(hs, hl, D_MODEL)
        h = _rms(x, g_pre_mlp_ref[...] * (1.0 + mod(idx, 4))) + mod(idx, 3)
        return x, h.reshape(hs * hl, D_MODEL).astype(BF16)

    def mlp_chunk(hb, f, c):
        cols = slice(c * ff_chunk, (c + 1) * ff_chunk)
        up = jnp.maximum(_dot(hb, w_up_ref[:, cols]), 0.0)
        return f + _dot((up * up).astype(BF16), w_down_ref[cols, :])

    def mlp_residual(idx, x, f):
        y_ref[idx[0], idx[1], :] = x + mod(idx, 5) * _rms(f, g_post_mlp_ref[...]).reshape(hs, hl, D_MODEL)

    first, second = halves
    zero = jnp.zeros((hs * hl, D_MODEL), F32)
    o1 = attn_out(first)
    o2 = attn_out(second)
    x1, hb1 = mix_residual(first, o1)
    f1 = mlp_chunk(hb1, zero, 0)
    x2, hb2 = mix_residual(second, o2)
    for c in range(1, n_chunks):
        f1 = mlp_chunk(hb1, f1, c)
    f2 = mlp_chunk(hb2, zero, 0)
    mlp_residual(first, x1, f1)
    for c in range(1, n_chunks):
        f2 = mlp_chunk(hb2, f2, c)
    mlp_residual(second, x2, f2)


def _back(x, o_lru, o_sb, mod, g_sb, g_post_mix, g_pre_mlp, g_post_mlp, w_out_bf, w_up_bf, w_down_bf,
          *, n_seq, n_rows, name, ff_chunk=1024):
    B, T, _ = x.shape
    S, L = n_seq, n_rows
    const2 = lambda b, t: (0, 0)
    tile3 = lambda b, t: (b, t, 0)
    seq3 = lambda b, t: (b, 0, 0)
    kernel = functools.partial(_back_kernel, n_seq=S, n_rows=L, ff_chunk=ff_chunk)
    return pl.pallas_call(
        kernel,
        grid=(B // S, T // L),
        in_specs=[
            pl.BlockSpec((S, L, D_MODEL), tile3),
            pl.BlockSpec((S, L, D_LRU), tile3),
            pl.BlockSpec((S, L, D_SB), tile3),
            pl.BlockSpec((S, N_MOD, D_MODEL), seq3),
            pl.BlockSpec((1, D_SB), const2),
            pl.BlockSpec((1, D_MODEL), const2),
            pl.BlockSpec((1, D_MODEL), const2),
            pl.BlockSpec((1, D_MODEL), const2),
            pl.BlockSpec((D_MODEL, D_MODEL), const2),
            pl.BlockSpec((D_MODEL, D_FF), const2),
            pl.BlockSpec((D_FF, D_MODEL), const2),
        ],
        out_specs=pl.BlockSpec((S, L, D_MODEL), tile3),
        out_shape=jax.ShapeDtypeStruct((B, T, D_MODEL), F32),
        compiler_params=pltpu.CompilerParams(
            dimension_semantics=("arbitrary", "arbitrary"), vmem_limit_bytes=VMEM_LIMIT_BYTES),
        name=name,
    )(x, o_lru, o_sb, mod, g_sb, g_post_mix, g_pre_mlp, g_post_mlp, w_out_bf, w_up_bf, w_down_bf)


FRONT_ROWS = 1024
BACK_ROWS = 1024


def _block_diag(w):
    n, bi, bj = w.shape
    eye = jnp.eye(n, dtype=w.dtype)
    return (eye[:, None, :, None] * w[:, :, None, :]).reshape(n * bi, n * bj)


def _pad_conv_state(buf):
    return jnp.pad(buf, ((0, 0), (CONV_CARRY_ROWS - (CONV_WIDTH - 1), 0), (0, 0)))


def kernel(x_prompt, x_sample, c_prompt, c_sample, cache_conv, state_lru, cache_k, cache_v, w_ada, b_ada, g_pre_mix, g_post_mix, g_pre_mlp, g_post_mlp, w_in, conv_w, conv_b, w_rg_a, b_rg_a, w_rg_x, b_rg_x, lru_lambda, g_lru_out, g_sb_out, w_out, w_up, w_down):
    depth = w_ada.shape[0]
    bp, seq, _ = x_prompt.shape
    bs, dec_seq, _ = x_sample.shape
    assert depth == 1 and bp == 1, "single-layer, single-prompt configuration"
    past_len = cache_k.shape[2]
    l = 0

    row = lambda a: a.reshape(1, -1)
    q_cols = jnp.arange(D_IN) // D_SB == 2 * D_LRU // D_SB
    w_in_bf = (w_in[l] * jnp.where(q_cols, SB_HEAD_DIM ** -0.5, 1.0)).astype(BF16)
    w_gate_bf = jnp.concatenate([_block_diag(w_rg_a[l]), _block_diag(w_rg_x[l])], axis=1).astype(BF16)
    b_gate = jnp.concatenate([b_rg_a[l].reshape(-1), b_rg_x[l].reshape(-1)]).reshape(1, -1)
    w_out_bf = w_out[l].astype(BF16)
    w_up_bf = w_up[l].astype(BF16)
    w_down_bf = w_down[l].astype(BF16)
    front_w = (row(g_pre_mix[l]), w_in_bf, conv_w[l], row(conv_b[l]), w_gate_bf, b_gate,
               row(lru_lambda[l]), row(g_lru_out[l]))
    back_w = (row(g_sb_out[l]), row(g_post_mix[l]), row(g_pre_mlp[l]), row(g_post_mlp[l]),
              w_out_bf, w_up_bf, w_down_bf)

    mod = _adaln(jnp.concatenate([c_prompt, c_sample], axis=0), w_ada[l], b_ada[l])
    mod = mod.reshape(bp + bs, N_MOD, D_MODEL)
    mod_p, mod_s = mod[:bp], mod[bp:]

    zeros_conv = jnp.zeros((bp, CONV_CARRY_ROWS, D_LRU), F32)
    zeros_h = jnp.zeros((bp, 1, D_LRU), F32)
    olru_p, q_p, kbf_p, vbf_p, k_p, v_p, conv_p, h_p = _front(
        x_prompt, mod_p, zeros_conv, zeros_h, *front_w, n_rows=FRONT_ROWS, name="front_prompt")
    osb_p = _attention(q_p, kbf_p, vbf_p, None, None, tq=KEY_TILE, name="attn_prompt")
    y_p = _back(x_prompt, olru_p, osb_p, mod_p, *back_w, n_seq=1, n_rows=BACK_ROWS, name="back_prompt")

    olru_s, q_s, kbf_s, vbf_s, k_s, v_s, conv_s, h_s = _front(
        x_sample, mod_s, _pad_conv_state(cache_conv[l]), state_lru[l].reshape(bs, 1, D_LRU),
        *front_w, n_rows=dec_seq, name="front_sample")
    osb_s = _attention(q_s, kbf_s, vbf_s, cache_k[l].reshape(bs, past_len, D_SB),
                       cache_v[l].reshape(bs, past_len, D_SB), tq=dec_seq, name="attn_sample")
    y_s = _back(x_sample, olru_s, osb_s, mod_s, *back_w, n_seq=8, n_rows=dec_seq, name="back_sample")

    heads = lambda a: a.reshape(1, a.shape[0], a.shape[1], N_SB_HEADS, SB_HEAD_DIM)
    return (y_p, y_s, conv_p[None], h_p.reshape(1, bp, D_LRU), heads(k_p), heads(v_p),
            conv_s[None], h_s.reshape(1, bs, D_LRU), heads(k_s), heads(v_s))
```

```python
import functools

import jax
import jax.numpy as jnp
from jax import lax
from jax.experimental import pallas as pl
from jax.experimental.pallas import tpu as pltpu

D_MODEL = 1024
D_LRU = 512
D_SB = 512
N_SB_HEADS = 8
SB_HEAD_DIM = 64
HEAD_PAIR = 2 * SB_HEAD_DIM
N_PAIRS = N_SB_HEADS // 2
CONV_WIDTH = 4
CONV_CARRY_ROWS = 8
RG_C = 8.0
D_FF = 4 * D_MODEL
D_IN = 2 * D_LRU + 3 * D_SB
N_MOD = 6
EPS = 1e-6
LOG2_E = 1.4426950408889634
F32_TINY = 1.1754943508222875e-38

KEY_TILE = 256
LOG_WEIGHT_FLOOR = -110.0

VMEM_LIMIT_BYTES = 56 * 1024 * 1024

BF16 = jnp.bfloat16
F32 = jnp.float32


def _rms(x, g):
    return x * lax.rsqrt(jnp.mean(x * x, axis=-1, keepdims=True) + EPS) * g


def _dot(a, b):
    return jnp.dot(a, b, preferred_element_type=F32)


def _adaln_kernel(c_ref, w_ref, b_ref, o_ref):
    c = c_ref[...]
    s = (c * jax.nn.sigmoid(c)).astype(BF16)
    o_ref[...] = _dot(s, w_ref[...].astype(BF16)) + b_ref[...]


def _adaln(c_all, w_ada, b_ada, *, tn=1536):
    n_rows = c_all.shape[0]
    n_out = w_ada.shape[1]
    return pl.pallas_call(
        _adaln_kernel,
        grid=(n_out // tn,),
        in_specs=[
            pl.BlockSpec((n_rows, D_MODEL), lambda j: (0, 0)),
            pl.BlockSpec((D_MODEL, tn), lambda j: (0, j)),
            pl.BlockSpec((1, tn), lambda j: (0, j)),
        ],
        out_specs=pl.BlockSpec((n_rows, tn), lambda j: (0, j)),
        out_shape=jax.ShapeDtypeStruct((n_rows, n_out), F32),
        compiler_params=pltpu.CompilerParams(
            dimension_semantics=("arbitrary",), vmem_limit_bytes=VMEM_LIMIT_BYTES),
        name="adaln",
    )(c_all, w_ada, b_ada.reshape(1, n_out))


def _shift_rows(x, s, fill, row):
    return jnp.where(row >= s, pltpu.roll(x, s, axis=1), fill)


SUBLANES = 8
LANES = 128


def _lru_scan(a, u, h_in, a_scr, b_scr):
    S, L, C = a.shape
    G = L // SUBLANES
    ag = a.reshape(S * G, SUBLANES, C)
    bg = u.reshape(S * G, SUBLANES, C)
    sub = lax.broadcasted_iota(jnp.int32, ag.shape, 1)
    step = 1
    while step < SUBLANES:
        bg = ag * _shift_rows(bg, step, 0.0, sub) + bg
        ag = ag * _shift_rows(ag, step, 1.0, sub)
        step *= 2
    last_rows = pl.ds(SUBLANES - 1, S * G, stride=SUBLANES)

    def group_ends(x, scr):
        x = x.reshape(S * L, C)
        for c in range(C // LANES):
            scr[c] = x[:, c * LANES:(c + 1) * LANES]
        ends = [scr[c, last_rows, :] for c in range(C // LANES)]
        return jnp.concatenate(ends, axis=1).reshape(S, G, C)

    a_sum = group_ends(ag, a_scr)
    b_sum = group_ends(bg, b_scr)
    grp = lax.broadcasted_iota(jnp.int32, a_sum.shape, 1)
    step = 1
    while step < G:
        b_sum = a_sum * _shift_rows(b_sum, step, 0.0, grp) + b_sum
        a_sum = a_sum * _shift_rows(a_sum, step, 1.0, grp)
        step *= 2
    h_end = b_sum + a_sum * h_in
    h_enter = _shift_rows(h_end, 1, h_in, grp)
    h = bg + ag * h_enter.reshape(S * G, 1, C)
    return h.reshape(S, L, C), h_end[:, G - 1:G, :]


def _front_kernel(x_ref, mod_ref, conv0_ref, h0_ref, g_pre_ref, w_in_ref, conv_w_ref, conv_b_ref,
                  w_gate_ref, b_gate_ref, lam_ref, g_lru_ref,
                  olru_ref, q_ref, kbf_ref, vbf_ref, k_ref, v_ref, conv_out_ref, h_out_ref,
                  ext_ref, hc_ref, a_scr, b_scr, *, n_seq, n_rows):
    S, L = n_seq, n_rows

    @pl.when(pl.program_id(0) == 0)
    def _():
        ext_ref[:, 0:CONV_CARRY_ROWS, :] = conv0_ref[...]
        hc_ref[...] = h0_ref[...]

    x = x_ref[...]
    shift = mod_ref[:, 0:1, :]
    scale = mod_ref[:, 1:2, :]
    h = _rms(x, g_pre_ref[...] * (1.0 + scale)) + shift
    hb = h.reshape(S * L, D_MODEL).astype(BF16)

    def project(first_col, n_cols):
        return _dot(hb, w_in_ref[:, first_col:first_col + n_cols])

    lru_in = project(0, 2 * D_LRU)
    xl = lru_in[:, 0:D_LRU].reshape(S, L, D_LRU)
    gl = lru_in[:, D_LRU:].reshape(S, L, D_LRU)

    k = project(2 * D_LRU + D_SB, D_SB)
    k_ref[...] = k.reshape(S, L, D_SB)
    kbf_ref[...] = k.astype(BF16).reshape(S, L, D_SB)

    ext_ref[:, CONV_CARRY_ROWS:, :] = xl
    cw = conv_w_ref[...]
    xc = conv_b_ref[...]
    for j in range(CONV_WIDTH - 1):
        start = CONV_CARRY_ROWS - (CONV_WIDTH - 1 - j)
        xc = xc + ext_ref[:, start:start + L, :] * cw[j:j + 1, :]
    xc = xc + xl * cw[CONV_WIDTH - 1:CONV_WIDTH, :]
    conv_out_ref[...] = ext_ref[:, L + CONV_CARRY_ROWS - (CONV_WIDTH - 1):, :]
    ext_ref[:, 0:CONV_CARRY_ROWS, :] = xl[:, L - CONV_CARRY_ROWS:, :]

    gates = _dot(xc.reshape(S * L, D_LRU).astype(BF16), w_gate_ref[...]) + b_gate_ref[...]
    v = project(2 * D_LRU + 2 * D_SB, D_SB)
    v_ref[...] = v.reshape(S, L, D_SB)
    vbf_ref[...] = v.astype(BF16).reshape(S, L, D_SB)

    r = jax.nn.sigmoid(gates[:, 0:D_LRU]).reshape(S, L, D_LRU)
    i = jax.nn.sigmoid(gates[:, D_LRU:]).reshape(S, L, D_LRU)
    neg_lam = -lam_ref[...]
    decay = RG_C * (jnp.maximum(neg_lam, 0.0) + jnp.log1p(jnp.exp(-jnp.abs(neg_lam))))
    neg_log_a = r * decay
    a = jnp.exp2(r * (decay * -LOG2_E))
    m = jnp.tanh(neg_log_a) * (1.0 + a * a)
    u = (m * lax.rsqrt(jnp.maximum(m, F32_TINY))) * (i * xc)

    q_ref[...] = project(2 * D_LRU, D_SB).astype(BF16).reshape(S, L, D_SB)

    hl, h_last = _lru_scan(a, u, hc_ref[...], a_scr, b_scr)
    hc_ref[...] = h_last
    h_out_ref[...] = h_last

    o = hl * jax.nn.gelu(gl)
    olru_ref[...] = _rms(o, g_lru_ref[...]).astype(BF16)


def _front(x, mod, conv0, h0, g_pre, w_in_bf, conv_w, conv_b, w_gate_bf, b_gate, lam, g_lru,
           *, n_rows, name):
    S, T, _ = x.shape
    L = n_rows
    n_t = T // L
    const2 = lambda t: (0, 0)
    const3 = lambda t: (0, 0, 0)
    tile3 = lambda t: (0, t, 0)
    kernel = functools.partial(_front_kernel, n_seq=S, n_rows=L)
    return pl.pallas_call(
        kernel,
        grid=(n_t,),
        in_specs=[
            pl.BlockSpec((S, L, D_MODEL), tile3),
            pl.BlockSpec((S, N_MOD, D_MODEL), const3),
            pl.BlockSpec((S, CONV_CARRY_ROWS, D_LRU), const3),
            pl.BlockSpec((S, 1, D_LRU), const3),
            pl.BlockSpec((1, D_MODEL), const2),
            pl.BlockSpec((D_MODEL, D_IN), const2),
            pl.BlockSpec((CONV_WIDTH, D_LRU), const2),
            pl.BlockSpec((1, D_LRU), const2),
            pl.BlockSpec((D_LRU, 2 * D_LRU), const2),
            pl.BlockSpec((1, 2 * D_LRU), const2),
            pl.BlockSpec((1, D_LRU), const2),
            pl.BlockSpec((1, D_LRU), const2),
        ],
        out_specs=[
            pl.BlockSpec((S, L, D_LRU), tile3),
            pl.BlockSpec((S, L, D_SB), tile3),
            pl.BlockSpec((S, L, D_SB), tile3),
            pl.BlockSpec((S, L, D_SB), tile3),
            pl.BlockSpec((S, L, D_SB), tile3),
            pl.BlockSpec((S, L, D_SB), tile3),
            pl.BlockSpec((S, CONV_WIDTH - 1, D_LRU), const3),
            pl.BlockSpec((S, 1, D_LRU), const3),
        ],
        out_shape=[
            jax.ShapeDtypeStruct((S, T, D_LRU), BF16),
            jax.ShapeDtypeStruct((S, T, D_SB), BF16),
            jax.ShapeDtypeStruct((S, T, D_SB), BF16),
            jax.ShapeDtypeStruct((S, T, D_SB), BF16),
            jax.ShapeDtypeStruct((S, T, D_SB), F32),
            jax.ShapeDtypeStruct((S, T, D_SB), F32),
            jax.ShapeDtypeStruct((S, CONV_WIDTH - 1, D_LRU), F32),
            jax.ShapeDtypeStruct((S, 1, D_LRU), F32),
        ],
        scratch_shapes=[
            pltpu.VMEM((S, CONV_CARRY_ROWS + L, D_LRU), F32),
            pltpu.VMEM((S, 1, D_LRU), F32),
            pltpu.VMEM((D_LRU // LANES, S * L, LANES), F32),
            pltpu.VMEM((D_LRU // LANES, S * L, LANES), F32),
        ],
        compiler_params=pltpu.CompilerParams(
            dimension_semantics=("arbitrary",), vmem_limit_bytes=VMEM_LIMIT_BYTES),
        name=name,
    )(x, mod, conv0, h0, g_pre, w_in_bf, conv_w, conv_b, w_gate_bf, b_gate, lam, g_lru)


def _prefix_ones(n):
    j = lax.broadcasted_iota(jnp.int32, (n, n), 0)
    s = lax.broadcasted_iota(jnp.int32, (n, n), 1)
    return jnp.where(j > s, -1.0, 0.0).astype(BF16)


def _stack_heads(x):
    lane = lax.broadcasted_iota(jnp.int32, x.shape, 1)
    zero = jnp.zeros_like(x)
    return jnp.concatenate([jnp.where(lane < SB_HEAD_DIM, x, zero),
                            jnp.where(lane >= SB_HEAD_DIM, x, zero)], axis=0)


def _pair_cols(p):
    return slice(p * HEAD_PAIR, (p + 1) * HEAD_PAIR)


HIDDEN_CARRY = -1e30


def _sweep_tiles(q_pairs, tiles, state):
    result = []
    for _ in _sweep_stages(q_pairs, tiles, state, result):
        pass
    return result[0]


def _zero_after(x):
    bits = lax.bitcast_convert_type(x[-SUBLANES:, -LANES:], jnp.uint32)
    zero = lax.shift_right_logical(lax.shift_right_logical(bits, jnp.uint32(16)), jnp.uint32(16))
    return zero[0:1, 0:1].astype(F32)


def _sweep_stages(q_pairs, tiles, state, result, gates=None):
    accs, carries = state
    tq = accs[0].shape[0]
    pairs = range(N_PAIRS)
    lane = lax.broadcasted_iota(jnp.int32, (tq, HEAD_PAIR), 1)

    def scores(kt):
        return [lax.dot_general(q_pairs[p], kt[:, _pair_cols(p)], (((1,), (1,)), ((), ())),
                                preferred_element_type=F32) for p in pairs]

    def gate(stage):
        return gates[stage - 2] if gates is not None and 0 <= stage - 2 < len(gates) else 0.0

    def drop_terms(z, mask, zero):
        log_beta, drop, stacked = [], [], []
        scale = zero - LOG2_E
        for p in pairs:
            sp = jnp.maximum(z[p], 0.0) + jnp.log(1.0 + jnp.exp2(jnp.abs(z[p]) * scale))
            log_beta.append(z[p] - sp)
            if mask is not None:
                sp = jnp.where(mask, sp, 0.0)
            drop.append(sp)
            stacked.append(sp.astype(BF16))
        return log_beta, drop, jnp.concatenate(stacked, axis=0)

    def weigh(log_beta, log_keep_after, carries, mask, vt, accs):
        new_accs = []
        for p in pairs:
            w = jnp.exp(log_beta[p] + log_keep_after[2 * tq * p:2 * tq * (p + 1)] + carries[p])
            if mask is not None:
                w = jnp.where(mask, w, 0.0)
            pv = _dot(w.astype(BF16), vt[:, _pair_cols(p)])
            new_accs.append(accs[p] + jnp.where(lane < SB_HEAD_DIM, pv[:tq], pv[tq:]))
        return new_accs

    z = [scores(tile[0]) for tile in tiles]
    yield
    stage = 1
    log_beta, log_keep_after, tile_carries = [], [], []
    for n, (_, _, neg_ones, mask, visible) in enumerate(tiles):
        lb, drop, stacked = drop_terms(z[n], mask, gate(stage))
        log_beta.append(lb)
        log_keep_after.append(_dot(stacked, neg_ones))
        if visible is not None:
            carries = tuple(jnp.where(visible, c, HIDDEN_CARRY) for c in carries)
        tile_carries.append(carries)
        carries = tuple(carries[p] - jnp.sum(drop[p], axis=-1, keepdims=True) for p in pairs)
        stage += 1
        yield
    for n, (_, vt, _, mask, _) in enumerate(tiles):
        gated = tuple(c + gate(stage) for c in tile_carries[n])
        accs = weigh(log_beta[n], log_keep_after[n], gated, mask, vt, accs)
        stage += 1
        yield
    result.append((tuple(accs), carries))


def _attn_kernel(q_ref, kd_ref, vd_ref, kp_ref, vp_ref, ksrc_ref, vsrc_ref, o_ref,
                 kt_ref, vt_ref, sem, *, tq, n_before):
    b = pl.program_id(0)
    i = pl.program_id(1)
    before = i * tq if n_before is None else n_before

    q = q_ref[...]
    q_pairs = [_stack_heads(q[:, p * HEAD_PAIR:(p + 1) * HEAD_PAIR]) for p in range(N_PAIRS)]

    state = (tuple(jnp.zeros((tq, HEAD_PAIR), F32) for _ in range(N_PAIRS)),
             tuple(jnp.zeros((2 * tq, 1), F32) for _ in range(N_PAIRS)))
    t_idx = lax.broadcasted_iota(jnp.int32, (tq, tq), 0)
    s_idx = lax.broadcasted_iota(jnp.int32, (tq, tq), 1)
    causal = s_idx < t_idx
    own_tile = (kd_ref[...], vd_ref[...], _prefix_ones(tq), jnp.concatenate([causal, causal], axis=0),
                None)

    def write(state):
        o_ref[...] = jnp.concatenate(state[0], axis=1)

    def own_and_earlier_keys(state):
        tri = _prefix_ones(KEY_TILE)
        prev_tile = (kp_ref[...].astype(BF16), vp_ref[...].astype(BF16), tri, None, None)
        state = _sweep_tiles(q_pairs, [own_tile, prev_tile], state)
        n_tail = before // KEY_TILE - 1

        def live(carries):
            return jnp.max(functools.reduce(jnp.maximum, carries))

        def cond(c):
            n, top, _ = c
            return jnp.logical_and(n < n_tail, top > LOG_WEIGHT_FLOOR)

        def body(c):
            n, _, state = c
            rows = pl.ds(pl.multiple_of(before - (n + 2) * KEY_TILE, KEY_TILE), KEY_TILE)
            copies = (pltpu.make_async_copy(ksrc_ref.at[b, rows, :], kt_ref, sem.at[0]),
                      pltpu.make_async_copy(vsrc_ref.at[b, rows, :], vt_ref, sem.at[1]))
            for cp in copies:
                cp.start()
            for cp in copies:
                cp.wait()
            tile = (kt_ref[...].astype(BF16), vt_ref[...].astype(BF16), tri, None, None)
            state = _sweep_tiles(q_pairs, [tile], state)
            return n + 1, live(state[1]), state

        return lax.while_loop(cond, body, (jnp.int32(0), live(state[1]), state))[2]

    if n_before is None:
        @pl.when(i == 0)
        def _():
            write(_sweep_tiles(q_pairs, [own_tile], state))

        @pl.when(i > 0)
        def _():
            write(own_and_earlier_keys(state))
    else:
        write(own_and_earlier_keys(state))


def _attention(q, k_new, v_new, k_before, v_before, *, tq, name):
    B, T, _ = q.shape
    n_q = T // tq
    if k_before is None:
        assert tq == KEY_TILE
        k_before, v_before, n_before = k_new, v_new, None
        prev_idx = lambda b, i: (b, jnp.maximum(i - 1, 0), 0)
    else:
        assert n_q == 1
        n_before = k_before.shape[1]
        prev_idx = lambda b, i: (b, n_before // KEY_TILE - 1, 0)
    tile = lambda b, i: (b, i, 0)
    new_spec = pl.BlockSpec((None, tq, D_SB), tile)
    prev_spec = pl.BlockSpec((None, KEY_TILE, D_SB), prev_idx)
    hbm_spec = pl.BlockSpec(memory_space=pl.ANY)
    kernel = functools.partial(_attn_kernel, tq=tq, n_before=n_before)
    return pl.pallas_call(
        kernel,
        grid=(B, n_q),
        in_specs=[new_spec, new_spec, new_spec, prev_spec, prev_spec, hbm_spec, hbm_spec],
        out_specs=pl.BlockSpec((None, tq, D_SB), tile),
        out_shape=jax.ShapeDtypeStruct((B, T, D_SB), F32),
        scratch_shapes=[
            pltpu.VMEM((KEY_TILE, D_SB), k_before.dtype),
            pltpu.VMEM((KEY_TILE, D_SB), v_before.dtype),
            pltpu.SemaphoreType.DMA((2,)),
        ],
        compiler_params=pltpu.CompilerParams(
            dimension_semantics=("arbitrary", "arbitrary"), vmem_limit_bytes=VMEM_LIMIT_BYTES),
        name=name,
    )(q, k_new, v_new, k_before, v_before, k_before, v_before)


def _back_kernel(x_ref, olru_ref, osb_ref, mod_ref, g_sb_ref, g_post_mix_ref, g_pre_mlp_ref,
                 g_post_mlp_ref, w_out_ref, w_up_ref, w_down_ref, y_ref, *, n_seq, n_rows, ff_chunk):
    S, L = n_seq, n_rows
    if S > 1:
        halves = [(slice(0, S // 2), slice(None)), (slice(S // 2, S), slice(None))]
        hs, hl = S // 2, L
    else:
        halves = [(slice(None), slice(0, L // 2)), (slice(None), slice(L // 2, L))]
        hs, hl = S, L // 2
    n_chunks = D_FF // ff_chunk

    def mod(idx, row):
        return mod_ref[idx[0], row:row + 1, :]

    def attn_out(idx):
        o_lru = olru_ref[idx[0], idx[1], :].reshape(hs * hl, D_LRU)
        o_sb = _rms(osb_ref[idx[0], idx[1], :], g_sb_ref[...]).reshape(hs * hl, D_SB).astype(BF16)
        return _dot(o_lru, w_out_ref[0:D_LRU, :]) + _dot(o_sb, w_out_ref[D_LRU:, :])

    def mix_residual(idx, o):
        x = x_ref[idx[0], idx[1], :] + mod(idx, 2) * _rms(o, g_post_mix_ref[...]).reshape(hs, hl, D_MODEL)
        h = _rms(x, g_pre_mlp_ref[...] * (1.0 + mod(idx, 4))) + mod(idx, 3)
        return x, h.reshape(hs * hl, D_MODEL).astype(BF16)

    def mlp_chunk(hb, f, c):
        cols = slice(c * ff_chunk, (c + 1) * ff_chunk)
        up = jnp.maximum(_dot(hb, w_up_ref[:, cols]), 0.0)
        return f + _dot((up * up).astype(BF16), w_down_ref[cols, :])

    def mlp_residual(idx, x, f):
        y_ref[idx[0], idx[1], :] = x + mod(idx, 5) * _rms(f, g_post_mlp_ref[...]).reshape(hs, hl, D_MODEL)

    first, second = halves
    zero = jnp.zeros((hs * hl, D_MODEL), F32)
    o1 = attn_out(first)
    o2 = attn_out(second)
    x1, hb1 = mix_residual(first, o1)
    f1 = mlp_chunk(hb1, zero, 0)
    x2, hb2 = mix_residual(second, o2)
    for c in range(1, n_chunks):
        f1 = mlp_chunk(hb1, f1, c)
    f2 = mlp_chunk(hb2, zero, 0)
    mlp_residual(first, x1, f1)
    for c in range(1, n_chunks):
        f2 = mlp_chunk(hb2, f2, c)
    mlp_residual(second, x2, f2)


def _back(x, o_lru, o_sb, mod, g_sb, g_post_mix, g_pre_mlp, g_post_mlp, w_out_bf, w_up_bf, w_down_bf,
          *, n_seq, n_rows, name, ff_chunk=1024):
    B, T, _ = x.shape
    S, L = n_seq, n_rows
    const2 = lambda b, t: (0, 0)
    tile3 = lambda b, t: (b, t, 0)
    seq3 = lambda b, t: (b, 0, 0)
    kernel = functools.partial(_back_kernel, n_seq=S, n_rows=L, ff_chunk=ff_chunk)
    return pl.pallas_call(
        kernel,
        grid=(B // S, T // L),
        in_specs=[
            pl.BlockSpec((S, L, D_MODEL), tile3),
            pl.BlockSpec((S, L, D_LRU), tile3),
            pl.BlockSpec((S, L, D_SB), tile3),
            pl.BlockSpec((S, N_MOD, D_MODEL), seq3),
            pl.BlockSpec((1, D_SB), const2),
            pl.BlockSpec((1, D_MODEL), const2),
            pl.BlockSpec((1, D_MODEL), const2),
            pl.BlockSpec((1, D_MODEL), const2),
            pl.BlockSpec((D_MODEL, D_MODEL), const2),
            pl.BlockSpec((D_MODEL, D_FF), const2),
            pl.BlockSpec((D_FF, D_MODEL), const2),
        ],
        out_specs=pl.BlockSpec((S, L, D_MODEL), tile3),
        out_shape=jax.ShapeDtypeStruct((B, T, D_MODEL), F32),
        compiler_params=pltpu.CompilerParams(
            dimension_semantics=("arbitrary", "arbitrary"), vmem_limit_bytes=VMEM_LIMIT_BYTES),
        name=name,
    )(x, o_lru, o_sb, mod, g_sb, g_post_mix, g_pre_mlp, g_post_mlp, w_out_bf, w_up_bf, w_down_bf)


def _back_half_stages(x_ref, olru_ref, o_sb, mod_ref, g_sb_ref, g_post_mix_ref, g_pre_mlp_ref,
                      g_post_mlp_ref, w_out_ref, w_up_ref, w_down_ref, y_ref, rows, ff_chunk, gates):
    def mod(row):
        return mod_ref[row:row + 1, :]

    o_sb = _rms(o_sb, g_sb_ref[...]).astype(BF16)
    o = _dot(olru_ref[rows, :], w_out_ref[0:D_LRU, :]) + _dot(o_sb, w_out_ref[D_LRU:, :])
    yield
    x = x_ref[rows, :] + mod(2) * _rms(o, g_post_mix_ref[...])
    hb = (_rms(x, g_pre_mlp_ref[...] * (1.0 + mod(4))) + mod(3)).astype(BF16)
    yield
    n_chunks = D_FF // ff_chunk
    ups = []
    for c in range(n_chunks):
        ups.append(_dot(hb, w_up_ref[:, c * ff_chunk:(c + 1) * ff_chunk]))
        gates.append(_zero_after(ups[c]))
        yield
    f = jnp.zeros(x.shape, F32)
    for c in range(n_chunks):
        up = jnp.maximum(ups[c], 0.0)
        f = f + _dot((up * up).astype(BF16), w_down_ref[c * ff_chunk:(c + 1) * ff_chunk, :])
    y_ref[rows, :] = x + mod(5) * _rms(f, g_post_mlp_ref[...])
    yield


def _alternate(*stage_generators):
    running = list(stage_generators)
    while running:
        for gen in list(running):
            try:
                next(gen)
            except StopIteration:
                running.remove(gen)


def _attn_back_kernel(q_ref, k_ref, v_ref, kp_ref, vp_ref, ksrc_ref, vsrc_ref,
                      x_ref, olru_ref, mod_ref, g_sb_ref, g_post_mix_ref, g_pre_mlp_ref,
                      g_post_mlp_ref, w_out_ref, w_up_ref, w_down_ref,
                      y_ref, osb_ref, kt_ref, vt_ref, sem, *, n_tiles, ff_chunk):
    i = pl.program_id(0)
    tq = KEY_TILE
    n_q = q_ref.shape[0] // tq
    write_slot = i % 2
    read_slot = 1 - write_slot
    tile = jnp.minimum(i, n_tiles - 1)

    @pl.when(i == 0)
    def _():
        osb_ref[...] = jnp.zeros(osb_ref.shape, F32)

    neg_ones = _prefix_ones(tq)
    t_idx = lax.broadcasted_iota(jnp.int32, (tq, tq), 0)
    s_idx = lax.broadcasted_iota(jnp.int32, (tq, tq), 1)
    causal = s_idx < t_idx
    causal = jnp.concatenate([causal, causal], axis=0)

    for t in range(n_q):
        rows = slice(t * tq, (t + 1) * tq)
        q_tile = tile * n_q + t
        before = q_tile * tq
        q = q_ref[rows, :]
        q_pairs = [_stack_heads(q[:, _pair_cols(p)]) for p in range(N_PAIRS)]
        own_tile = (k_ref[rows, :], v_ref[rows, :], neg_ones, causal, None)
        if t == 0:
            prev_tile = (kp_ref[...], vp_ref[...], neg_ones, None, q_tile > 0)
        else:
            prev_rows = slice((t - 1) * tq, t * tq)
            prev_tile = (k_ref[prev_rows, :], v_ref[prev_rows, :], neg_ones, None, None)
        state = (tuple(jnp.zeros((tq, HEAD_PAIR), F32) for _ in range(N_PAIRS)),
                 tuple(jnp.zeros((2 * tq, 1), F32) for _ in range(N_PAIRS)))
        result, gates = [], []
        back = _back_half_stages(x_ref, olru_ref, osb_ref[read_slot, rows, :], mod_ref, g_sb_ref,
                                 g_post_mix_ref, g_pre_mlp_ref, g_post_mlp_ref, w_out_ref, w_up_ref,
                                 w_down_ref, y_ref, rows, ff_chunk, gates)
        next(back)
        next(back)
        _alternate(_sweep_stages(q_pairs, [own_tile, prev_tile], state, result, gates), back)
        state = result[0]

        n_tail = q_tile - 1

        def live(carries):
            return jnp.max(functools.reduce(jnp.maximum, carries))

        def cond(c):
            n, top, _ = c
            return jnp.logical_and(n < n_tail, top > LOG_WEIGHT_FLOOR)

        def body(c, q_pairs=q_pairs, before=before):
            n, _, state = c
            key_rows = pl.ds(pl.multiple_of(before - (n + 2) * KEY_TILE, KEY_TILE), KEY_TILE)
            copies = (pltpu.make_async_copy(ksrc_ref.at[0, key_rows, :], kt_ref, sem.at[0]),
                      pltpu.make_async_copy(vsrc_ref.at[0, key_rows, :], vt_ref, sem.at[1]))
            for cp in copies:
                cp.start()
            for cp in copies:
                cp.wait()
            state = _sweep_tiles(q_pairs, [(kt_ref[...], vt_ref[...], neg_ones, None, None)], state)
            return n + 1, live(state[1]), state

        state = lax.while_loop(cond, body, (jnp.int32(0), live(state[1]), state))[2]
        osb_ref[write_slot, rows, :] = jnp.concatenate(state[0], axis=1)


def _attn_back(q, kbf, vbf, x, o_lru, mod, g_sb, g_post_mix, g_pre_mlp, g_post_mlp,
               w_out_bf, w_up_bf, w_down_bf, *, n_rows, name, ff_chunk=1024):
    _, T, _ = x.shape
    R = n_rows
    n_tiles = T // R
    attn_tile = lambda i: jnp.minimum(i, n_tiles - 1)
    back_tile = lambda i: jnp.maximum(i - 1, 0)
    const2 = lambda i: (0, 0)
    qkv_spec = pl.BlockSpec((None, R, D_SB), lambda i: (0, attn_tile(i), 0))
    prev_spec = pl.BlockSpec((None, KEY_TILE, D_SB),
                             lambda i: (0, jnp.maximum(attn_tile(i) * (R // KEY_TILE) - 1, 0), 0))
    hbm_spec = pl.BlockSpec(memory_space=pl.ANY)
    kernel = functools.partial(_attn_back_kernel, n_tiles=n_tiles, ff_chunk=ff_chunk)
    return pl.pallas_call(
        kernel,
        grid=(n_tiles + 1,),
        in_specs=[
            qkv_spec, qkv_spec, qkv_spec, prev_spec, prev_spec, hbm_spec, hbm_spec,
            pl.BlockSpec((None, R, D_MODEL), lambda i: (0, back_tile(i), 0)),
            pl.BlockSpec((None, R, D_LRU), lambda i: (0, back_tile(i), 0)),
            pl.BlockSpec((None, N_MOD, D_MODEL), lambda i: (0, 0, 0)),
            pl.BlockSpec((1, D_SB), const2),
            pl.BlockSpec((1, D_MODEL), const2),
            pl.BlockSpec((1, D_MODEL), const2),
            pl.BlockSpec((1, D_MODEL), const2),
            pl.BlockSpec((D_MODEL, D_MODEL), const2),
            pl.BlockSpec((D_MODEL, D_FF), const2),
            pl.BlockSpec((D_FF, D_MODEL), const2),
        ],
        out_specs=pl.BlockSpec((None, R, D_MODEL), lambda i: (0, back_tile(i), 0)),
        out_shape=jax.ShapeDtypeStruct((1, T, D_MODEL), F32),
        scratch_shapes=[
            pltpu.VMEM((2, R, D_SB), F32),
            pltpu.VMEM((KEY_TILE, D_SB), BF16),
            pltpu.VMEM((KEY_TILE, D_SB), BF16),
            pltpu.SemaphoreType.DMA((2,)),
        ],
        compiler_params=pltpu.CompilerParams(
            dimension_semantics=("arbitrary",), vmem_limit_bytes=VMEM_LIMIT_BYTES),
        name=name,
    )(q, kbf, vbf, kbf, vbf, kbf, vbf, x, o_lru, mod, g_sb, g_post_mix, g_pre_mlp, g_post_mlp,
      w_out_bf, w_up_bf, w_down_bf)


FRONT_ROWS = 1024
BACK_ROWS = 1024
ATTN_BACK_ROWS = 512


def _block_diag(w):
    n, bi, bj = w.shape
    eye = jnp.eye(n, dtype=w.dtype)
    return (eye[:, None, :, None] * w[:, :, None, :]).reshape(n * bi, n * bj)


def _pad_conv_state(buf):
    return jnp.pad(buf, ((0, 0), (CONV_CARRY_ROWS - (CONV_WIDTH - 1), 0), (0, 0)))


def kernel(x_prompt, x_sample, c_prompt, c_sample, cache_conv, state_lru, cache_k, cache_v, w_ada, b_ada, g_pre_mix, g_post_mix, g_pre_mlp, g_post_mlp, w_in, conv_w, conv_b, w_rg_a, b_rg_a, w_rg_x, b_rg_x, lru_lambda, g_lru_out, g_sb_out, w_out, w_up, w_down):
    depth = w_ada.shape[0]
    bp, seq, _ = x_prompt.shape
    bs, dec_seq, _ = x_sample.shape
    assert depth == 1 and bp == 1, "single-layer, single-prompt configuration"
    past_len = cache_k.shape[2]
    l = 0

    row = lambda a: a.reshape(1, -1)
    q_cols = jnp.arange(D_IN) // D_SB == 2 * D_LRU // D_SB
    w_in_bf = (w_in[l] * jnp.where(q_cols, SB_HEAD_DIM ** -0.5, 1.0)).astype(BF16)
    w_gate_bf = jnp.concatenate([_block_diag(w_rg_a[l]), _block_diag(w_rg_x[l])], axis=1).astype(BF16)
    b_gate = jnp.concatenate([b_rg_a[l].reshape(-1), b_rg_x[l].reshape(-1)]).reshape(1, -1)
    w_out_bf = w_out[l].astype(BF16)
    w_up_bf = w_up[l].astype(BF16)
    w_down_bf = w_down[l].astype(BF16)
    front_w = (row(g_pre_mix[l]), w_in_bf, conv_w[l], row(conv_b[l]), w_gate_bf, b_gate,
               row(lru_lambda[l]), row(g_lru_out[l]))
    back_w = (row(g_sb_out[l]), row(g_post_mix[l]), row(g_pre_mlp[l]), row(g_post_mlp[l]),
              w_out_bf, w_up_bf, w_down_bf)

    mod = _adaln(jnp.concatenate([c_prompt, c_sample], axis=0), w_ada[l], b_ada[l])
    mod = mod.reshape(bp + bs, N_MOD, D_MODEL)
    mod_p, mod_s = mod[:bp], mod[bp:]

    zeros_conv = jnp.zeros((bp, CONV_CARRY_ROWS, D_LRU), F32)
    zeros_h = jnp.zeros((bp, 1, D_LRU), F32)
    olru_p, q_p, kbf_p, vbf_p, k_p, v_p, conv_p, h_p = _front(
        x_prompt, mod_p, zeros_conv, zeros_h, *front_w, n_rows=FRONT_ROWS, name="front_prompt")
    y_p = _attn_back(q_p, kbf_p, vbf_p, x_prompt, olru_p, mod_p, *back_w, n_rows=ATTN_BACK_ROWS,
                     name="attn_back_prompt")

    olru_s, q_s, kbf_s, vbf_s, k_s, v_s, conv_s, h_s = _front(
        x_sample, mod_s, _pad_conv_state(cache_conv[l]), state_lru[l].reshape(bs, 1, D_LRU),
        *front_w, n_rows=dec_seq, name="front_sample")
    osb_s = _attention(q_s, kbf_s, vbf_s, cache_k[l].reshape(bs, past_len, D_SB),
                       cache_v[l].reshape(bs, past_len, D_SB), tq=dec_seq, name="attn_sample")
    y_s = _back(x_sample, olru_s, osb_s, mod_s, *back_w, n_seq=8, n_rows=dec_seq, name="back_sample")

    heads = lambda a: a.reshape(1, a.shape[0], a.shape[1], N_SB_HEADS, SB_HEAD_DIM)
    return (y_p, y_s, conv_p[None], h_p.reshape(1, bp, D_LRU), heads(k_p), heads(v_p),
            conv_s[None], h_s.reshape(1, bs, D_LRU), heads(k_s), heads(v_s))
```

```python
import functools

import jax
import jax.numpy as jnp
from jax import lax
from jax.experimental import pallas as pl
from jax.experimental.pallas import tpu as pltpu

D_MODEL = 1024
D_LRU = 512
D_SB = 512
N_SB_HEADS = 8
SB_HEAD_DIM = 64
HEAD_PAIR = 2 * SB_HEAD_DIM
N_PAIRS = N_SB_HEADS // 2
CONV_WIDTH = 4
CONV_CARRY_ROWS = 8
RG_C = 8.0
D_FF = 4 * D_MODEL
D_IN = 2 * D_LRU + 3 * D_SB
N_MOD = 6
EPS = 1e-6
LOG2_E = 1.4426950408889634
F32_TINY = 1.1754943508222875e-38

KEY_TILE = 256
LOG_WEIGHT_FLOOR = -110.0
HIDDEN_CARRY = -1e30

VMEM_LIMIT_BYTES = 56 * 1024 * 1024

BF16 = jnp.bfloat16
F32 = jnp.float32


def _rms(x, g):
    return x * lax.rsqrt(jnp.mean(x * x, axis=-1, keepdims=True) + EPS) * g


def _dot(a, b):
    return jnp.dot(a, b, preferred_element_type=F32)


def _adaln_kernel(c_ref, w_ref, b_ref, o_ref):
    c = c_ref[...]
    s = (c * jax.nn.sigmoid(c)).astype(BF16)
    o_ref[...] = _dot(s, w_ref[...].astype(BF16)) + b_ref[...]


def _adaln(c_all, w_ada, b_ada, *, tn=1536):
    n_rows = c_all.shape[0]
    n_out = w_ada.shape[1]
    return pl.pallas_call(
        _adaln_kernel,
        grid=(n_out // tn,),
        in_specs=[
            pl.BlockSpec((n_rows, D_MODEL), lambda j: (0, 0)),
            pl.BlockSpec((D_MODEL, tn), lambda j: (0, j)),
            pl.BlockSpec((1, tn), lambda j: (0, j)),
        ],
        out_specs=pl.BlockSpec((n_rows, tn), lambda j: (0, j)),
        out_shape=jax.ShapeDtypeStruct((n_rows, n_out), F32),
        compiler_params=pltpu.CompilerParams(
            dimension_semantics=("arbitrary",), vmem_limit_bytes=VMEM_LIMIT_BYTES),
        name="adaln",
    )(c_all, w_ada, b_ada.reshape(1, n_out))


def _shift_rows(x, s, fill, row):
    return jnp.where(row >= s, pltpu.roll(x, s, axis=1), fill)


SUBLANES = 8
LANES = 128


def _lru_scan(a, u, h_in, a_scr, b_scr):
    S, L, C = a.shape
    G = L // SUBLANES
    ag = a.reshape(S * G, SUBLANES, C)
    bg = u.reshape(S * G, SUBLANES, C)
    sub = lax.broadcasted_iota(jnp.int32, ag.shape, 1)
    step = 1
    while step < SUBLANES:
        bg = ag * _shift_rows(bg, step, 0.0, sub) + bg
        ag = ag * _shift_rows(ag, step, 1.0, sub)
        step *= 2
    last_rows = pl.ds(SUBLANES - 1, S * G, stride=SUBLANES)

    def group_ends(x, scr):
        x = x.reshape(S * L, C)
        for c in range(C // LANES):
            scr[c] = x[:, c * LANES:(c + 1) * LANES]
        ends = [scr[c, last_rows, :] for c in range(C // LANES)]
        return jnp.concatenate(ends, axis=1).reshape(S, G, C)

    a_sum = group_ends(ag, a_scr)
    b_sum = group_ends(bg, b_scr)
    grp = lax.broadcasted_iota(jnp.int32, a_sum.shape, 1)
    step = 1
    while step < G:
        b_sum = a_sum * _shift_rows(b_sum, step, 0.0, grp) + b_sum
        a_sum = a_sum * _shift_rows(a_sum, step, 1.0, grp)
        step *= 2
    h_end = b_sum + a_sum * h_in
    h_enter = _shift_rows(h_end, 1, h_in, grp)
    h = bg + ag * h_enter.reshape(S * G, 1, C)
    return h.reshape(S, L, C), h_end[:, G - 1:G, :]


def _front_kernel(x_ref, mod_ref, conv0_ref, h0_ref, g_pre_ref, w_in_ref, conv_w_ref, conv_b_ref,
                  w_gate_ref, b_gate_ref, lam_ref, g_lru_ref,
                  olru_ref, q_ref, kbf_ref, vbf_ref, k_ref, v_ref, conv_out_ref, h_out_ref,
                  ext_ref, hc_ref, a_scr, b_scr, *, n_seq, n_rows):
    S, L = n_seq, n_rows

    @pl.when(pl.program_id(1) == 0)
    def _():
        ext_ref[:, 0:CONV_CARRY_ROWS, :] = conv0_ref[...]
        hc_ref[...] = h0_ref[...]

    x = x_ref[...]
    shift = mod_ref[:, 0:1, :]
    scale = mod_ref[:, 1:2, :]
    h = _rms(x, g_pre_ref[...] * (1.0 + scale)) + shift
    hb = h.reshape(S * L, D_MODEL).astype(BF16)

    def project(first_col, n_cols):
        return _dot(hb, w_in_ref[:, first_col:first_col + n_cols])

    lru_in = project(0, 2 * D_LRU)
    xl = lru_in[:, 0:D_LRU].reshape(S, L, D_LRU)
    gl = lru_in[:, D_LRU:].reshape(S, L, D_LRU)

    k = project(2 * D_LRU + D_SB, D_SB)
    k_ref[...] = k.reshape(S, L, D_SB)
    kbf_ref[...] = k.astype(BF16).reshape(S, L, D_SB)

    ext_ref[:, CONV_CARRY_ROWS:, :] = xl
    cw = conv_w_ref[...]
    xc = conv_b_ref[...]
    for j in range(CONV_WIDTH - 1):
        start = CONV_CARRY_ROWS - (CONV_WIDTH - 1 - j)
        xc = xc + ext_ref[:, start:start + L, :] * cw[j:j + 1, :]
    xc = xc + xl * cw[CONV_WIDTH - 1:CONV_WIDTH, :]
    conv_out_ref[...] = ext_ref[:, L + CONV_CARRY_ROWS - (CONV_WIDTH - 1):, :]
    ext_ref[:, 0:CONV_CARRY_ROWS, :] = xl[:, L - CONV_CARRY_ROWS:, :]

    gates = _dot(xc.reshape(S * L, D_LRU).astype(BF16), w_gate_ref[...]) + b_gate_ref[...]
    v = project(2 * D_LRU + 2 * D_SB, D_SB)
    v_ref[...] = v.reshape(S, L, D_SB)
    vbf_ref[...] = v.astype(BF16).reshape(S, L, D_SB)

    r = jax.nn.sigmoid(gates[:, 0:D_LRU]).reshape(S, L, D_LRU)
    i = jax.nn.sigmoid(gates[:, D_LRU:]).reshape(S, L, D_LRU)
    neg_lam = -lam_ref[...]
    decay = RG_C * (jnp.maximum(neg_lam, 0.0) + jnp.log1p(jnp.exp(-jnp.abs(neg_lam))))
    neg_log_a = r * decay
    a = jnp.exp2(r * (decay * -LOG2_E))
    m = jnp.tanh(neg_log_a) * (1.0 + a * a)
    u = (m * lax.rsqrt(jnp.maximum(m, F32_TINY))) * (i * xc)

    q_ref[...] = project(2 * D_LRU, D_SB).astype(BF16).reshape(S, L, D_SB)

    hl, h_last = _lru_scan(a, u, hc_ref[...], a_scr, b_scr)
    hc_ref[...] = h_last
    h_out_ref[...] = h_last

    o = hl * jax.nn.gelu(gl)
    olru_ref[...] = _rms(o, g_lru_ref[...]).astype(BF16)


def _front(x, mod, conv0, h0, g_pre, w_in_bf, conv_w, conv_b, w_gate_bf, b_gate, lam, g_lru,
           *, n_seq, n_rows, name):
    B, T, _ = x.shape
    S, L = n_seq, n_rows
    const2 = lambda s, t: (0, 0)
    seq3 = lambda s, t: (s, 0, 0)
    tile3 = lambda s, t: (s, t, 0)
    kernel = functools.partial(_front_kernel, n_seq=S, n_rows=L)
    return pl.pallas_call(
        kernel,
        grid=(B // S, T // L),
        in_specs=[
            pl.BlockSpec((S, L, D_MODEL), tile3),
            pl.BlockSpec((S, N_MOD, D_MODEL), seq3),
            pl.BlockSpec((S, CONV_CARRY_ROWS, D_LRU), seq3),
            pl.BlockSpec((S, 1, D_LRU), seq3),
            pl.BlockSpec((1, D_MODEL), const2),
            pl.BlockSpec((D_MODEL, D_IN), const2),
            pl.BlockSpec((CONV_WIDTH, D_LRU), const2),
            pl.BlockSpec((1, D_LRU), const2),
            pl.BlockSpec((D_LRU, 2 * D_LRU), const2),
            pl.BlockSpec((1, 2 * D_LRU), const2),
            pl.BlockSpec((1, D_LRU), const2),
            pl.BlockSpec((1, D_LRU), const2),
        ],
        out_specs=[
            pl.BlockSpec((S, L, D_LRU), tile3),
            pl.BlockSpec((S, L, D_SB), tile3),
            pl.BlockSpec((S, L, D_SB), tile3),
            pl.BlockSpec((S, L, D_SB), tile3),
            pl.BlockSpec((S, L, D_SB), tile3),
            pl.BlockSpec((S, L, D_SB), tile3),
            pl.BlockSpec((S, CONV_WIDTH - 1, D_LRU), seq3),
            pl.BlockSpec((S, 1, D_LRU), seq3),
        ],
        out_shape=[
            jax.ShapeDtypeStruct((B, T, D_LRU), BF16),
            jax.ShapeDtypeStruct((B, T, D_SB), BF16),
            jax.ShapeDtypeStruct((B, T, D_SB), BF16),
            jax.ShapeDtypeStruct((B, T, D_SB), BF16),
            jax.ShapeDtypeStruct((B, T, D_SB), F32),
            jax.ShapeDtypeStruct((B, T, D_SB), F32),
            jax.ShapeDtypeStruct((B, CONV_WIDTH - 1, D_LRU), F32),
            jax.ShapeDtypeStruct((B, 1, D_LRU), F32),
        ],
        scratch_shapes=[
            pltpu.VMEM((S, CONV_CARRY_ROWS + L, D_LRU), F32),
            pltpu.VMEM((S, 1, D_LRU), F32),
            pltpu.VMEM((D_LRU // LANES, S * L, LANES), F32),
            pltpu.VMEM((D_LRU // LANES, S * L, LANES), F32),
        ],
        compiler_params=pltpu.CompilerParams(
            dimension_semantics=("arbitrary", "arbitrary"), vmem_limit_bytes=VMEM_LIMIT_BYTES),
        name=name,
    )(x, mod, conv0, h0, g_pre, w_in_bf, conv_w, conv_b, w_gate_bf, b_gate, lam, g_lru)


def _prefix_ones(n):
    j = lax.broadcasted_iota(jnp.int32, (n, n), 0)
    s = lax.broadcasted_iota(jnp.int32, (n, n), 1)
    return jnp.where(j > s, -1.0, 0.0).astype(BF16)


def _stack_heads(x):
    lane = lax.broadcasted_iota(jnp.int32, x.shape, 1)
    zero = jnp.zeros_like(x)
    return jnp.concatenate([jnp.where(lane < SB_HEAD_DIM, x, zero),
                            jnp.where(lane >= SB_HEAD_DIM, x, zero)], axis=0)


def _pair_cols(p):
    return slice(p * HEAD_PAIR, (p + 1) * HEAD_PAIR)


def _sweep_tiles(q_pairs, tiles, state):
    accs, carries = state
    tq = accs[0].shape[0]
    pairs = range(N_PAIRS)
    lane = lax.broadcasted_iota(jnp.int32, (tq, HEAD_PAIR), 1)

    def scores(kt):
        return [lax.dot_general(q_pairs[p], kt[:, _pair_cols(p)], (((1,), (1,)), ((), ())),
                                preferred_element_type=F32) for p in pairs]

    def drop_terms(z, mask):
        log_beta, drop, stacked = [], [], []
        for p in pairs:
            sp = jnp.maximum(z[p], 0.0) + jnp.log(1.0 + jnp.exp2(jnp.abs(z[p]) * -LOG2_E))
            log_beta.append(z[p] - sp)
            if mask is not None:
                sp = jnp.where(mask, sp, 0.0)
            drop.append(sp)
            stacked.append(sp.astype(BF16))
        return log_beta, drop, jnp.concatenate(stacked, axis=0)

    def weigh(log_beta, log_keep_after, carries, mask, vt, accs):
        new_accs = []
        for p in pairs:
            w = jnp.exp(log_beta[p] + log_keep_after[2 * tq * p:2 * tq * (p + 1)] + carries[p])
            if mask is not None:
                w = jnp.where(mask, w, 0.0)
            pv = _dot(w.astype(BF16), vt[:, _pair_cols(p)])
            new_accs.append(accs[p] + jnp.where(lane < SB_HEAD_DIM, pv[:tq], pv[tq:]))
        return new_accs

    z = [scores(tile[0]) for tile in tiles]
    log_beta, log_keep_after, tile_carries = [], [], []
    for n, (_, _, neg_ones, mask, exists) in enumerate(tiles):
        lb, drop, stacked = drop_terms(z[n], mask)
        log_beta.append(lb)
        log_keep_after.append(_dot(stacked, neg_ones))
        if exists is not None:
            carries = tuple(jnp.where(exists, c, HIDDEN_CARRY) for c in carries)
        tile_carries.append(carries)
        carries = tuple(carries[p] - jnp.sum(drop[p], axis=-1, keepdims=True) for p in pairs)
    for n, (_, vt, _, mask, _) in enumerate(tiles):
        accs = weigh(log_beta[n], log_keep_after[n], tile_carries[n], mask, vt, accs)
    return tuple(accs), carries


def _attn_kernel(q_ref, kd_ref, vd_ref, kp_ref, vp_ref, ksrc_ref, vsrc_ref, o_ref,
                 kt_ref, vt_ref, sem, *, tq, n_before):
    b = pl.program_id(0)
    i = pl.program_id(1)
    before = i * tq if n_before is None else n_before

    q = q_ref[...]
    q_pairs = [_stack_heads(q[:, _pair_cols(p)]) for p in range(N_PAIRS)]

    state = (tuple(jnp.zeros((tq, HEAD_PAIR), F32) for _ in range(N_PAIRS)),
             tuple(jnp.zeros((2 * tq, 1), F32) for _ in range(N_PAIRS)))
    t_idx = lax.broadcasted_iota(jnp.int32, (tq, tq), 0)
    s_idx = lax.broadcasted_iota(jnp.int32, (tq, tq), 1)
    causal = s_idx < t_idx
    tri = _prefix_ones(KEY_TILE)
    own_tile = (kd_ref[...], vd_ref[...], _prefix_ones(tq), jnp.concatenate([causal, causal], axis=0),
                None)
    prev_tile = (kp_ref[...].astype(BF16), vp_ref[...].astype(BF16), tri, None,
                 i > 0 if n_before is None else None)
    state = _sweep_tiles(q_pairs, [own_tile, prev_tile], state)
    n_tail = before // KEY_TILE - 1

    def live(carries):
        return jnp.max(functools.reduce(jnp.maximum, carries))

    def cond(c):
        n, top, _ = c
        return jnp.logical_and(n < n_tail, top > LOG_WEIGHT_FLOOR)

    def body(c):
        n, _, state = c
        rows = pl.ds(pl.multiple_of(before - (n + 2) * KEY_TILE, KEY_TILE), KEY_TILE)
        copies = (pltpu.make_async_copy(ksrc_ref.at[b, rows, :], kt_ref, sem.at[0]),
                  pltpu.make_async_copy(vsrc_ref.at[b, rows, :], vt_ref, sem.at[1]))
        for cp in copies:
            cp.start()
        for cp in copies:
            cp.wait()
        tile = (kt_ref[...].astype(BF16), vt_ref[...].astype(BF16), tri, None, None)
        state = _sweep_tiles(q_pairs, [tile], state)
        return n + 1, live(state[1]), state

    state = lax.while_loop(cond, body, (jnp.int32(0), live(state[1]), state))[2]
    o_ref[...] = jnp.concatenate(state[0], axis=1)


def _attention(q, k_new, v_new, k_before, v_before, *, tq, name):
    B, T, _ = q.shape
    n_q = T // tq
    if k_before is None:
        assert tq == KEY_TILE
        k_before, v_before, n_before = k_new, v_new, None
        prev_idx = lambda b, i: (b, jnp.maximum(i - 1, 0), 0)
    else:
        assert n_q == 1
        n_before = k_before.shape[1]
        prev_idx = lambda b, i: (b, n_before // KEY_TILE - 1, 0)
    tile = lambda b, i: (b, i, 0)
    new_spec = pl.BlockSpec((None, tq, D_SB), tile)
    prev_spec = pl.BlockSpec((None, KEY_TILE, D_SB), prev_idx)
    hbm_spec = pl.BlockSpec(memory_space=pl.ANY)
    kernel = functools.partial(_attn_kernel, tq=tq, n_before=n_before)
    return pl.pallas_call(
        kernel,
        grid=(B, n_q),
        in_specs=[new_spec, new_spec, new_spec, prev_spec, prev_spec, hbm_spec, hbm_spec],
        out_specs=pl.BlockSpec((None, tq, D_SB), tile),
        out_shape=jax.ShapeDtypeStruct((B, T, D_SB), F32),
        scratch_shapes=[
            pltpu.VMEM((KEY_TILE, D_SB), k_before.dtype),
            pltpu.VMEM((KEY_TILE, D_SB), v_before.dtype),
            pltpu.SemaphoreType.DMA((2,)),
        ],
        compiler_params=pltpu.CompilerParams(
            dimension_semantics=("arbitrary", "arbitrary"), vmem_limit_bytes=VMEM_LIMIT_BYTES),
        name=name,
    )(q, k_new, v_new, k_before, v_before, k_before, v_before)


def _back_kernel(x_ref, olru_ref, osb_ref, mod_ref, g_sb_ref, g_post_mix_ref, g_pre_mlp_ref,
                 g_post_mlp_ref, w_out_ref, w_up_ref, w_down_ref, y_ref, *, n_seq, n_rows, ff_chunk):
    S, L = n_seq, n_rows
    if S > 1:
        halves = [(slice(0, S // 2), slice(None)), (slice(S // 2, S), slice(None))]
        hs, hl = S // 2, L
    else:
        halves = [(slice(None), slice(0, L // 2)), (slice(None), slice(L // 2, L))]
        hs, hl = S, L // 2
    n_chunks = D_FF // ff_chunk

    def mod(idx, row):
        return mod_ref[idx[0], row:row + 1, :]

    def attn_out(idx):
        o_lru = olru_ref[idx[0], idx[1], :].reshape(hs * hl, D_LRU)
        o_sb = _rms(osb_ref[idx[0], idx[1], :], g_sb_ref[...]).reshape(hs * hl, D_SB).astype(BF16)
        return _dot(o_lru, w_out_ref[0:D_LRU, :]) + _dot(o_sb, w_out_ref[D_LRU:, :])

    def mix_residual(idx, o):
        x = x_ref[idx[0], idx[1], :] + mod(idx, 2) * _rms(o, g_post_mix_ref[...]).reshape(hs, hl, D_MODEL)
        h = _rms(x, g_pre_mlp_ref[...] * (1.0 + mod(idx, 4))) + mod(idx, 3)
        return x, h.reshape(hs * hl, D_MODEL).astype(BF16)

    def mlp_chunk(hb, f, c):
        cols = slice(c * ff_chunk, (c + 1) * ff_chunk)
        up = jnp.maximum(_dot(hb, w_up_ref[:, cols]), 0.0)
        return f + _dot((up * up).astype(BF16), w_down_ref[cols, :])

    def mlp_residual(idx, x, f):
        y_ref[idx[0], idx[1], :] = x + mod(idx, 5) * _rms(f, g_post_mlp_ref[...]).reshape(hs, hl, D_MODEL)

    first, second = halves
    zero = jnp.zeros((hs * hl, D_MODEL), F32)
    o1 = attn_out(first)
    o2 = attn_out(second)
    x1, hb1 = mix_residual(first, o1)
    f1 = mlp_chunk(hb1, zero, 0)
    x2, hb2 = mix_residual(second, o2)
    for c in range(1, n_chunks):
        f1 = mlp_chunk(hb1, f1, c)
    f2 = mlp_chunk(hb2, zero, 0)
    mlp_residual(first, x1, f1)
    for c in range(1, n_chunks):
        f2 = mlp_chunk(hb2, f2, c)
    mlp_residual(second, x2, f2)


def _back(x, o_lru, o_sb, mod, g_sb, g_post_mix, g_pre_mlp, g_post_mlp, w_out_bf, w_up_bf, w_down_bf,
          *, n_seq, n_rows, name, ff_chunk=1024):
    B, T, _ = x.shape
    S, L = n_seq, n_rows
    const2 = lambda b, t: (0, 0)
    tile3 = lambda b, t: (b, t, 0)
    seq3 = lambda b, t: (b, 0, 0)
    kernel = functools.partial(_back_kernel, n_seq=S, n_rows=L, ff_chunk=ff_chunk)
    return pl.pallas_call(
        kernel,
        grid=(B // S, T // L),
        in_specs=[
            pl.BlockSpec((S, L, D_MODEL), tile3),
            pl.BlockSpec((S, L, D_LRU), tile3),
            pl.BlockSpec((S, L, D_SB), tile3),
            pl.BlockSpec((S, N_MOD, D_MODEL), seq3),
            pl.BlockSpec((1, D_SB), const2),
            pl.BlockSpec((1, D_MODEL), const2),
            pl.BlockSpec((1, D_MODEL), const2),
            pl.BlockSpec((1, D_MODEL), const2),
            pl.BlockSpec((D_MODEL, D_MODEL), const2),
            pl.BlockSpec((D_MODEL, D_FF), const2),
            pl.BlockSpec((D_FF, D_MODEL), const2),
        ],
        out_specs=pl.BlockSpec((S, L, D_MODEL), tile3),
        out_shape=jax.ShapeDtypeStruct((B, T, D_MODEL), F32),
        compiler_params=pltpu.CompilerParams(
            dimension_semantics=("arbitrary", "arbitrary"), vmem_limit_bytes=VMEM_LIMIT_BYTES),
        name=name,
    )(x, o_lru, o_sb, mod, g_sb, g_post_mix, g_pre_mlp, g_post_mlp, w_out_bf, w_up_bf, w_down_bf)


FRONT_ROWS = 1024
BACK_ROWS = 1024
SAMPLE_SEQS_PER_STEP = 8


def _block_diag(w):
    n, bi, bj = w.shape
    eye = jnp.eye(n, dtype=w.dtype)
    return (eye[:, None, :, None] * w[:, :, None, :]).reshape(n * bi, n * bj)


def _pad_conv_state(buf):
    return jnp.pad(buf, ((0, 0), (CONV_CARRY_ROWS - (CONV_WIDTH - 1), 0), (0, 0)))


def kernel(x_prompt, x_sample, c_prompt, c_sample, cache_conv, state_lru, cache_k, cache_v, w_ada, b_ada, g_pre_mix, g_post_mix, g_pre_mlp, g_post_mlp, w_in, conv_w, conv_b, w_rg_a, b_rg_a, w_rg_x, b_rg_x, lru_lambda, g_lru_out, g_sb_out, w_out, w_up, w_down):
    depth = w_ada.shape[0]
    bp, seq, _ = x_prompt.shape
    bs, dec_seq, _ = x_sample.shape
    assert depth == 1 and bp == 1, "single-layer, single-prompt configuration"
    past_len = cache_k.shape[2]
    l = 0

    row = lambda a: a.reshape(1, -1)
    q_cols = jnp.arange(D_IN) // D_SB == 2 * D_LRU // D_SB
    w_in_bf = (w_in[l] * jnp.where(q_cols, SB_HEAD_DIM ** -0.5, 1.0)).astype(BF16)
    w_gate_bf = jnp.concatenate([_block_diag(w_rg_a[l]), _block_diag(w_rg_x[l])], axis=1).astype(BF16)
    b_gate = jnp.concatenate([b_rg_a[l].reshape(-1), b_rg_x[l].reshape(-1)]).reshape(1, -1)
    w_out_bf = w_out[l].astype(BF16)
    w_up_bf = w_up[l].astype(BF16)
    w_down_bf = w_down[l].astype(BF16)
    front_w = (row(g_pre_mix[l]), w_in_bf, conv_w[l], row(conv_b[l]), w_gate_bf, b_gate,
               row(lru_lambda[l]), row(g_lru_out[l]))
    back_w = (row(g_sb_out[l]), row(g_post_mix[l]), row(g_pre_mlp[l]), row(g_post_mlp[l]),
              w_out_bf, w_up_bf, w_down_bf)

    mod = _adaln(jnp.concatenate([c_prompt, c_sample], axis=0), w_ada[l], b_ada[l])
    mod = mod.reshape(bp + bs, N_MOD, D_MODEL)
    mod_p, mod_s = mod[:bp], mod[bp:]

    zeros_conv = jnp.zeros((bp, CONV_CARRY_ROWS, D_LRU), F32)
    zeros_h = jnp.zeros((bp, 1, D_LRU), F32)
    olru_p, q_p, kbf_p, vbf_p, k_p, v_p, conv_p, h_p = _front(
        x_prompt, mod_p, zeros_conv, zeros_h, *front_w, n_seq=bp, n_rows=FRONT_ROWS,
        name="front_prompt")
    osb_p = _attention(q_p, kbf_p, vbf_p, None, None, tq=KEY_TILE, name="attn_prompt")
    y_p = _back(x_prompt, olru_p, osb_p, mod_p, *back_w, n_seq=1, n_rows=BACK_ROWS, name="back_prompt")

    olru_s, q_s, kbf_s, vbf_s, k_s, v_s, conv_s, h_s = _front(
        x_sample, mod_s, _pad_conv_state(cache_conv[l]), state_lru[l].reshape(bs, 1, D_LRU),
        *front_w, n_seq=SAMPLE_SEQS_PER_STEP, n_rows=dec_seq, name="front_sample")
    osb_s = _attention(q_s, kbf_s, vbf_s, cache_k[l].reshape(bs, past_len, D_SB),
                       cache_v[l].reshape(bs, past_len, D_SB), tq=dec_seq, name="attn_sample")
    y_s = _back(x_sample, olru_s, osb_s, mod_s, *back_w, n_seq=SAMPLE_SEQS_PER_STEP, n_rows=dec_seq,
                name="back_sample")

    heads = lambda a: a.reshape(1, a.shape[0], a.shape[1], N_SB_HEADS, SB_HEAD_DIM)
    return (y_p, y_s, conv_p[None], h_p.reshape(1, bp, D_LRU), heads(k_p), heads(v_p),
            conv_s[None], h_s.reshape(1, bs, D_LRU), heads(k_s), heads(v_s))
```

```python
import functools

import jax
import jax.numpy as jnp
from jax import lax
from jax.experimental import pallas as pl
from jax.experimental.pallas import tpu as pltpu

D_MODEL = 1024
D_LRU = 512
D_SB = 512
N_SB_HEADS = 8
SB_HEAD_DIM = 64
HEAD_PAIR = 2 * SB_HEAD_DIM
N_PAIRS = N_SB_HEADS // 2
CONV_WIDTH = 4
CONV_CARRY_ROWS = 8
RG_C = 8.0
D_FF = 4 * D_MODEL
D_IN = 2 * D_LRU + 3 * D_SB
N_MOD = 6
EPS = 1e-6
LOG2_E = 1.4426950408889634
F32_TINY = 1.1754943508222875e-38

KEY_TILE = 256
LOG_WEIGHT_FLOOR = -110.0
HIDDEN_CARRY = -1e30

VMEM_LIMIT_BYTES = 56 * 1024 * 1024

BF16 = jnp.bfloat16
F32 = jnp.float32


def _rms(x, g):
    return x * lax.rsqrt(jnp.mean(x * x, axis=-1, keepdims=True) + EPS) * g


def _dot(a, b):
    return jnp.dot(a, b, preferred_element_type=F32)


def _adaln_kernel(c_ref, w_ref, b_ref, o_ref):
    c = c_ref[...]
    s = (c * jax.nn.sigmoid(c)).astype(BF16)
    o_ref[...] = _dot(s, w_ref[...].astype(BF16)) + b_ref[...]


def _adaln(c_all, w_ada, b_ada, *, tn=1536):
    n_rows = c_all.shape[0]
    n_out = w_ada.shape[1]
    return pl.pallas_call(
        _adaln_kernel,
        grid=(n_out // tn,),
        in_specs=[
            pl.BlockSpec((n_rows, D_MODEL), lambda j: (0, 0)),
            pl.BlockSpec((D_MODEL, tn), lambda j: (0, j)),
            pl.BlockSpec((1, tn), lambda j: (0, j)),
        ],
        out_specs=pl.BlockSpec((n_rows, tn), lambda j: (0, j)),
        out_shape=jax.ShapeDtypeStruct((n_rows, n_out), F32),
        compiler_params=pltpu.CompilerParams(
            dimension_semantics=("arbitrary",), vmem_limit_bytes=VMEM_LIMIT_BYTES),
        name="adaln",
    )(c_all, w_ada, b_ada.reshape(1, n_out))


def _shift_rows(x, s, fill, row):
    return jnp.where(row >= s, pltpu.roll(x, s, axis=1), fill)


SUBLANES = 8
LANES = 128


def _lru_scan(a, u, h_in, a_scr, b_scr):
    S, L, C = a.shape
    G = L // SUBLANES
    ag = a.reshape(S * G, SUBLANES, C)
    bg = u.reshape(S * G, SUBLANES, C)
    sub = lax.broadcasted_iota(jnp.int32, ag.shape, 1)
    step = 1
    while step < SUBLANES:
        bg = ag * _shift_rows(bg, step, 0.0, sub) + bg
        ag = ag * _shift_rows(ag, step, 1.0, sub)
        step *= 2
    last_rows = pl.ds(SUBLANES - 1, S * G, stride=SUBLANES)

    def group_ends(x, scr):
        x = x.reshape(S * L, C)
        for c in range(C // LANES):
            scr[c] = x[:, c * LANES:(c + 1) * LANES]
        ends = [scr[c, last_rows, :] for c in range(C // LANES)]
        return jnp.concatenate(ends, axis=1).reshape(S, G, C)

    a_sum = group_ends(ag, a_scr)
    b_sum = group_ends(bg, b_scr)
    grp = lax.broadcasted_iota(jnp.int32, a_sum.shape, 1)
    step = 1
    while step < G:
        b_sum = a_sum * _shift_rows(b_sum, step, 0.0, grp) + b_sum
        a_sum = a_sum * _shift_rows(a_sum, step, 1.0, grp)
        step *= 2
    h_end = b_sum + a_sum * h_in
    h_enter = _shift_rows(h_end, 1, h_in, grp)
    h = bg + ag * h_enter.reshape(S * G, 1, C)
    return h.reshape(S, L, C), h_end[:, G - 1:G, :]


def _front_kernel(x_ref, mod_ref, conv0_ref, h0_ref, g_pre_ref, w_in_ref, conv_w_ref, conv_b_ref,
                  w_gate_ref, b_gate_ref, lam_ref, g_lru_ref,
                  olru_ref, q_ref, kbf_ref, vbf_ref, k_ref, v_ref, conv_out_ref, h_out_ref,
                  ext_ref, hc_ref, a_scr, b_scr, *, n_seq, n_rows):
    S, L = n_seq, n_rows

    @pl.when(pl.program_id(1) == 0)
    def _():
        ext_ref[:, 0:CONV_CARRY_ROWS, :] = conv0_ref[...]
        hc_ref[...] = h0_ref[...]

    x = x_ref[...]
    shift = mod_ref[:, 0:1, :]
    scale = mod_ref[:, 1:2, :]
    h = _rms(x, g_pre_ref[...] * (1.0 + scale)) + shift
    hb = h.reshape(S * L, D_MODEL).astype(BF16)

    def project(first_col, n_cols):
        return _dot(hb, w_in_ref[:, first_col:first_col + n_cols])

    lru_in = project(0, 2 * D_LRU)
    xl = lru_in[:, 0:D_LRU].reshape(S, L, D_LRU)
    gl = lru_in[:, D_LRU:].reshape(S, L, D_LRU)

    k = project(2 * D_LRU + D_SB, D_SB)
    k_ref[...] = k.reshape(S, L, D_SB)
    kbf_ref[...] = k.astype(BF16).reshape(S, L, D_SB)

    ext_ref[:, CONV_CARRY_ROWS:, :] = xl
    cw = conv_w_ref[...]
    xc = conv_b_ref[...]
    for j in range(CONV_WIDTH - 1):
        start = CONV_CARRY_ROWS - (CONV_WIDTH - 1 - j)
        xc = xc + ext_ref[:, start:start + L, :] * cw[j:j + 1, :]
    xc = xc + xl * cw[CONV_WIDTH - 1:CONV_WIDTH, :]
    conv_out_ref[...] = ext_ref[:, L + CONV_CARRY_ROWS - (CONV_WIDTH - 1):, :]
    ext_ref[:, 0:CONV_CARRY_ROWS, :] = xl[:, L - CONV_CARRY_ROWS:, :]

    gates = _dot(xc.reshape(S * L, D_LRU).astype(BF16), w_gate_ref[...]) + b_gate_ref[...]
    v = project(2 * D_LRU + 2 * D_SB, D_SB)
    v_ref[...] = v.reshape(S, L, D_SB)
    vbf_ref[...] = v.astype(BF16).reshape(S, L, D_SB)

    r = jax.nn.sigmoid(gates[:, 0:D_LRU]).reshape(S, L, D_LRU)
    i = jax.nn.sigmoid(gates[:, D_LRU:]).reshape(S, L, D_LRU)
    neg_lam = -lam_ref[...]
    decay = RG_C * (jnp.maximum(neg_lam, 0.0) + jnp.log1p(jnp.exp(-jnp.abs(neg_lam))))
    neg_log_a = r * decay
    a = jnp.exp2(r * (decay * -LOG2_E))
    m = jnp.tanh(neg_log_a) * (1.0 + a * a)
    u = (m * lax.rsqrt(jnp.maximum(m, F32_TINY))) * (i * xc)

    q_ref[...] = project(2 * D_LRU, D_SB).astype(BF16).reshape(S, L, D_SB)

    hl, h_last = _lru_scan(a, u, hc_ref[...], a_scr, b_scr)
    hc_ref[...] = h_last
    h_out_ref[...] = h_last

    o = hl * jax.nn.gelu(gl)
    olru_ref[...] = _rms(o, g_lru_ref[...]).astype(BF16)


def _front(x, mod, conv0, h0, g_pre, w_in_bf, conv_w, conv_b, w_gate_bf, b_gate, lam, g_lru,
           *, n_seq, n_rows, name):
    B, T, _ = x.shape
    S, L = n_seq, n_rows
    const2 = lambda s, t: (0, 0)
    seq3 = lambda s, t: (s, 0, 0)
    tile3 = lambda s, t: (s, t, 0)
    kernel = functools.partial(_front_kernel, n_seq=S, n_rows=L)
    return pl.pallas_call(
        kernel,
        grid=(B // S, T // L),
        in_specs=[
            pl.BlockSpec((S, L, D_MODEL), tile3),
            pl.BlockSpec((S, N_MOD, D_MODEL), seq3),
            pl.BlockSpec((S, CONV_CARRY_ROWS, D_LRU), seq3),
            pl.BlockSpec((S, 1, D_LRU), seq3),
            pl.BlockSpec((1, D_MODEL), const2),
            pl.BlockSpec((D_MODEL, D_IN), const2),
            pl.BlockSpec((CONV_WIDTH, D_LRU), const2),
            pl.BlockSpec((1, D_LRU), const2),
            pl.BlockSpec((D_LRU, 2 * D_LRU), const2),
            pl.BlockSpec((1, 2 * D_LRU), const2),
            pl.BlockSpec((1, D_LRU), const2),
            pl.BlockSpec((1, D_LRU), const2),
        ],
        out_specs=[
            pl.BlockSpec((S, L, D_LRU), tile3),
            pl.BlockSpec((S, L, D_SB), tile3),
            pl.BlockSpec((S, L, D_SB), tile3),
            pl.BlockSpec((S, L, D_SB), tile3),
            pl.BlockSpec((S, L, D_SB), tile3),
            pl.BlockSpec((S, L, D_SB), tile3),
            pl.BlockSpec((S, CONV_WIDTH - 1, D_LRU), seq3),
            pl.BlockSpec((S, 1, D_LRU), seq3),
        ],
        out_shape=[
            jax.ShapeDtypeStruct((B, T, D_LRU), BF16),
            jax.ShapeDtypeStruct((B, T, D_SB), BF16),
            jax.ShapeDtypeStruct((B, T, D_SB), BF16),
            jax.ShapeDtypeStruct((B, T, D_SB), BF16),
            jax.ShapeDtypeStruct((B, T, D_SB), F32),
            jax.ShapeDtypeStruct((B, T, D_SB), F32),
            jax.ShapeDtypeStruct((B, CONV_WIDTH - 1, D_LRU), F32),
            jax.ShapeDtypeStruct((B, 1, D_LRU), F32),
        ],
        scratch_shapes=[
            pltpu.VMEM((S, CONV_CARRY_ROWS + L, D_LRU), F32),
            pltpu.VMEM((S, 1, D_LRU), F32),
            pltpu.VMEM((D_LRU // LANES, S * L, LANES), F32),
            pltpu.VMEM((D_LRU // LANES, S * L, LANES), F32),
        ],
        compiler_params=pltpu.CompilerParams(
            dimension_semantics=("arbitrary", "arbitrary"), vmem_limit_bytes=VMEM_LIMIT_BYTES),
        name=name,
    )(x, mod, conv0, h0, g_pre, w_in_bf, conv_w, conv_b, w_gate_bf, b_gate, lam, g_lru)


def _prefix_ones(n):
    j = lax.broadcasted_iota(jnp.int32, (n, n), 0)
    s = lax.broadcasted_iota(jnp.int32, (n, n), 1)
    return jnp.where(j > s, -1.0, 0.0).astype(BF16)


def _stack_heads(x):
    lane = lax.broadcasted_iota(jnp.int32, x.shape, 1)
    zero = jnp.zeros_like(x)
    return jnp.concatenate([jnp.where(lane < SB_HEAD_DIM, x, zero),
                            jnp.where(lane >= SB_HEAD_DIM, x, zero)], axis=0)


def _pair_cols(p):
    return slice(p * HEAD_PAIR, (p + 1) * HEAD_PAIR)


def _sweep_tiles(q_pairs, tiles, state):
    accs, carries = state
    tq = accs[0].shape[0]
    pairs = range(N_PAIRS)
    lane = lax.broadcasted_iota(jnp.int32, (tq, HEAD_PAIR), 1)

    def scores(kt):
        return [lax.dot_general(q_pairs[p], kt[:, _pair_cols(p)], (((1,), (1,)), ((), ())),
                                preferred_element_type=F32) for p in pairs]

    def drop_terms(z, mask):
        log_beta, drop, stacked = [], [], []
        for p in pairs:
            sp = jnp.maximum(z[p], 0.0) + jnp.log(1.0 + jnp.exp2(jnp.abs(z[p]) * -LOG2_E))
            log_beta.append(z[p] - sp)
            if mask is not None:
                sp = jnp.where(mask, sp, 0.0)
            drop.append(sp)
            stacked.append(sp.astype(BF16))
        return log_beta, drop, jnp.concatenate(stacked, axis=0)

    def weigh(log_beta, log_keep_after, carries, mask, vt, accs):
        new_accs = []
        for p in pairs:
            w = jnp.exp(log_beta[p] + log_keep_after[2 * tq * p:2 * tq * (p + 1)] + carries[p])
            if mask is not None:
                w = jnp.where(mask, w, 0.0)
            pv = _dot(w.astype(BF16), vt[:, _pair_cols(p)])
            new_accs.append(accs[p] + jnp.where(lane < SB_HEAD_DIM, pv[:tq], pv[tq:]))
        return new_accs

    z = [scores(tile[0]) for tile in tiles]
    log_beta, log_keep_after, tile_carries = [], [], []
    for n, (_, _, neg_ones, mask, exists) in enumerate(tiles):
        lb, drop, stacked = drop_terms(z[n], mask)
        log_beta.append(lb)
        log_keep_after.append(_dot(stacked, neg_ones))
        if exists is not None:
            carries = tuple(jnp.where(exists, c, HIDDEN_CARRY) for c in carries)
        tile_carries.append(carries)
        carries = tuple(carries[p] - jnp.sum(drop[p], axis=-1, keepdims=True) for p in pairs)
    for n, (_, vt, _, mask, _) in enumerate(tiles):
        accs = weigh(log_beta[n], log_keep_after[n], tile_carries[n], mask, vt, accs)
    return tuple(accs), carries


def _attn_kernel(q_ref, kd_ref, vd_ref, kp_ref, vp_ref, ksrc_ref, vsrc_ref, o_ref,
                 kt_ref, vt_ref, carry_ref, sem, *, tq, n_before):
    b = pl.program_id(0)
    i = pl.program_id(1)
    before = i * tq if n_before is None else n_before

    q = q_ref[...]
    q_pairs = [_stack_heads(q[:, _pair_cols(p)]) for p in range(N_PAIRS)]

    state = (tuple(jnp.zeros((tq, HEAD_PAIR), F32) for _ in range(N_PAIRS)),
             tuple(jnp.zeros((2 * tq, 1), F32) for _ in range(N_PAIRS)))
    t_idx = lax.broadcasted_iota(jnp.int32, (tq, tq), 0)
    s_idx = lax.broadcasted_iota(jnp.int32, (tq, tq), 1)
    causal = s_idx < t_idx
    tri = _prefix_ones(KEY_TILE)
    own_tile = (kd_ref[...], vd_ref[...], _prefix_ones(tq), jnp.concatenate([causal, causal], axis=0),
                None)
    prev_tile = (kp_ref[...].astype(BF16), vp_ref[...].astype(BF16), tri, None,
                 i > 0 if n_before is None else None)
    accs, carries = _sweep_tiles(q_pairs, [own_tile, prev_tile], state)
    o_ref[...] = jnp.concatenate(accs, axis=1)
    n_tail = before // KEY_TILE - 1

    def live(carries):
        return jnp.max(functools.reduce(jnp.maximum, carries))

    top = live(carries)

    @pl.when(jnp.logical_and(n_tail > 0, top > LOG_WEIGHT_FLOOR))
    def _():
        for p in range(N_PAIRS):
            carry_ref[p] = carries[p]

        def cond(c):
            n, top = c
            return jnp.logical_and(n < n_tail, top > LOG_WEIGHT_FLOOR)

        def body(c):
            n, _ = c
            rows = pl.ds(pl.multiple_of(before - (n + 2) * KEY_TILE, KEY_TILE), KEY_TILE)
            copies = (pltpu.make_async_copy(ksrc_ref.at[b, rows, :], kt_ref, sem.at[0]),
                      pltpu.make_async_copy(vsrc_ref.at[b, rows, :], vt_ref, sem.at[1]))
            for cp in copies:
                cp.start()
            for cp in copies:
                cp.wait()
            tile = (kt_ref[...].astype(BF16), vt_ref[...].astype(BF16), tri, None, None)
            state = (tuple(o_ref[:, _pair_cols(p)] for p in range(N_PAIRS)),
                     tuple(carry_ref[p] for p in range(N_PAIRS)))
            accs, carries = _sweep_tiles(q_pairs, [tile], state)
            o_ref[...] = jnp.concatenate(accs, axis=1)
            for p in range(N_PAIRS):
                carry_ref[p] = carries[p]
            return n + 1, live(carries)

        lax.while_loop(cond, body, (jnp.int32(0), top))


def _attention(q, k_new, v_new, k_before, v_before, *, tq, name):
    B, T, _ = q.shape
    n_q = T // tq
    if k_before is None:
        assert tq == KEY_TILE
        k_before, v_before, n_before = k_new, v_new, None
        prev_idx = lambda b, i: (b, jnp.maximum(i - 1, 0), 0)
    else:
        assert n_q == 1
        n_before = k_before.shape[1]
        prev_idx = lambda b, i: (b, n_before // KEY_TILE - 1, 0)
    tile = lambda b, i: (b, i, 0)
    new_spec = pl.BlockSpec((None, tq, D_SB), tile)
    prev_spec = pl.BlockSpec((None, KEY_TILE, D_SB), prev_idx)
    hbm_spec = pl.BlockSpec(memory_space=pl.ANY)
    kernel = functools.partial(_attn_kernel, tq=tq, n_before=n_before)
    return pl.pallas_call(
        kernel,
        grid=(B, n_q),
        in_specs=[new_spec, new_spec, new_spec, prev_spec, prev_spec, hbm_spec, hbm_spec],
        out_specs=pl.BlockSpec((None, tq, D_SB), tile),
        out_shape=jax.ShapeDtypeStruct((B, T, D_SB), F32),
        scratch_shapes=[
            pltpu.VMEM((KEY_TILE, D_SB), k_before.dtype),
            pltpu.VMEM((KEY_TILE, D_SB), v_before.dtype),
            pltpu.VMEM((N_PAIRS, 2 * tq, 1), F32),
            pltpu.SemaphoreType.DMA((2,)),
        ],
        compiler_params=pltpu.CompilerParams(
            dimension_semantics=("arbitrary", "arbitrary"), vmem_limit_bytes=VMEM_LIMIT_BYTES),
        name=name,
    )(q, k_new, v_new, k_before, v_before, k_before, v_before)


def _back_kernel(x_ref, olru_ref, osb_ref, mod_ref, g_sb_ref, g_post_mix_ref, g_pre_mlp_ref,
                 g_post_mlp_ref, w_out_ref, w_up_ref, w_down_ref, y_ref, *, n_seq, n_rows, ff_chunk):
    S, L = n_seq, n_rows
    if S > 1:
        halves = [(slice(0, S // 2), slice(None)), (slice(S // 2, S), slice(None))]
        hs, hl = S // 2, L
    else:
        halves = [(slice(None), slice(0, L // 2)), (slice(None), slice(L // 2, L))]
        hs, hl = S, L // 2
    n_chunks = D_FF // ff_chunk

    def mod(idx, row):
        return mod_ref[idx[0], row:row + 1, :]

    def attn_out(idx):
        o_lru = olru_ref[idx[0], idx[1], :].reshape(hs * hl, D_LRU)
        o_sb = _rms(osb_ref[idx[0], idx[1], :], g_sb_ref[...]).reshape(hs * hl, D_SB).astype(BF16)
        return _dot(o_lru, w_out_ref[0:D_LRU, :]) + _dot(o_sb, w_out_ref[D_LRU:, :])

    def mix_residual(idx, o):
        x = x_ref[idx[0], idx[1], :] + mod(idx, 2) * _rms(o, g_post_mix_ref[...]).reshape(hs, hl, D_MODEL)
        h = _rms(x, g_pre_mlp_ref[...] * (1.0 + mod(idx, 4))) + mod(idx, 3)
        return x, h.reshape(hs * hl, D_MODEL).astype(BF16)

    def mlp_chunk(hb, f, c):
        cols = slice(c * ff_chunk, (c + 1) * ff_chunk)
        up = jnp.maximum(_dot(hb, w_up_ref[:, cols]), 0.0)
        return f + _dot((up * up).astype(BF16), w_down_ref[cols, :])

    def mlp_residual(idx, x, f):
        y_ref[idx[0], idx[1], :] = x + mod(idx, 5) * _rms(f, g_post_mlp_ref[...]).reshape(hs, hl, D_MODEL)

    first, second = halves
    zero = jnp.zeros((hs * hl, D_MODEL), F32)
    o1 = attn_out(first)
    o2 = attn_out(second)
    x1, hb1 = mix_residual(first, o1)
    f1 = mlp_chunk(hb1, zero, 0)
    x2, hb2 = mix_residual(second, o2)
    for c in range(1, n_chunks):
        f1 = mlp_chunk(hb1, f1, c)
    f2 = mlp_chunk(hb2, zero, 0)
    mlp_residual(first, x1, f1)
    for c in range(1, n_chunks):
        f2 = mlp_chunk(hb2, f2, c)
    mlp_residual(second, x2, f2)


def _back(x, o_lru, o_sb, mod, g_sb, g_post_mix, g_pre_mlp, g_post_mlp, w_out_bf, w_up_bf, w_down_bf,
          *, n_seq, n_rows, name, ff_chunk=1024):
    B, T, _ = x.shape
    S, L = n_seq, n_rows
    const2 = lambda b, t: (0, 0)
    tile3 = lambda b, t: (b, t, 0)
    seq3 = lambda b, t: (b, 0, 0)
    kernel = functools.partial(_back_kernel, n_seq=S, n_rows=L, ff_chunk=ff_chunk)
    return pl.pallas_call(
        kernel,
        grid=(B // S, T // L),
        in_specs=[
            pl.BlockSpec((S, L, D_MODEL), tile3),
            pl.BlockSpec((S, L, D_LRU), tile3),
            pl.BlockSpec((S, L, D_SB), tile3),
            pl.BlockSpec((S, N_MOD, D_MODEL), seq3),
            pl.BlockSpec((1, D_SB), const2),
            pl.BlockSpec((1, D_MODEL), const2),
            pl.BlockSpec((1, D_MODEL), const2),
            pl.BlockSpec((1, D_MODEL), const2),
            pl.BlockSpec((D_MODEL, D_MODEL), const2),
            pl.BlockSpec((D_MODEL, D_FF), const2),
            pl.BlockSpec((D_FF, D_MODEL), const2),
        ],
        out_specs=pl.BlockSpec((S, L, D_MODEL), tile3),
        out_shape=jax.ShapeDtypeStruct((B, T, D_MODEL), F32),
        compiler_params=pltpu.CompilerParams(
            dimension_semantics=("arbitrary", "arbitrary"), vmem_limit_bytes=VMEM_LIMIT_BYTES),
        name=name,
    )(x, o_lru, o_sb, mod, g_sb, g_post_mix, g_pre_mlp, g_post_mlp, w_out_bf, w_up_bf, w_down_bf)


FRONT_ROWS = 1024
BACK_ROWS = 1024
SAMPLE_SEQS_PER_STEP = 8


def _block_diag(w):
    n, bi, bj = w.shape
    eye = jnp.eye(n, dtype=w.dtype)
    return (eye[:, None, :, None] * w[:, :, None, :]).reshape(n * bi, n * bj)


def _pad_conv_state(buf):
    return jnp.pad(buf, ((0, 0), (CONV_CARRY_ROWS - (CONV_WIDTH - 1), 0), (0, 0)))


def kernel(x_prompt, x_sample, c_prompt, c_sample, cache_conv, state_lru, cache_k, cache_v, w_ada, b_ada, g_pre_mix, g_post_mix, g_pre_mlp, g_post_mlp, w_in, conv_w, conv_b, w_rg_a, b_rg_a, w_rg_x, b_rg_x, lru_lambda, g_lru_out, g_sb_out, w_out, w_up, w_down):
    depth = w_ada.shape[0]
    bp, seq, _ = x_prompt.shape
    bs, dec_seq, _ = x_sample.shape
    assert depth == 1 and bp == 1, "single-layer, single-prompt configuration"
    past_len = cache_k.shape[2]
    l = 0

    row = lambda a: a.reshape(1, -1)
    q_cols = jnp.arange(D_IN) // D_SB == 2 * D_LRU // D_SB
    w_in_bf = (w_in[l] * jnp.where(q_cols, SB_HEAD_DIM ** -0.5, 1.0)).astype(BF16)
    w_gate_bf = jnp.concatenate([_block_diag(w_rg_a[l]), _block_diag(w_rg_x[l])], axis=1).astype(BF16)
    b_gate = jnp.concatenate([b_rg_a[l].reshape(-1), b_rg_x[l].reshape(-1)]).reshape(1, -1)
    w_out_bf = w_out[l].astype(BF16)
    w_up_bf = w_up[l].astype(BF16)
    w_down_bf = w_down[l].astype(BF16)
    front_w = (row(g_pre_mix[l]), w_in_bf, conv_w[l], row(conv_b[l]), w_gate_bf, b_gate,
               row(lru_lambda[l]), row(g_lru_out[l]))
    back_w = (row(g_sb_out[l]), row(g_post_mix[l]), row(g_pre_mlp[l]), row(g_post_mlp[l]),
              w_out_bf, w_up_bf, w_down_bf)

    mod = _adaln(jnp.concatenate([c_prompt, c_sample], axis=0), w_ada[l], b_ada[l])
    mod = mod.reshape(bp + bs, N_MOD, D_MODEL)
    mod_p, mod_s = mod[:bp], mod[bp:]

    zeros_conv = jnp.zeros((bp, CONV_CARRY_ROWS, D_LRU), F32)
    zeros_h = jnp.zeros((bp, 1, D_LRU), F32)
    olru_p, q_p, kbf_p, vbf_p, k_p, v_p, conv_p, h_p = _front(
        x_prompt, mod_p, zeros_conv, zeros_h, *front_w, n_seq=bp, n_rows=FRONT_ROWS,
        name="front_prompt")
    osb_p = _attention(q_p, kbf_p, vbf_p, None, None, tq=KEY_TILE, name="attn_prompt")
    y_p = _back(x_prompt, olru_p, osb_p, mod_p, *back_w, n_seq=1, n_rows=BACK_ROWS, name="back_prompt")

    olru_s, q_s, kbf_s, vbf_s, k_s, v_s, conv_s, h_s = _front(
        x_sample, mod_s, _pad_conv_state(cache_conv[l]), state_lru[l].reshape(bs, 1, D_LRU),
        *front_w, n_seq=SAMPLE_SEQS_PER_STEP, n_rows=dec_seq, name="front_sample")
    osb_s = _attention(q_s, kbf_s, vbf_s, cache_k[l].reshape(bs, past_len, D_SB),
                       cache_v[l].reshape(bs, past_len, D_SB), tq=dec_seq, name="attn_sample")
    y_s = _back(x_sample, olru_s, osb_s, mod_s, *back_w, n_seq=SAMPLE_SEQS_PER_STEP, n_rows=dec_seq,
                name="back_sample")

    heads = lambda a: a.reshape(1, a.shape[0], a.shape[1], N_SB_HEADS, SB_HEAD_DIM)
    return (y_p, y_s, conv_p[None], h_p.reshape(1, bp, D_LRU), heads(k_p), heads(v_p),
            conv_s[None], h_s.reshape(1, bs, D_LRU), heads(k_s), heads(v_s))
```

```python
import functools

import jax
import jax.numpy as jnp
from jax import lax
from jax.experimental import pallas as pl
from jax.experimental.pallas import tpu as pltpu

D_MODEL = 1024
D_LRU = 512
D_SB = 512
N_SB_HEADS = 8
SB_HEAD_DIM = 64
HEAD_PAIR = 2 * SB_HEAD_DIM
N_PAIRS = N_SB_HEADS // 2
CONV_WIDTH = 4
CONV_CARRY_ROWS = 8
RG_C = 8.0
D_FF = 4 * D_MODEL
D_IN = 2 * D_LRU + 3 * D_SB
N_MOD = 6
EPS = 1e-6
LOG2_E = 1.4426950408889634
F32_TINY = 1.1754943508222875e-38

KEY_TILE = 256
LOG_WEIGHT_FLOOR = -110.0
HIDDEN_CARRY = -1e30

VMEM_LIMIT_BYTES = 56 * 1024 * 1024

BF16 = jnp.bfloat16
F32 = jnp.float32


def _rms(x, g):
    return x * lax.rsqrt(jnp.mean(x * x, axis=-1, keepdims=True) + EPS) * g


def _dot(a, b):
    return jnp.dot(a, b, preferred_element_type=F32)


def _adaln_kernel(c_ref, w_ref, b_ref, o_ref):
    c = c_ref[...]
    s = (c * jax.nn.sigmoid(c)).astype(BF16)
    o_ref[...] = _dot(s, w_ref[...].astype(BF16)) + b_ref[...]


def _adaln(c_all, w_ada, b_ada, *, tn=1536):
    n_rows = c_all.shape[0]
    n_out = w_ada.shape[1]
    return pl.pallas_call(
        _adaln_kernel,
        grid=(n_out // tn,),
        in_specs=[
            pl.BlockSpec((n_rows, D_MODEL), lambda j: (0, 0)),
            pl.BlockSpec((D_MODEL, tn), lambda j: (0, j)),
            pl.BlockSpec((1, tn), lambda j: (0, j)),
        ],
        out_specs=pl.BlockSpec((n_rows, tn), lambda j: (0, j)),
        out_shape=jax.ShapeDtypeStruct((n_rows, n_out), F32),
        compiler_params=pltpu.CompilerParams(
            dimension_semantics=("arbitrary",), vmem_limit_bytes=VMEM_LIMIT_BYTES),
        name="adaln",
    )(c_all, w_ada, b_ada.reshape(1, n_out))


def _shift_rows(x, s, fill, row):
    return jnp.where(row >= s, pltpu.roll(x, s, axis=1), fill)


SUBLANES = 8
LANES = 128


def _lru_scan(a, u, h_in, a_scr, b_scr):
    S, L, C = a.shape
    G = L // SUBLANES
    ag = a.reshape(S * G, SUBLANES, C)
    bg = u.reshape(S * G, SUBLANES, C)
    sub = lax.broadcasted_iota(jnp.int32, ag.shape, 1)
    step = 1
    while step < SUBLANES:
        bg = ag * _shift_rows(bg, step, 0.0, sub) + bg
        ag = ag * _shift_rows(ag, step, 1.0, sub)
        step *= 2
    last_rows = pl.ds(SUBLANES - 1, S * G, stride=SUBLANES)

    def group_ends(x, scr):
        x = x.reshape(S * L, C)
        for c in range(C // LANES):
            scr[c] = x[:, c * LANES:(c + 1) * LANES]
        ends = [scr[c, last_rows, :] for c in range(C // LANES)]
        return jnp.concatenate(ends, axis=1).reshape(S, G, C)

    a_sum = group_ends(ag, a_scr)
    b_sum = group_ends(bg, b_scr)
    grp = lax.broadcasted_iota(jnp.int32, a_sum.shape, 1)
    step = 1
    while step < G:
        b_sum = a_sum * _shift_rows(b_sum, step, 0.0, grp) + b_sum
        a_sum = a_sum * _shift_rows(a_sum, step, 1.0, grp)
        step *= 2
    h_end = b_sum + a_sum * h_in
    h_enter = _shift_rows(h_end, 1, h_in, grp)
    h = bg + ag * h_enter.reshape(S * G, 1, C)
    return h.reshape(S, L, C), h_end[:, G - 1:G, :]


def _front_kernel(x_ref, mod_ref, conv0_ref, h0_ref, g_pre_ref, w_in_ref, conv_w_ref, conv_b_ref,
                  w_gate_ref, b_gate_ref, lam_ref, g_lru_ref,
                  olru_ref, q_ref, kbf_ref, vbf_ref, k_ref, v_ref, conv_out_ref, h_out_ref,
                  ext_ref, hc_ref, a_scr, b_scr, *, n_seq, n_rows):
    S, L = n_seq, n_rows

    @pl.when(pl.program_id(1) == 0)
    def _():
        ext_ref[:, 0:CONV_CARRY_ROWS, :] = conv0_ref[...]
        hc_ref[...] = h0_ref[...]

    x = x_ref[...]
    shift = mod_ref[:, 0:1, :]
    scale = mod_ref[:, 1:2, :]
    h = _rms(x, g_pre_ref[...] * (1.0 + scale)) + shift
    hb = h.reshape(S * L, D_MODEL).astype(BF16)

    def project(first_col, n_cols):
        return _dot(hb, w_in_ref[:, first_col:first_col + n_cols])

    lru_in = project(0, 2 * D_LRU)
    xl = lru_in[:, 0:D_LRU].reshape(S, L, D_LRU)
    gl = lru_in[:, D_LRU:].reshape(S, L, D_LRU)

    k = project(2 * D_LRU + D_SB, D_SB)
    k_ref[...] = k.reshape(S, L, D_SB)
    kbf_ref[...] = k.astype(BF16).reshape(S, L, D_SB)

    ext_ref[:, CONV_CARRY_ROWS:, :] = xl
    cw = conv_w_ref[...]
    xc = conv_b_ref[...]
    for j in range(CONV_WIDTH - 1):
        start = CONV_CARRY_ROWS - (CONV_WIDTH - 1 - j)
        xc = xc + ext_ref[:, start:start + L, :] * cw[j:j + 1, :]
    xc = xc + xl * cw[CONV_WIDTH - 1:CONV_WIDTH, :]
    conv_out_ref[...] = ext_ref[:, L + CONV_CARRY_ROWS - (CONV_WIDTH - 1):, :]
    ext_ref[:, 0:CONV_CARRY_ROWS, :] = xl[:, L - CONV_CARRY_ROWS:, :]

    gates = _dot(xc.reshape(S * L, D_LRU).astype(BF16), w_gate_ref[...]) + b_gate_ref[...]
    v = project(2 * D_LRU + 2 * D_SB, D_SB)
    v_ref[...] = v.reshape(S, L, D_SB)
    vbf_ref[...] = v.astype(BF16).reshape(S, L, D_SB)

    r = jax.nn.sigmoid(gates[:, 0:D_LRU]).reshape(S, L, D_LRU)
    i = jax.nn.sigmoid(gates[:, D_LRU:]).reshape(S, L, D_LRU)
    neg_lam = -lam_ref[...]
    decay = RG_C * (jnp.maximum(neg_lam, 0.0) + jnp.log1p(jnp.exp(-jnp.abs(neg_lam))))
    neg_log_a = r * decay
    a = jnp.exp2(r * (decay * -LOG2_E))
    m = jnp.tanh(neg_log_a) * (1.0 + a * a)
    u = (m * lax.rsqrt(jnp.maximum(m, F32_TINY))) * (i * xc)

    q_ref[...] = project(2 * D_LRU, D_SB).astype(BF16).reshape(S, L, D_SB)

    hl, h_last = _lru_scan(a, u, hc_ref[...], a_scr, b_scr)
    hc_ref[...] = h_last
    h_out_ref[...] = h_last

    o = hl * jax.nn.gelu(gl)
    olru_ref[...] = _rms(o, g_lru_ref[...]).astype(BF16)


def _front(x, mod, conv0, h0, g_pre, w_in_bf, conv_w, conv_b, w_gate_bf, b_gate, lam, g_lru,
           *, n_seq, n_rows, name):
    B, T, _ = x.shape
    S, L = n_seq, n_rows
    const2 = lambda s, t: (0, 0)
    seq3 = lambda s, t: (s, 0, 0)
    tile3 = lambda s, t: (s, t, 0)
    kernel = functools.partial(_front_kernel, n_seq=S, n_rows=L)
    return pl.pallas_call(
        kernel,
        grid=(B // S, T // L),
        in_specs=[
            pl.BlockSpec((S, L, D_MODEL), tile3),
            pl.BlockSpec((S, N_MOD, D_MODEL), seq3),
            pl.BlockSpec((S, CONV_CARRY_ROWS, D_LRU), seq3),
            pl.BlockSpec((S, 1, D_LRU), seq3),
            pl.BlockSpec((1, D_MODEL), const2),
            pl.BlockSpec((D_MODEL, D_IN), const2),
            pl.BlockSpec((CONV_WIDTH, D_LRU), const2),
            pl.BlockSpec((1, D_LRU), const2),
            pl.BlockSpec((D_LRU, 2 * D_LRU), const2),
            pl.BlockSpec((1, 2 * D_LRU), const2),
            pl.BlockSpec((1, D_LRU), const2),
            pl.BlockSpec((1, D_LRU), const2),
        ],
        out_specs=[
            pl.BlockSpec((S, L, D_LRU), tile3),
            pl.BlockSpec((S, L, D_SB), tile3),
            pl.BlockSpec((S, L, D_SB), tile3),
            pl.BlockSpec((S, L, D_SB), tile3),
            pl.BlockSpec((S, L, D_SB), tile3),
            pl.BlockSpec((S, L, D_SB), tile3),
            pl.BlockSpec((S, CONV_WIDTH - 1, D_LRU), seq3),
            pl.BlockSpec((S, 1, D_LRU), seq3),
        ],
        out_shape=[
            jax.ShapeDtypeStruct((B, T, D_LRU), BF16),
            jax.ShapeDtypeStruct((B, T, D_SB), BF16),
            jax.ShapeDtypeStruct((B, T, D_SB), BF16),
            jax.ShapeDtypeStruct((B, T, D_SB), BF16),
            jax.ShapeDtypeStruct((B, T, D_SB), F32),
            jax.ShapeDtypeStruct((B, T, D_SB), F32),
            jax.ShapeDtypeStruct((B, CONV_WIDTH - 1, D_LRU), F32),
            jax.ShapeDtypeStruct((B, 1, D_LRU), F32),
        ],
        scratch_shapes=[
            pltpu.VMEM((S, CONV_CARRY_ROWS + L, D_LRU), F32),
            pltpu.VMEM((S, 1, D_LRU), F32),
            pltpu.VMEM((D_LRU // LANES, S * L, LANES), F32),
            pltpu.VMEM((D_LRU // LANES, S * L, LANES), F32),
        ],
        compiler_params=pltpu.CompilerParams(
            dimension_semantics=("arbitrary", "arbitrary"), vmem_limit_bytes=VMEM_LIMIT_BYTES),
        name=name,
    )(x, mod, conv0, h0, g_pre, w_in_bf, conv_w, conv_b, w_gate_bf, b_gate, lam, g_lru)


def _prefix_ones(n):
    j = lax.broadcasted_iota(jnp.int32, (n, n), 0)
    s = lax.broadcasted_iota(jnp.int32, (n, n), 1)
    return jnp.where(j > s, -1.0, 0.0).astype(BF16)


def _stack_heads(x):
    lane = lax.broadcasted_iota(jnp.int32, x.shape, 1)
    zero = jnp.zeros_like(x)
    return jnp.concatenate([jnp.where(lane < SB_HEAD_DIM, x, zero),
                            jnp.where(lane >= SB_HEAD_DIM, x, zero)], axis=0)


def _pair_cols(p):
    return slice(p * HEAD_PAIR, (p + 1) * HEAD_PAIR)


def _sweep_tiles(q_pairs, tiles, state):
    accs, carries = state
    tq = accs[0].shape[0]
    pairs = range(N_PAIRS)
    lane = lax.broadcasted_iota(jnp.int32, (tq, HEAD_PAIR), 1)

    def scores(kt):
        return [lax.dot_general(q_pairs[p], kt[:, _pair_cols(p)], (((1,), (1,)), ((), ())),
                                preferred_element_type=F32) for p in pairs]

    def drop_terms(z, mask):
        log_beta, drop, stacked = [], [], []
        for p in pairs:
            sp = jnp.maximum(z[p], 0.0) + jnp.log(1.0 + jnp.exp2(jnp.abs(z[p]) * -LOG2_E))
            log_beta.append(z[p] - sp)
            if mask is not None:
                sp = jnp.where(mask, sp, 0.0)
            drop.append(sp)
            stacked.append(sp.astype(BF16))
        return log_beta, drop, jnp.concatenate(stacked, axis=0)

    def weigh(log_beta, log_keep_after, carries, mask, vt, accs):
        new_accs = []
        for p in pairs:
            w = jnp.exp(log_beta[p] + log_keep_after[2 * tq * p:2 * tq * (p + 1)] + carries[p])
            if mask is not None:
                w = jnp.where(mask, w, 0.0)
            pv = _dot(w.astype(BF16), vt[:, _pair_cols(p)])
            new_accs.append(accs[p] + jnp.where(lane < SB_HEAD_DIM, pv[:tq], pv[tq:]))
        return new_accs

    z = [scores(tile[0]) for tile in tiles]
    log_beta, log_keep_after, tile_carries = [], [], []
    for n, (_, _, neg_ones, mask, exists) in enumerate(tiles):
        lb, drop, stacked = drop_terms(z[n], mask)
        log_beta.append(lb)
        log_keep_after.append(_dot(stacked, neg_ones))
        if exists is not None:
            carries = tuple(jnp.where(exists, c, HIDDEN_CARRY) for c in carries)
        tile_carries.append(carries)
        carries = tuple(carries[p] - jnp.sum(drop[p], axis=-1, keepdims=True) for p in pairs)
    for n, (_, vt, _, mask, _) in enumerate(tiles):
        accs = weigh(log_beta[n], log_keep_after[n], tile_carries[n], mask, vt, accs)
    return tuple(accs), carries


def _attn_kernel(q_ref, kd_ref, vd_ref, kp_ref, vp_ref, ksrc_ref, vsrc_ref, o_ref,
                 kt_ref, vt_ref, prev_k_ref, prev_v_ref, carry_ref, sem, *, tq, n_before):
    b = pl.program_id(0)
    i = pl.program_id(1)
    before = i * tq if n_before is None else n_before

    q = q_ref[...]
    q_pairs = [_stack_heads(q[:, _pair_cols(p)]) for p in range(N_PAIRS)]

    state = (tuple(jnp.zeros((tq, HEAD_PAIR), F32) for _ in range(N_PAIRS)),
             tuple(jnp.zeros((2 * tq, 1), F32) for _ in range(N_PAIRS)))
    t_idx = lax.broadcasted_iota(jnp.int32, (tq, tq), 0)
    s_idx = lax.broadcasted_iota(jnp.int32, (tq, tq), 1)
    causal = s_idx < t_idx
    tri = _prefix_ones(KEY_TILE)
    own_tile = (kd_ref[...], vd_ref[...], _prefix_ones(tq), jnp.concatenate([causal, causal], axis=0),
                None)
    if n_before is None:
        @pl.when(i == 0)
        def _():
            prev_k_ref[...] = jnp.zeros(prev_k_ref.shape, BF16)
            prev_v_ref[...] = jnp.zeros(prev_v_ref.shape, BF16)

        prev_tile = (prev_k_ref[...], prev_v_ref[...], tri, None, i > 0)
    else:
        prev_tile = (kp_ref[...].astype(BF16), vp_ref[...].astype(BF16), tri, None, None)
    accs, carries = _sweep_tiles(q_pairs, [own_tile, prev_tile], state)
    o_ref[...] = jnp.concatenate(accs, axis=1)
    if n_before is None:
        prev_k_ref[...] = kd_ref[...]
        prev_v_ref[...] = vd_ref[...]
    n_tail = before // KEY_TILE - 1

    def live(carries):
        return jnp.max(functools.reduce(jnp.maximum, carries))

    top = live(carries)

    @pl.when(jnp.logical_and(n_tail > 0, top > LOG_WEIGHT_FLOOR))
    def _():
        for p in range(N_PAIRS):
            carry_ref[p] = carries[p]

        def cond(c):
            n, top = c
            return jnp.logical_and(n < n_tail, top > LOG_WEIGHT_FLOOR)

        def body(c):
            n, _ = c
            rows = pl.ds(pl.multiple_of(before - (n + 2) * KEY_TILE, KEY_TILE), KEY_TILE)
            copies = (pltpu.make_async_copy(ksrc_ref.at[b, rows, :], kt_ref, sem.at[0]),
                      pltpu.make_async_copy(vsrc_ref.at[b, rows, :], vt_ref, sem.at[1]))
            for cp in copies:
                cp.start()
            for cp in copies:
                cp.wait()
            tile = (kt_ref[...].astype(BF16), vt_ref[...].astype(BF16), tri, None, None)
            state = (tuple(o_ref[:, _pair_cols(p)] for p in range(N_PAIRS)),
                     tuple(carry_ref[p] for p in range(N_PAIRS)))
            accs, carries = _sweep_tiles(q_pairs, [tile], state)
            o_ref[...] = jnp.concatenate(accs, axis=1)
            for p in range(N_PAIRS):
                carry_ref[p] = carries[p]
            return n + 1, live(carries)

        lax.while_loop(cond, body, (jnp.int32(0), top))


def _attention(q, k_new, v_new, k_before, v_before, *, tq, name):
    B, T, _ = q.shape
    n_q = T // tq
    hbm_spec = pl.BlockSpec(memory_space=pl.ANY)
    if k_before is None:
        assert tq == KEY_TILE
        k_before, v_before, n_before = k_new, v_new, None
        prev_spec = hbm_spec
    else:
        assert n_q == 1
        n_before = k_before.shape[1]
        prev_spec = pl.BlockSpec((None, KEY_TILE, D_SB), lambda b, i: (b, n_before // KEY_TILE - 1, 0))
    tile = lambda b, i: (b, i, 0)
    new_spec = pl.BlockSpec((None, tq, D_SB), tile)
    kernel = functools.partial(_attn_kernel, tq=tq, n_before=n_before)
    return pl.pallas_call(
        kernel,
        grid=(B, n_q),
        in_specs=[new_spec, new_spec, new_spec, prev_spec, prev_spec, hbm_spec, hbm_spec],
        out_specs=pl.BlockSpec((None, tq, D_SB), tile),
        out_shape=jax.ShapeDtypeStruct((B, T, D_SB), F32),
        scratch_shapes=[
            pltpu.VMEM((KEY_TILE, D_SB), k_before.dtype),
            pltpu.VMEM((KEY_TILE, D_SB), v_before.dtype),
            pltpu.VMEM((KEY_TILE, D_SB), BF16),
            pltpu.VMEM((KEY_TILE, D_SB), BF16),
            pltpu.VMEM((N_PAIRS, 2 * tq, 1), F32),
            pltpu.SemaphoreType.DMA((2,)),
        ],
        compiler_params=pltpu.CompilerParams(
            dimension_semantics=("arbitrary", "arbitrary"), vmem_limit_bytes=VMEM_LIMIT_BYTES),
        name=name,
    )(q, k_new, v_new, k_before, v_before, k_before, v_before)


def _back_kernel(x_ref, olru_ref, osb_ref, mod_ref, g_sb_ref, g_post_mix_ref, g_pre_mlp_ref,
                 g_post_mlp_ref, w_out_ref, w_up_ref, w_down_ref, y_ref, *, n_seq, n_rows, ff_chunk):
    S, L = n_seq, n_rows
    if S > 1:
        halves = [(slice(0, S // 2), slice(None)), (slice(S // 2, S), slice(None))]
        hs, hl = S // 2, L
    else:
        halves = [(slice(None), slice(0, L // 2)), (slice(None), slice(L // 2, L))]
        hs, hl = S, L // 2
    n_chunks = D_FF // ff_chunk

    def mod(idx, row):
        return mod_ref[idx[0], row:row + 1, :]

    def attn_out(idx):
        o_lru = olru_ref[idx[0], idx[1], :].reshape(hs * hl, D_LRU)
        o_sb = _rms(osb_ref[idx[0], idx[1], :], g_sb_ref[...]).reshape(hs * hl, D_SB).astype(BF16)
        return _dot(o_lru, w_out_ref[0:D_LRU, :]) + _dot(o_sb, w_out_ref[D_LRU:, :])

    def mix_residual(idx, o):
        x = x_ref[idx[0], idx[1], :] + mod(idx, 2) * _rms(o, g_post_mix_ref[...]).reshape(hs, hl, D_MODEL)
        h = _rms(x, g_pre_mlp_ref[...] * (1.0 + mod(idx, 4))) + mod(idx, 3)
        return x, h.reshape(hs * hl, D_MODEL).astype(BF16)

    def mlp_chunk(hb, f, c):
        cols = slice(c * ff_chunk, (c + 1) * ff_chunk)
        up = jnp.maximum(_dot(hb, w_up_ref[:, cols]), 0.0)
        return f + _dot((up * up).astype(BF16), w_down_ref[cols, :])

    def mlp_residual(idx, x, f):
        y_ref[idx[0], idx[1], :] = x + mod(idx, 5) * _rms(f, g_post_mlp_ref[...]).reshape(hs, hl, D_MODEL)

    first, second = halves
    zero = jnp.zeros((hs * hl, D_MODEL), F32)
    o1 = attn_out(first)
    o2 = attn_out(second)
    x1, hb1 = mix_residual(first, o1)
    f1 = mlp_chunk(hb1, zero, 0)
    x2, hb2 = mix_residual(second, o2)
    for c in range(1, n_chunks):
        f1 = mlp_chunk(hb1, f1, c)
    f2 = mlp_chunk(hb2, zero, 0)
    mlp_residual(first, x1, f1)
    for c in range(1, n_chunks):
        f2 = mlp_chunk(hb2, f2, c)
    mlp_residual(second, x2, f2)


def _back(x, o_lru, o_sb, mod, g_sb, g_post_mix, g_pre_mlp, g_post_mlp, w_out_bf, w_up_bf, w_down_bf,
          *, n_seq, n_rows, name, ff_chunk=1024):
    B, T, _ = x.shape
    S, L = n_seq, n_rows
    const2 = lambda b, t: (0, 0)
    tile3 = lambda b, t: (b, t, 0)
    seq3 = lambda b, t: (b, 0, 0)
    kernel = functools.partial(_back_kernel, n_seq=S, n_rows=L, ff_chunk=ff_chunk)
    return pl.pallas_call(
        kernel,
        grid=(B // S, T // L),
        in_specs=[
            pl.BlockSpec((S, L, D_MODEL), tile3),
            pl.BlockSpec((S, L, D_LRU), tile3),
            pl.BlockSpec((S, L, D_SB), tile3),
            pl.BlockSpec((S, N_MOD, D_MODEL), seq3),
            pl.BlockSpec((1, D_SB), const2),
            pl.BlockSpec((1, D_MODEL), const2),
            pl.BlockSpec((1, D_MODEL), const2),
            pl.BlockSpec((1, D_MODEL), const2),
            pl.BlockSpec((D_MODEL, D_MODEL), const2),
            pl.BlockSpec((D_MODEL, D_FF), const2),
            pl.BlockSpec((D_FF, D_MODEL), const2),
        ],
        out_specs=pl.BlockSpec((S, L, D_MODEL), tile3),
        out_shape=jax.ShapeDtypeStruct((B, T, D_MODEL), F32),
        compiler_params=pltpu.CompilerParams(
            dimension_semantics=("arbitrary", "arbitrary"), vmem_limit_bytes=VMEM_LIMIT_BYTES),
        name=name,
    )(x, o_lru, o_sb, mod, g_sb, g_post_mix, g_pre_mlp, g_post_mlp, w_out_bf, w_up_bf, w_down_bf)


FRONT_ROWS = 1024
BACK_ROWS = 1024
SAMPLE_SEQS_PER_STEP = 8


def _block_diag(w):
    n, bi, bj = w.shape
    eye = jnp.eye(n, dtype=w.dtype)
    return (eye[:, None, :, None] * w[:, :, None, :]).reshape(n * bi, n * bj)


def _pad_conv_state(buf):
    return jnp.pad(buf, ((0, 0), (CONV_CARRY_ROWS - (CONV_WIDTH - 1), 0), (0, 0)))


def kernel(x_prompt, x_sample, c_prompt, c_sample, cache_conv, state_lru, cache_k, cache_v, w_ada, b_ada, g_pre_mix, g_post_mix, g_pre_mlp, g_post_mlp, w_in, conv_w, conv_b, w_rg_a, b_rg_a, w_rg_x, b_rg_x, lru_lambda, g_lru_out, g_sb_out, w_out, w_up, w_down):
    depth = w_ada.shape[0]
    bp, seq, _ = x_prompt.shape
    bs, dec_seq, _ = x_sample.shape
    assert depth == 1 and bp == 1, "single-layer, single-prompt configuration"
    past_len = cache_k.shape[2]
    l = 0

    row = lambda a: a.reshape(1, -1)
    q_cols = jnp.arange(D_IN) // D_SB == 2 * D_LRU // D_SB
    w_in_bf = (w_in[l] * jnp.where(q_cols, SB_HEAD_DIM ** -0.5, 1.0)).astype(BF16)
    w_gate_bf = jnp.concatenate([_block_diag(w_rg_a[l]), _block_diag(w_rg_x[l])], axis=1).astype(BF16)
    b_gate = jnp.concatenate([b_rg_a[l].reshape(-1), b_rg_x[l].reshape(-1)]).reshape(1, -1)
    w_out_bf = w_out[l].astype(BF16)
    w_up_bf = w_up[l].astype(BF16)
    w_down_bf = w_down[l].astype(BF16)
    front_w = (row(g_pre_mix[l]), w_in_bf, conv_w[l], row(conv_b[l]), w_gate_bf, b_gate,
               row(lru_lambda[l]), row(g_lru_out[l]))
    back_w = (row(g_sb_out[l]), row(g_post_mix[l]), row(g_pre_mlp[l]), row(g_post_mlp[l]),
              w_out_bf, w_up_bf, w_down_bf)

    mod = _adaln(jnp.concatenate([c_prompt, c_sample], axis=0), w_ada[l], b_ada[l])
    mod = mod.reshape(bp + bs, N_MOD, D_MODEL)
    mod_p, mod_s = mod[:bp], mod[bp:]

    zeros_conv = jnp.zeros((bp, CONV_CARRY_ROWS, D_LRU), F32)
    zeros_h = jnp.zeros((bp, 1, D_LRU), F32)
    olru_p, q_p, kbf_p, vbf_p, k_p, v_p, conv_p, h_p = _front(
        x_prompt, mod_p, zeros_conv, zeros_h, *front_w, n_seq=bp, n_rows=FRONT_ROWS,
        name="front_prompt")
    osb_p = _attention(q_p, kbf_p, vbf_p, None, None, tq=KEY_TILE, name="attn_prompt")
    y_p = _back(x_prompt, olru_p, osb_p, mod_p, *back_w, n_seq=1, n_rows=BACK_ROWS, name="back_prompt")

    olru_s, q_s, kbf_s, vbf_s, k_s, v_s, conv_s, h_s = _front(
        x_sample, mod_s, _pad_conv_state(cache_conv[l]), state_lru[l].reshape(bs, 1, D_LRU),
        *front_w, n_seq=SAMPLE_SEQS_PER_STEP, n_rows=dec_seq, name="front_sample")
    osb_s = _attention(q_s, kbf_s, vbf_s, cache_k[l].reshape(bs, past_len, D_SB),
                       cache_v[l].reshape(bs, past_len, D_SB), tq=dec_seq, name="attn_sample")
    y_s = _back(x_sample, olru_s, osb_s, mod_s, *back_w, n_seq=SAMPLE_SEQS_PER_STEP, n_rows=dec_seq,
                name="back_sample")

    heads = lambda a: a.reshape(1, a.shape[0], a.shape[1], N_SB_HEADS, SB_HEAD_DIM)
    return (y_p, y_s, conv_p[None], h_p.reshape(1, bp, D_LRU), heads(k_p), heads(v_p),
            conv_s[None], h_s.reshape(1, bs, D_LRU), heads(k_s), heads(v_s))
```

```python
import functools

import jax
import jax.numpy as jnp
from jax import lax
from jax.experimental import pallas as pl
from jax.experimental.pallas import tpu as pltpu

D_MODEL = 1024
D_LRU = 512
D_SB = 512
N_SB_HEADS = 8
SB_HEAD_DIM = 64
HEAD_PAIR = 2 * SB_HEAD_DIM
N_PAIRS = N_SB_HEADS // 2
CONV_WIDTH = 4
CONV_CARRY_ROWS = 8
RG_C = 8.0
D_FF = 4 * D_MODEL
D_IN = 2 * D_LRU + 3 * D_SB
N_MOD = 6
EPS = 1e-6
LOG2_E = 1.4426950408889634
F32_TINY = 1.1754943508222875e-38

KEY_TILE = 256
LOG_WEIGHT_FLOOR = -110.0
HIDDEN_CARRY = -1e30

VMEM_LIMIT_BYTES = 56 * 1024 * 1024

BF16 = jnp.bfloat16
F32 = jnp.float32


def _rms(x, g):
    return x * lax.rsqrt(jnp.mean(x * x, axis=-1, keepdims=True) + EPS) * g


def _dot(a, b):
    return jnp.dot(a, b, preferred_element_type=F32)


def _adaln_kernel(c_ref, w_ref, b_ref, o_ref):
    c = c_ref[...]
    s = (c * jax.nn.sigmoid(c)).astype(BF16)
    o_ref[...] = _dot(s, w_ref[...].astype(BF16)) + b_ref[...]


def _adaln(c_all, w_ada, b_ada, *, tn=1536):
    n_rows = c_all.shape[0]
    n_out = w_ada.shape[1]
    return pl.pallas_call(
        _adaln_kernel,
        grid=(n_out // tn,),
        in_specs=[
            pl.BlockSpec((n_rows, D_MODEL), lambda j: (0, 0)),
            pl.BlockSpec((D_MODEL, tn), lambda j: (0, j)),
            pl.BlockSpec((1, tn), lambda j: (0, j)),
        ],
        out_specs=pl.BlockSpec((n_rows, tn), lambda j: (0, j)),
        out_shape=jax.ShapeDtypeStruct((n_rows, n_out), F32),
        compiler_params=pltpu.CompilerParams(
            dimension_semantics=("arbitrary",), vmem_limit_bytes=VMEM_LIMIT_BYTES),
        name="adaln",
    )(c_all, w_ada, b_ada.reshape(1, n_out))


def _shift_rows(x, s, fill, row):
    return jnp.where(row >= s, pltpu.roll(x, s, axis=1), fill)


SUBLANES = 8
LANES = 128


def _lru_scan(a, u, h_in, a_scr, b_scr):
    S, L, C = a.shape
    G = L // SUBLANES
    ag = a.reshape(S * G, SUBLANES, C)
    bg = u.reshape(S * G, SUBLANES, C)
    sub = lax.broadcasted_iota(jnp.int32, ag.shape, 1)
    step = 1
    while step < SUBLANES:
        bg = ag * _shift_rows(bg, step, 0.0, sub) + bg
        ag = ag * _shift_rows(ag, step, 1.0, sub)
        step *= 2
    last_rows = pl.ds(SUBLANES - 1, S * G, stride=SUBLANES)

    def group_ends(x, scr):
        x = x.reshape(S * L, C)
        for c in range(C // LANES):
            scr[c] = x[:, c * LANES:(c + 1) * LANES]
        ends = [scr[c, last_rows, :] for c in range(C // LANES)]
        return jnp.concatenate(ends, axis=1).reshape(S, G, C)

    a_sum = group_ends(ag, a_scr)
    b_sum = group_ends(bg, b_scr)
    grp = lax.broadcasted_iota(jnp.int32, a_sum.shape, 1)
    step = 1
    while step < G:
        b_sum = a_sum * _shift_rows(b_sum, step, 0.0, grp) + b_sum
        a_sum = a_sum * _shift_rows(a_sum, step, 1.0, grp)
        step *= 2
    h_end = b_sum + a_sum * h_in
    h_enter = _shift_rows(h_end, 1, h_in, grp)
    h = bg + ag * h_enter.reshape(S * G, 1, C)
    return h.reshape(S, L, C), h_end[:, G - 1:G, :]


def _front_kernel(x_ref, mod_ref, conv0_ref, h0_ref, g_pre_ref, w_in_ref, conv_w_ref, conv_b_ref,
                  w_gate_ref, b_gate_ref, lam_ref, g_lru_ref,
                  olru_ref, q_ref, kbf_ref, vbf_ref, k_ref, v_ref, conv_out_ref, h_out_ref,
                  ext_ref, hc_ref, a_scr, b_scr, *, n_seq, n_rows):
    S, L = n_seq, n_rows

    @pl.when(pl.program_id(1) == 0)
    def _():
        ext_ref[:, 0:CONV_CARRY_ROWS, :] = conv0_ref[...]
        hc_ref[...] = h0_ref[...]

    x = x_ref[...]
    shift = mod_ref[:, 0:1, :]
    scale = mod_ref[:, 1:2, :]
    h = _rms(x, g_pre_ref[...] * (1.0 + scale)) + shift
    hb = h.reshape(S * L, D_MODEL).astype(BF16)

    def project(first_col, n_cols):
        return _dot(hb, w_in_ref[:, first_col:first_col + n_cols])

    lru_in = project(0, 2 * D_LRU)
    xl = lru_in[:, 0:D_LRU].reshape(S, L, D_LRU)
    gl = lru_in[:, D_LRU:].reshape(S, L, D_LRU)

    k = project(2 * D_LRU + D_SB, D_SB)
    k_ref[...] = k.reshape(S, L, D_SB)
    kbf_ref[...] = k.astype(BF16).reshape(S, L, D_SB)

    ext_ref[:, CONV_CARRY_ROWS:, :] = xl
    cw = conv_w_ref[...]
    xc = conv_b_ref[...]
    for j in range(CONV_WIDTH - 1):
        start = CONV_CARRY_ROWS - (CONV_WIDTH - 1 - j)
        xc = xc + ext_ref[:, start:start + L, :] * cw[j:j + 1, :]
    xc = xc + xl * cw[CONV_WIDTH - 1:CONV_WIDTH, :]
    conv_out_ref[...] = ext_ref[:, L + CONV_CARRY_ROWS - (CONV_WIDTH - 1):, :]
    ext_ref[:, 0:CONV_CARRY_ROWS, :] = xl[:, L - CONV_CARRY_ROWS:, :]

    gates = _dot(xc.reshape(S * L, D_LRU).astype(BF16), w_gate_ref[...]) + b_gate_ref[...]
    v = project(2 * D_LRU + 2 * D_SB, D_SB)
    v_ref[...] = v.reshape(S, L, D_SB)
    vbf_ref[...] = v.astype(BF16).reshape(S, L, D_SB)

    r = jax.nn.sigmoid(gates[:, 0:D_LRU]).reshape(S, L, D_LRU)
    i = jax.nn.sigmoid(gates[:, D_LRU:]).reshape(S, L, D_LRU)
    neg_lam = -lam_ref[...]
    decay = RG_C * (jnp.maximum(neg_lam, 0.0) + jnp.log1p(jnp.exp(-jnp.abs(neg_lam))))
    neg_log_a = r * decay
    a = jnp.exp2(r * (decay * -LOG2_E))
    m = jnp.tanh(neg_log_a) * (1.0 + a * a)
    u = (m * lax.rsqrt(jnp.maximum(m, F32_TINY))) * (i * xc)

    q_ref[...] = project(2 * D_LRU, D_SB).astype(BF16).reshape(S, L, D_SB)

    hl, h_last = _lru_scan(a, u, hc_ref[...], a_scr, b_scr)
    hc_ref[...] = h_last
    h_out_ref[...] = h_last

    o = hl * jax.nn.gelu(gl)
    olru_ref[...] = _rms(o, g_lru_ref[...]).astype(BF16)


def _front(x, mod, conv0, h0, g_pre, w_in_bf, conv_w, conv_b, w_gate_bf, b_gate, lam, g_lru,
           *, n_seq, n_rows, name):
    B, T, _ = x.shape
    S, L = n_seq, n_rows
    const2 = lambda s, t: (0, 0)
    seq3 = lambda s, t: (s, 0, 0)
    tile3 = lambda s, t: (s, t, 0)
    kernel = functools.partial(_front_kernel, n_seq=S, n_rows=L)
    return pl.pallas_call(
        kernel,
        grid=(B // S, T // L),
        in_specs=[
            pl.BlockSpec((S, L, D_MODEL), tile3),
            pl.BlockSpec((S, N_MOD, D_MODEL), seq3),
            pl.BlockSpec((S, CONV_CARRY_ROWS, D_LRU), seq3),
            pl.BlockSpec((S, 1, D_LRU), seq3),
            pl.BlockSpec((1, D_MODEL), const2),
            pl.BlockSpec((D_MODEL, D_IN), const2),
            pl.BlockSpec((CONV_WIDTH, D_LRU), const2),
            pl.BlockSpec((1, D_LRU), const2),
            pl.BlockSpec((D_LRU, 2 * D_LRU), const2),
            pl.BlockSpec((1, 2 * D_LRU), const2),
            pl.BlockSpec((1, D_LRU), const2),
            pl.BlockSpec((1, D_LRU), const2),
        ],
        out_specs=[
            pl.BlockSpec((S, L, D_LRU), tile3),
            pl.BlockSpec((S, L, D_SB), tile3),
            pl.BlockSpec((S, L, D_SB), tile3),
            pl.BlockSpec((S, L, D_SB), tile3),
            pl.BlockSpec((S, L, D_SB), tile3),
            pl.BlockSpec((S, L, D_SB), tile3),
            pl.BlockSpec((S, CONV_WIDTH - 1, D_LRU), seq3),
            pl.BlockSpec((S, 1, D_LRU), seq3),
        ],
        out_shape=[
            jax.ShapeDtypeStruct((B, T, D_LRU), BF16),
            jax.ShapeDtypeStruct((B, T, D_SB), BF16),
            jax.ShapeDtypeStruct((B, T, D_SB), BF16),
            jax.ShapeDtypeStruct((B, T, D_SB), BF16),
            jax.ShapeDtypeStruct((B, T, D_SB), F32),
            jax.ShapeDtypeStruct((B, T, D_SB), F32),
            jax.ShapeDtypeStruct((B, CONV_WIDTH - 1, D_LRU), F32),
            jax.ShapeDtypeStruct((B, 1, D_LRU), F32),
        ],
        scratch_shapes=[
            pltpu.VMEM((S, CONV_CARRY_ROWS + L, D_LRU), F32),
            pltpu.VMEM((S, 1, D_LRU), F32),
            pltpu.VMEM((D_LRU // LANES, S * L, LANES), F32),
            pltpu.VMEM((D_LRU // LANES, S * L, LANES), F32),
        ],
        compiler_params=pltpu.CompilerParams(
            dimension_semantics=("arbitrary", "arbitrary"), vmem_limit_bytes=VMEM_LIMIT_BYTES),
        name=name,
    )(x, mod, conv0, h0, g_pre, w_in_bf, conv_w, conv_b, w_gate_bf, b_gate, lam, g_lru)


def _prefix_ones(n):
    j = lax.broadcasted_iota(jnp.int32, (n, n), 0)
    s = lax.broadcasted_iota(jnp.int32, (n, n), 1)
    return jnp.where(j > s, -1.0, 0.0).astype(BF16)


def _stack_heads(x):
    lane = lax.broadcasted_iota(jnp.int32, x.shape, 1)
    zero = jnp.zeros_like(x)
    return jnp.concatenate([jnp.where(lane < SB_HEAD_DIM, x, zero),
                            jnp.where(lane >= SB_HEAD_DIM, x, zero)], axis=0)


def _pair_cols(p):
    return slice(p * HEAD_PAIR, (p + 1) * HEAD_PAIR)


def _sweep_tiles(q_pairs, tiles, state):
    accs, carries = state
    tq = accs[0].shape[0]
    pairs = range(N_PAIRS)
    lane = lax.broadcasted_iota(jnp.int32, (tq, HEAD_PAIR), 1)

    def scores(kt):
        return [lax.dot_general(q_pairs[p], kt[:, _pair_cols(p)], (((1,), (1,)), ((), ())),
                                preferred_element_type=F32) for p in pairs]

    def drop_terms(z, mask):
        log_beta, drop, stacked = [], [], []
        for p in pairs:
            sp = jnp.maximum(z[p], 0.0) + jnp.log(1.0 + jnp.exp2(jnp.abs(z[p]) * -LOG2_E))
            log_beta.append(z[p] - sp)
            if mask is not None:
                sp = jnp.where(mask, sp, 0.0)
            drop.append(sp)
            stacked.append(sp.astype(BF16))
        return log_beta, drop, jnp.concatenate(stacked, axis=0)

    def weigh(log_beta, log_keep_after, carries, mask, vt, accs):
        new_accs = []
        for p in pairs:
            w = jnp.exp(log_beta[p] + log_keep_after[2 * tq * p:2 * tq * (p + 1)] + carries[p])
            if mask is not None:
                w = jnp.where(mask, w, 0.0)
            pv = _dot(w.astype(BF16), vt[:, _pair_cols(p)])
            new_accs.append(accs[p] + jnp.where(lane < SB_HEAD_DIM, pv[:tq], pv[tq:]))
        return new_accs

    z = [scores(tile[0]) for tile in tiles]
    log_beta, log_keep_after, tile_carries = [], [], []
    for n, (_, _, neg_ones, mask, exists) in enumerate(tiles):
        lb, drop, stacked = drop_terms(z[n], mask)
        log_beta.append(lb)
        log_keep_after.append(_dot(stacked, neg_ones))
        if exists is not None:
            carries = tuple(jnp.where(exists, c, HIDDEN_CARRY) for c in carries)
        tile_carries.append(carries)
        carries = tuple(carries[p] - jnp.sum(drop[p], axis=-1, keepdims=True) for p in pairs)
    for n, (_, vt, _, mask, _) in enumerate(tiles):
        accs = weigh(log_beta[n], log_keep_after[n], tile_carries[n], mask, vt, accs)
    return tuple(accs), carries


def _attn_kernel(q_ref, kd_ref, vd_ref, kp_ref, vp_ref, ksrc_ref, vsrc_ref, o_ref,
                 kt_ref, vt_ref, prev_k_ref, prev_v_ref, carry_ref, sem, *, tq, n_before):
    b = pl.program_id(0)
    i = pl.program_id(1)
    before = i * tq if n_before is None else n_before

    q = q_ref[...]
    q_pairs = [_stack_heads(q[:, _pair_cols(p)]) for p in range(N_PAIRS)]

    state = (tuple(jnp.zeros((tq, HEAD_PAIR), F32) for _ in range(N_PAIRS)),
             tuple(jnp.zeros((2 * tq, 1), F32) for _ in range(N_PAIRS)))
    t_idx = lax.broadcasted_iota(jnp.int32, (tq, tq), 0)
    s_idx = lax.broadcasted_iota(jnp.int32, (tq, tq), 1)
    causal = s_idx < t_idx
    tri = _prefix_ones(KEY_TILE)
    own_tile = (kd_ref[...], vd_ref[...], _prefix_ones(tq), jnp.concatenate([causal, causal], axis=0),
                None)
    if n_before is None:
        @pl.when(i == 0)
        def _():
            prev_k_ref[...] = jnp.zeros(prev_k_ref.shape, BF16)
            prev_v_ref[...] = jnp.zeros(prev_v_ref.shape, BF16)

        prev_tile = (prev_k_ref[(i + 1) % 2], prev_v_ref[(i + 1) % 2], tri, None, i > 0)
    else:
        prev_tile = (kp_ref[...].astype(BF16), vp_ref[...].astype(BF16), tri, None, None)
    accs, carries = _sweep_tiles(q_pairs, [own_tile, prev_tile], state)
    o_ref[...] = jnp.concatenate(accs, axis=1)
    if n_before is None:
        prev_k_ref[i % 2] = kd_ref[...]
        prev_v_ref[i % 2] = vd_ref[...]
    n_tail = before // KEY_TILE - 1

    def live(carries):
        return jnp.max(functools.reduce(jnp.maximum, carries))

    top = live(carries)

    @pl.when(jnp.logical_and(n_tail > 0, top > LOG_WEIGHT_FLOOR))
    def _():
        for p in range(N_PAIRS):
            carry_ref[p] = carries[p]

        def cond(c):
            n, top = c
            return jnp.logical_and(n < n_tail, top > LOG_WEIGHT_FLOOR)

        def body(c):
            n, _ = c
            rows = pl.ds(pl.multiple_of(before - (n + 2) * KEY_TILE, KEY_TILE), KEY_TILE)
            copies = (pltpu.make_async_copy(ksrc_ref.at[b, rows, :], kt_ref, sem.at[0]),
                      pltpu.make_async_copy(vsrc_ref.at[b, rows, :], vt_ref, sem.at[1]))
            for cp in copies:
                cp.start()
            for cp in copies:
                cp.wait()
            tile = (kt_ref[...].astype(BF16), vt_ref[...].astype(BF16), tri, None, None)
            state = (tuple(o_ref[:, _pair_cols(p)] for p in range(N_PAIRS)),
                     tuple(carry_ref[p] for p in range(N_PAIRS)))
            accs, carries = _sweep_tiles(q_pairs, [tile], state)
            o_ref[...] = jnp.concatenate(accs, axis=1)
            for p in range(N_PAIRS):
                carry_ref[p] = carries[p]
            return n + 1, live(carries)

        lax.while_loop(cond, body, (jnp.int32(0), top))


def _attention(q, k_new, v_new, k_before, v_before, *, tq, name):
    B, T, _ = q.shape
    n_q = T // tq
    hbm_spec = pl.BlockSpec(memory_space=pl.ANY)
    if k_before is None:
        assert tq == KEY_TILE
        k_before, v_before, n_before = k_new, v_new, None
        prev_spec = hbm_spec
    else:
        assert n_q == 1
        n_before = k_before.shape[1]
        prev_spec = pl.BlockSpec((None, KEY_TILE, D_SB), lambda b, i: (b, n_before // KEY_TILE - 1, 0))
    tile = lambda b, i: (b, i, 0)
    new_spec = pl.BlockSpec((None, tq, D_SB), tile)
    kernel = functools.partial(_attn_kernel, tq=tq, n_before=n_before)
    return pl.pallas_call(
        kernel,
        grid=(B, n_q),
        in_specs=[new_spec, new_spec, new_spec, prev_spec, prev_spec, hbm_spec, hbm_spec],
        out_specs=pl.BlockSpec((None, tq, D_SB), tile),
        out_shape=jax.ShapeDtypeStruct((B, T, D_SB), F32),
        scratch_shapes=[
            pltpu.VMEM((KEY_TILE, D_SB), k_before.dtype),
            pltpu.VMEM((KEY_TILE, D_SB), v_before.dtype),
            pltpu.VMEM((2, KEY_TILE, D_SB), BF16),
            pltpu.VMEM((2, KEY_TILE, D_SB), BF16),
            pltpu.VMEM((N_PAIRS, 2 * tq, 1), F32),
            pltpu.SemaphoreType.DMA((2,)),
        ],
        compiler_params=pltpu.CompilerParams(
            dimension_semantics=("arbitrary", "arbitrary"), vmem_limit_bytes=VMEM_LIMIT_BYTES),
        name=name,
    )(q, k_new, v_new, k_before, v_before, k_before, v_before)


def _back_kernel(x_ref, olru_ref, osb_ref, mod_ref, g_sb_ref, g_post_mix_ref, g_pre_mlp_ref,
                 g_post_mlp_ref, w_out_ref, w_up_ref, w_down_ref, y_ref, *, n_seq, n_rows, ff_chunk):
    S, L = n_seq, n_rows
    if S > 1:
        halves = [(slice(0, S // 2), slice(None)), (slice(S // 2, S), slice(None))]
        hs, hl = S // 2, L
    else:
        halves = [(slice(None), slice(0, L // 2)), (slice(None), slice(L // 2, L))]
        hs, hl = S, L // 2
    n_chunks = D_FF // ff_chunk

    def mod(idx, row):
        return mod_ref[idx[0], row:row + 1, :]

    def attn_out(idx):
        o_lru = olru_ref[idx[0], idx[1], :].reshape(hs * hl, D_LRU)
        o_sb = _rms(osb_ref[idx[0], idx[1], :], g_sb_ref[...]).reshape(hs * hl, D_SB).astype(BF16)
        return _dot(o_lru, w_out_ref[0:D_LRU, :]) + _dot(o_sb, w_out_ref[D_LRU:, :])

    def mix_residual(idx, o):
        x = x_ref[idx[0], idx[1], :] + mod(idx, 2) * _rms(o, g_post_mix_ref[...]).reshape(hs, hl, D_MODEL)
        h = _rms(x, g_pre_mlp_ref[...] * (1.0 + mod(idx, 4))) + mod(idx, 3)
        return x, h.reshape(hs * hl, D_MODEL).astype(BF16)

    def mlp_chunk(hb, f, c):
        cols = slice(c * ff_chunk, (c + 1) * ff_chunk)
        up = jnp.maximum(_dot(hb, w_up_ref[:, cols]), 0.0)
        return f + _dot((up * up).astype(BF16), w_down_ref[cols, :])

    def mlp_residual(idx, x, f):
        y_ref[idx[0], idx[1], :] = x + mod(idx, 5) * _rms(f, g_post_mlp_ref[...]).reshape(hs, hl, D_MODEL)

    first, second = halves
    zero = jnp.zeros((hs * hl, D_MODEL), F32)
    o1 = attn_out(first)
    o2 = attn_out(second)
    x1, hb1 = mix_residual(first, o1)
    f1 = mlp_chunk(hb1, zero, 0)
    x2, hb2 = mix_residual(second, o2)
    for c in range(1, n_chunks):
        f1 = mlp_chunk(hb1, f1, c)
    f2 = mlp_chunk(hb2, zero, 0)
    mlp_residual(first, x1, f1)
    for c in range(1, n_chunks):
        f2 = mlp_chunk(hb2, f2, c)
    mlp_residual(second, x2, f2)


def _back(x, o_lru, o_sb, mod, g_sb, g_post_mix, g_pre_mlp, g_post_mlp, w_out_bf, w_up_bf, w_down_bf,
          *, n_seq, n_rows, name, ff_chunk=1024):
    B, T, _ = x.shape
    S, L = n_seq, n_rows
    const2 = lambda b, t: (0, 0)
    tile3 = lambda b, t: (b, t, 0)
    seq3 = lambda b, t: (b, 0, 0)
    kernel = functools.partial(_back_kernel, n_seq=S, n_rows=L, ff_chunk=ff_chunk)
    return pl.pallas_call(
        kernel,
        grid=(B // S, T // L),
        in_specs=[
            pl.BlockSpec((S, L, D_MODEL), tile3),
            pl.BlockSpec((S, L, D_LRU), tile3),
            pl.BlockSpec((S, L, D_SB), tile3),
            pl.BlockSpec((S, N_MOD, D_MODEL), seq3),
            pl.BlockSpec((1, D_SB), const2),
            pl.BlockSpec((1, D_MODEL), const2),
            pl.BlockSpec((1, D_MODEL), const2),
            pl.BlockSpec((1, D_MODEL), const2),
            pl.BlockSpec((D_MODEL, D_MODEL), const2),
            pl.BlockSpec((D_MODEL, D_FF), const2),
            pl.BlockSpec((D_FF, D_MODEL), const2),
        ],
        out_specs=pl.BlockSpec((S, L, D_MODEL), tile3),
        out_shape=jax.ShapeDtypeStruct((B, T, D_MODEL), F32),
        compiler_params=pltpu.CompilerParams(
            dimension_semantics=("arbitrary", "arbitrary"), vmem_limit_bytes=VMEM_LIMIT_BYTES),
        name=name,
    )(x, o_lru, o_sb, mod, g_sb, g_post_mix, g_pre_mlp, g_post_mlp, w_out_bf, w_up_bf, w_down_bf)


FRONT_ROWS = 1024
BACK_ROWS = 1024
SAMPLE_SEQS_PER_STEP = 8


def _block_diag(w):
    n, bi, bj = w.shape
    eye = jnp.eye(n, dtype=w.dtype)
    return (eye[:, None, :, None] * w[:, :, None, :]).reshape(n * bi, n * bj)


def _pad_conv_state(buf):
    return jnp.pad(buf, ((0, 0), (CONV_CARRY_ROWS - (CONV_WIDTH - 1), 0), (0, 0)))


def kernel(x_prompt, x_sample, c_prompt, c_sample, cache_conv, state_lru, cache_k, cache_v, w_ada, b_ada, g_pre_mix, g_post_mix, g_pre_mlp, g_post_mlp, w_in, conv_w, conv_b, w_rg_a, b_rg_a, w_rg_x, b_rg_x, lru_lambda, g_lru_out, g_sb_out, w_out, w_up, w_down):
    depth = w_ada.shape[0]
    bp, seq, _ = x_prompt.shape
    bs, dec_seq, _ = x_sample.shape
    assert depth == 1 and bp == 1, "single-layer, single-prompt configuration"
    past_len = cache_k.shape[2]
    l = 0

    row = lambda a: a.reshape(1, -1)
    q_cols = jnp.arange(D_IN) // D_SB == 2 * D_LRU // D_SB
    w_in_bf = (w_in[l] * jnp.where(q_cols, SB_HEAD_DIM ** -0.5, 1.0)).astype(BF16)
    w_gate_bf = jnp.concatenate([_block_diag(w_rg_a[l]), _block_diag(w_rg_x[l])], axis=1).astype(BF16)
    b_gate = jnp.concatenate([b_rg_a[l].reshape(-1), b_rg_x[l].reshape(-1)]).reshape(1, -1)
    w_out_bf = w_out[l].astype(BF16)
    w_up_bf = w_up[l].astype(BF16)
    w_down_bf = w_down[l].astype(BF16)
    front_w = (row(g_pre_mix[l]), w_in_bf, conv_w[l], row(conv_b[l]), w_gate_bf, b_gate,
               row(lru_lambda[l]), row(g_lru_out[l]))
    back_w = (row(g_sb_out[l]), row(g_post_mix[l]), row(g_pre_mlp[l]), row(g_post_mlp[l]),
              w_out_bf, w_up_bf, w_down_bf)

    mod = _adaln(jnp.concatenate([c_prompt, c_sample], axis=0), w_ada[l], b_ada[l])
    mod = mod.reshape(bp + bs, N_MOD, D_MODEL)
    mod_p, mod_s = mod[:bp], mod[bp:]

    zeros_conv = jnp.zeros((bp, CONV_CARRY_ROWS, D_LRU), F32)
    zeros_h = jnp.zeros((bp, 1, D_LRU), F32)
    olru_p, q_p, kbf_p, vbf_p, k_p, v_p, conv_p, h_p = _front(
        x_prompt, mod_p, zeros_conv, zeros_h, *front_w, n_seq=bp, n_rows=FRONT_ROWS,
        name="front_prompt")
    osb_p = _attention(q_p, kbf_p, vbf_p, None, None, tq=KEY_TILE, name="attn_prompt")
    y_p = _back(x_prompt, olru_p, osb_p, mod_p, *back_w, n_seq=1, n_rows=BACK_ROWS, name="back_prompt")

    olru_s, q_s, kbf_s, vbf_s, k_s, v_s, conv_s, h_s = _front(
        x_sample, mod_s, _pad_conv_state(cache_conv[l]), state_lru[l].reshape(bs, 1, D_LRU),
        *front_w, n_seq=SAMPLE_SEQS_PER_STEP, n_rows=dec_seq, name="front_sample")
    osb_s = _attention(q_s, kbf_s, vbf_s, cache_k[l].reshape(bs, past_len, D_SB),
                       cache_v[l].reshape(bs, past_len, D_SB), tq=dec_seq, name="attn_sample")
    y_s = _back(x_sample, olru_s, osb_s, mod_s, *back_w, n_seq=SAMPLE_SEQS_PER_STEP, n_rows=dec_seq,
                name="back_sample")

    heads = lambda a: a.reshape(1, a.shape[0], a.shape[1], N_SB_HEADS, SB_HEAD_DIM)
    return (y_p, y_s, conv_p[None], h_p.reshape(1, bp, D_LRU), heads(k_p), heads(v_p),
            conv_s[None], h_s.reshape(1, bs, D_LRU), heads(k_s), heads(v_s))
```

```python
import functools

import jax
import jax.numpy as jnp
from jax import lax
from jax.experimental import pallas as pl
from jax.experimental.pallas import tpu as pltpu

D_MODEL = 1024
D_LRU = 512
D_SB = 512
N_SB_HEADS = 8
SB_HEAD_DIM = 64
HEAD_PAIR = 2 * SB_HEAD_DIM
N_PAIRS = N_SB_HEADS // 2
CONV_WIDTH = 4
CONV_CARRY_ROWS = 8
RG_C = 8.0
D_FF = 4 * D_MODEL
D_IN = 2 * D_LRU + 3 * D_SB
N_MOD = 6
EPS = 1e-6
LOG2_E = 1.4426950408889634
F32_TINY = 1.1754943508222875e-38

KEY_TILE = 256
LOG_WEIGHT_FLOOR = -110.0
HIDDEN_CARRY = -1e30

VMEM_LIMIT_BYTES = 56 * 1024 * 1024

BF16 = jnp.bfloat16
F32 = jnp.float32


def _rms(x, g):
    return x * lax.rsqrt(jnp.mean(x * x, axis=-1, keepdims=True) + EPS) * g


def _dot(a, b):
    return jnp.dot(a, b, preferred_element_type=F32)


def _adaln_kernel(c_ref, w_ref, b_ref, o_ref):
    c = c_ref[...]
    s = (c * jax.nn.sigmoid(c)).astype(BF16)
    o_ref[...] = _dot(s, w_ref[...].astype(BF16)) + b_ref[...]


def _adaln(c_all, w_ada, b_ada, *, tn=1536):
    n_rows = c_all.shape[0]
    n_out = w_ada.shape[1]
    return pl.pallas_call(
        _adaln_kernel,
        grid=(n_out // tn,),
        in_specs=[
            pl.BlockSpec((n_rows, D_MODEL), lambda j: (0, 0)),
            pl.BlockSpec((D_MODEL, tn), lambda j: (0, j)),
            pl.BlockSpec((1, tn), lambda j: (0, j)),
        ],
        out_specs=pl.BlockSpec((n_rows, tn), lambda j: (0, j)),
        out_shape=jax.ShapeDtypeStruct((n_rows, n_out), F32),
        compiler_params=pltpu.CompilerParams(
            dimension_semantics=("arbitrary",), vmem_limit_bytes=VMEM_LIMIT_BYTES),
        name="adaln",
    )(c_all, w_ada, b_ada.reshape(1, n_out))


def _shift_rows(x, s, fill, row):
    return jnp.where(row >= s, pltpu.roll(x, s, axis=1), fill)


SUBLANES = 8
LANES = 128


def _lru_scan(a, u, h_in, a_scr, b_scr):
    S, L, C = a.shape
    G = L // SUBLANES
    ag = a.reshape(S * G, SUBLANES, C)
    bg = u.reshape(S * G, SUBLANES, C)
    sub = lax.broadcasted_iota(jnp.int32, ag.shape, 1)
    step = 1
    while step < SUBLANES:
        bg = ag * _shift_rows(bg, step, 0.0, sub) + bg
        ag = ag * _shift_rows(ag, step, 1.0, sub)
        step *= 2
    last_rows = pl.ds(SUBLANES - 1, S * G, stride=SUBLANES)

    def group_ends(x, scr):
        x = x.reshape(S * L, C)
        for c in range(C // LANES):
            scr[c] = x[:, c * LANES:(c + 1) * LANES]
        ends = [scr[c, last_rows, :] for c in range(C // LANES)]
        return jnp.concatenate(ends, axis=1).reshape(S, G, C)

    a_sum = group_ends(ag, a_scr)
    b_sum = group_ends(bg, b_scr)
    grp = lax.broadcasted_iota(jnp.int32, a_sum.shape, 1)
    step = 1
    while step < G:
        b_sum = a_sum * _shift_rows(b_sum, step, 0.0, grp) + b_sum
        a_sum = a_sum * _shift_rows(a_sum, step, 1.0, grp)
        step *= 2
    h_end = b_sum + a_sum * h_in
    h_enter = _shift_rows(h_end, 1, h_in, grp)
    h = bg + ag * h_enter.reshape(S * G, 1, C)
    return h.reshape(S, L, C), h_end[:, G - 1:G, :]


def _front_kernel(x_ref, mod_ref, conv0_ref, h0_ref, g_pre_ref, w_in_ref, conv_w_ref, conv_b_ref,
                  w_gate_ref, b_gate_ref, lam_ref, g_lru_ref,
                  olru_ref, q_ref, kbf_ref, vbf_ref, k_ref, v_ref, conv_out_ref, h_out_ref,
                  ext_ref, hc_ref, a_scr, b_scr, *, n_seq, n_rows):
    S, L = n_seq, n_rows

    @pl.when(pl.program_id(1) == 0)
    def _():
        ext_ref[:, 0:CONV_CARRY_ROWS, :] = conv0_ref[...]
        hc_ref[...] = h0_ref[...]

    x = x_ref[...]
    shift = mod_ref[:, 0:1, :]
    scale = mod_ref[:, 1:2, :]
    h = _rms(x, g_pre_ref[...] * (1.0 + scale)) + shift
    hb = h.reshape(S * L, D_MODEL).astype(BF16)

    def project(first_col, n_cols):
        return _dot(hb, w_in_ref[:, first_col:first_col + n_cols])

    lru_in = project(0, 2 * D_LRU)
    xl = lru_in[:, 0:D_LRU].reshape(S, L, D_LRU)
    gl = lru_in[:, D_LRU:].reshape(S, L, D_LRU)

    k = project(2 * D_LRU + D_SB, D_SB)
    k_ref[...] = k.reshape(S, L, D_SB)
    kbf_ref[...] = k.astype(BF16).reshape(S, L, D_SB)

    ext_ref[:, CONV_CARRY_ROWS:, :] = xl
    cw = conv_w_ref[...]
    xc = conv_b_ref[...]
    for j in range(CONV_WIDTH - 1):
        start = CONV_CARRY_ROWS - (CONV_WIDTH - 1 - j)
        xc = xc + ext_ref[:, start:start + L, :] * cw[j:j + 1, :]
    xc = xc + xl * cw[CONV_WIDTH - 1:CONV_WIDTH, :]
    conv_out_ref[...] = ext_ref[:, L + CONV_CARRY_ROWS - (CONV_WIDTH - 1):, :]
    ext_ref[:, 0:CONV_CARRY_ROWS, :] = xl[:, L - CONV_CARRY_ROWS:, :]

    gates = _dot(xc.reshape(S * L, D_LRU).astype(BF16), w_gate_ref[...]) + b_gate_ref[...]
    v = project(2 * D_LRU + 2 * D_SB, D_SB)
    v_ref[...] = v.reshape(S, L, D_SB)
    vbf_ref[...] = v.astype(BF16).reshape(S, L, D_SB)

    r = jax.nn.sigmoid(gates[:, 0:D_LRU]).reshape(S, L, D_LRU)
    i = jax.nn.sigmoid(gates[:, D_LRU:]).reshape(S, L, D_LRU)
    neg_lam = -lam_ref[...]
    decay = RG_C * (jnp.maximum(neg_lam, 0.0) + jnp.log1p(jnp.exp(-jnp.abs(neg_lam))))
    neg_log_a = r * decay
    a = jnp.exp2(r * (decay * -LOG2_E))
    m = jnp.tanh(neg_log_a) * (1.0 + a * a)
    u = (m * lax.rsqrt(jnp.maximum(m, F32_TINY))) * (i * xc)

    q_ref[...] = project(2 * D_LRU, D_SB).astype(BF16).reshape(S, L, D_SB)

    hl, h_last = _lru_scan(a, u, hc_ref[...], a_scr, b_scr)
    hc_ref[...] = h_last
    h_out_ref[...] = h_last

    o = hl * jax.nn.gelu(gl)
    olru_ref[...] = _rms(o, g_lru_ref[...]).astype(BF16)


def _front(x, mod, conv0, h0, g_pre, w_in_bf, conv_w, conv_b, w_gate_bf, b_gate, lam, g_lru,
           *, n_seq, n_rows, name):
    B, T, _ = x.shape
    S, L = n_seq, n_rows
    const2 = lambda s, t: (0, 0)
    seq3 = lambda s, t: (s, 0, 0)
    tile3 = lambda s, t: (s, t, 0)
    kernel = functools.partial(_front_kernel, n_seq=S, n_rows=L)
    return pl.pallas_call(
        kernel,
        grid=(B // S, T // L),
        in_specs=[
            pl.BlockSpec((S, L, D_MODEL), tile3),
            pl.BlockSpec((S, N_MOD, D_MODEL), seq3),
            pl.BlockSpec((S, CONV_CARRY_ROWS, D_LRU), seq3),
            pl.BlockSpec((S, 1, D_LRU), seq3),
            pl.BlockSpec((1, D_MODEL), const2),
            pl.BlockSpec((D_MODEL, D_IN), const2),
            pl.BlockSpec((CONV_WIDTH, D_LRU), const2),
            pl.BlockSpec((1, D_LRU), const2),
            pl.BlockSpec((D_LRU, 2 * D_LRU), const2),
            pl.BlockSpec((1, 2 * D_LRU), const2),
            pl.BlockSpec((1, D_LRU), const2),
            pl.BlockSpec((1, D_LRU), const2),
        ],
        out_specs=[
            pl.BlockSpec((S, L, D_LRU), tile3),
            pl.BlockSpec((S, L, D_SB), tile3),
            pl.BlockSpec((S, L, D_SB), tile3),
            pl.BlockSpec((S, L, D_SB), tile3),
            pl.BlockSpec((S, L, D_SB), tile3),
            pl.BlockSpec((S, L, D_SB), tile3),
            pl.BlockSpec((S, CONV_WIDTH - 1, D_LRU), seq3),
            pl.BlockSpec((S, 1, D_LRU), seq3),
        ],
        out_shape=[
            jax.ShapeDtypeStruct((B, T, D_LRU), BF16),
            jax.ShapeDtypeStruct((B, T, D_SB), BF16),
            jax.ShapeDtypeStruct((B, T, D_SB), BF16),
            jax.ShapeDtypeStruct((B, T, D_SB), BF16),
            jax.ShapeDtypeStruct((B, T, D_SB), F32),
            jax.ShapeDtypeStruct((B, T, D_SB), F32),
            jax.ShapeDtypeStruct((B, CONV_WIDTH - 1, D_LRU), F32),
            jax.ShapeDtypeStruct((B, 1, D_LRU), F32),
        ],
        scratch_shapes=[
            pltpu.VMEM((S, CONV_CARRY_ROWS + L, D_LRU), F32),
            pltpu.VMEM((S, 1, D_LRU), F32),
            pltpu.VMEM((D_LRU // LANES, S * L, LANES), F32),
            pltpu.VMEM((D_LRU // LANES, S * L, LANES), F32),
        ],
        compiler_params=pltpu.CompilerParams(
            dimension_semantics=("arbitrary", "arbitrary"), vmem_limit_bytes=VMEM_LIMIT_BYTES),
        name=name,
    )(x, mod, conv0, h0, g_pre, w_in_bf, conv_w, conv_b, w_gate_bf, b_gate, lam, g_lru)


def _prefix_ones(n):
    j = lax.broadcasted_iota(jnp.int32, (n, n), 0)
    s = lax.broadcasted_iota(jnp.int32, (n, n), 1)
    return jnp.where(j > s, -1.0, 0.0).astype(BF16)


def _stack_heads(x):
    lane = lax.broadcasted_iota(jnp.int32, x.shape, 1)
    zero = jnp.zeros_like(x)
    return jnp.concatenate([jnp.where(lane < SB_HEAD_DIM, x, zero),
                            jnp.where(lane >= SB_HEAD_DIM, x, zero)], axis=0)


def _pair_cols(p):
    return slice(p * HEAD_PAIR, (p + 1) * HEAD_PAIR)


def _sweep_tiles(q_pairs, tiles, state):
    accs, carries = state
    tq = accs[0].shape[0]
    pairs = range(N_PAIRS)
    lane = lax.broadcasted_iota(jnp.int32, (tq, HEAD_PAIR), 1)

    def scores(kt):
        return [lax.dot_general(q_pairs[p], kt[:, _pair_cols(p)], (((1,), (1,)), ((), ())),
                                preferred_element_type=F32) for p in pairs]

    def drop_terms(z, mask):
        log_beta, drop, stacked = [], [], []
        for p in pairs:
            sp = jnp.maximum(z[p], 0.0) + jnp.log(1.0 + jnp.exp2(jnp.abs(z[p]) * -LOG2_E))
            log_beta.append(z[p] - sp)
            if mask is not None:
                sp = jnp.where(mask, sp, 0.0)
            drop.append(sp)
            stacked.append(sp.astype(BF16))
        return log_beta, drop, jnp.concatenate(stacked, axis=0)

    def weigh(log_beta, log_keep_after, carries, mask, vt, accs):
        new_accs = []
        for p in pairs:
            w = jnp.exp(log_beta[p] + log_keep_after[2 * tq * p:2 * tq * (p + 1)] + carries[p])
            if mask is not None:
                w = jnp.where(mask, w, 0.0)
            pv = _dot(w.astype(BF16), vt[:, _pair_cols(p)])
            new_accs.append(accs[p] + jnp.where(lane < SB_HEAD_DIM, pv[:tq], pv[tq:]))
        return new_accs

    z = [scores(tile[0]) for tile in tiles]
    log_beta, log_keep_after, tile_carries = [], [], []
    for n, (_, _, neg_ones, mask, exists) in enumerate(tiles):
        lb, drop, stacked = drop_terms(z[n], mask)
        log_beta.append(lb)
        log_keep_after.append(_dot(stacked, neg_ones))
        if exists is not None:
            carries = tuple(jnp.where(exists, c, HIDDEN_CARRY) for c in carries)
        tile_carries.append(carries)
        carries = tuple(carries[p] - jnp.sum(drop[p], axis=-1, keepdims=True) for p in pairs)
    for n, (_, vt, _, mask, _) in enumerate(tiles):
        accs = weigh(log_beta[n], log_keep_after[n], tile_carries[n], mask, vt, accs)
    return tuple(accs), carries


def _attn_kernel(q_ref, kd_ref, vd_ref, kp_ref, vp_ref, ksrc_ref, vsrc_ref, o_ref,
                 kt_ref, vt_ref, carry_ref, sem, *, tq, n_before):
    n_seq, n_rows, _ = q_ref.shape
    n_sub = n_rows // tq
    i = pl.program_id(1)
    t_idx = lax.broadcasted_iota(jnp.int32, (tq, tq), 0)
    s_idx = lax.broadcasted_iota(jnp.int32, (tq, tq), 1)
    causal = s_idx < t_idx
    causal = jnp.concatenate([causal, causal], axis=0)
    own_ones = _prefix_ones(tq)
    tri = _prefix_ones(KEY_TILE)

    def live(carries):
        return jnp.max(functools.reduce(jnp.maximum, carries))

    swept = []
    for s in range(n_seq):
        for t in range(n_sub):
            rows = slice(t * tq, (t + 1) * tq)
            q = q_ref[s, rows, :]
            q_pairs = [_stack_heads(q[:, _pair_cols(p)]) for p in range(N_PAIRS)]
            own_tile = (kd_ref[s, rows, :], vd_ref[s, rows, :], own_ones, causal, None)
            if t == 0:
                prev_tile = (kp_ref[s].astype(BF16), vp_ref[s].astype(BF16), tri, None,
                             i > 0 if n_before is None else None)
            else:
                prev_rows = slice((t - 1) * tq, t * tq)
                prev_tile = (kd_ref[s, prev_rows, :], vd_ref[s, prev_rows, :], tri, None, None)
            state = (tuple(jnp.zeros((tq, HEAD_PAIR), F32) for _ in range(N_PAIRS)),
                     tuple(jnp.zeros((2 * tq, 1), F32) for _ in range(N_PAIRS)))
            accs, carries = _sweep_tiles(q_pairs, [own_tile, prev_tile], state)
            o_ref[s, rows, :] = jnp.concatenate(accs, axis=1)
            before = (i * n_sub + t) * tq if n_before is None else n_before
            swept.append((s, rows, q_pairs, carries, before))

    for s, rows, q_pairs, carries, before in swept:
        n_tail = before // KEY_TILE - 1
        top = live(carries)

        @pl.when(jnp.logical_and(n_tail > 0, top > LOG_WEIGHT_FLOOR))
        def _(s=s, rows=rows, q_pairs=q_pairs, carries=carries, before=before, n_tail=n_tail, top=top):
            for p in range(N_PAIRS):
                carry_ref[p] = carries[p]
            seq = pl.program_id(0) * n_seq + s

            def cond(c):
                n, top = c
                return jnp.logical_and(n < n_tail, top > LOG_WEIGHT_FLOOR)

            def body(c):
                n, _ = c
                key_rows = pl.ds(pl.multiple_of(before - (n + 2) * KEY_TILE, KEY_TILE), KEY_TILE)
                copies = (pltpu.make_async_copy(ksrc_ref.at[seq, key_rows, :], kt_ref, sem.at[0]),
                          pltpu.make_async_copy(vsrc_ref.at[seq, key_rows, :], vt_ref, sem.at[1]))
                for cp in copies:
                    cp.start()
                for cp in copies:
                    cp.wait()
                tile = (kt_ref[...].astype(BF16), vt_ref[...].astype(BF16), tri, None, None)
                state = (tuple(o_ref[s, rows, _pair_cols(p)] for p in range(N_PAIRS)),
                         tuple(carry_ref[p] for p in range(N_PAIRS)))
                accs, new_carries = _sweep_tiles(q_pairs, [tile], state)
                o_ref[s, rows, :] = jnp.concatenate(accs, axis=1)
                for p in range(N_PAIRS):
                    carry_ref[p] = new_carries[p]
                return n + 1, live(new_carries)

            lax.while_loop(cond, body, (jnp.int32(0), top))


def _attention(q, k_new, v_new, k_before, v_before, *, tq, n_seq, n_rows, name):
    B, T, _ = q.shape
    S, R = n_seq, n_rows
    if k_before is None:
        assert tq == KEY_TILE
        k_before, v_before, n_before = k_new, v_new, None
        prev_idx = lambda b, i: (b, jnp.maximum(i * (R // KEY_TILE) - 1, 0), 0)
    else:
        assert R == T
        n_before = k_before.shape[1]
        prev_idx = lambda b, i: (b, n_before // KEY_TILE - 1, 0)
    tile = lambda b, i: (b, i, 0)
    new_spec = pl.BlockSpec((S, R, D_SB), tile)
    prev_spec = pl.BlockSpec((S, KEY_TILE, D_SB), prev_idx)
    hbm_spec = pl.BlockSpec(memory_space=pl.ANY)
    kernel = functools.partial(_attn_kernel, tq=tq, n_before=n_before)
    return pl.pallas_call(
        kernel,
        grid=(B // S, T // R),
        in_specs=[new_spec, new_spec, new_spec, prev_spec, prev_spec, hbm_spec, hbm_spec],
        out_specs=pl.BlockSpec((S, R, D_SB), tile),
        out_shape=jax.ShapeDtypeStruct((B, T, D_SB), F32),
        scratch_shapes=[
            pltpu.VMEM((KEY_TILE, D_SB), k_before.dtype),
            pltpu.VMEM((KEY_TILE, D_SB), v_before.dtype),
            pltpu.VMEM((N_PAIRS, 2 * tq, 1), F32),
            pltpu.SemaphoreType.DMA((2,)),
        ],
        compiler_params=pltpu.CompilerParams(
            dimension_semantics=("arbitrary", "arbitrary"), vmem_limit_bytes=VMEM_LIMIT_BYTES),
        name=name,
    )(q, k_new, v_new, k_before, v_before, k_before, v_before)


def _back_kernel(x_ref, olru_ref, osb_ref, mod_ref, g_sb_ref, g_post_mix_ref, g_pre_mlp_ref,
                 g_post_mlp_ref, w_out_ref, w_up_ref, w_down_ref, y_ref, *, n_seq, n_rows, ff_chunk):
    S, L = n_seq, n_rows
    if S > 1:
        halves = [(slice(0, S // 2), slice(None)), (slice(S // 2, S), slice(None))]
        hs, hl = S // 2, L
    else:
        halves = [(slice(None), slice(0, L // 2)), (slice(None), slice(L // 2, L))]
        hs, hl = S, L // 2
    n_chunks = D_FF // ff_chunk

    def mod(idx, row):
        return mod_ref[idx[0], row:row + 1, :]

    def attn_out(idx):
        o_lru = olru_ref[idx[0], idx[1], :].reshape(hs * hl, D_LRU)
        o_sb = _rms(osb_ref[idx[0], idx[1], :], g_sb_ref[...]).reshape(hs * hl, D_SB).astype(BF16)
        return _dot(o_lru, w_out_ref[0:D_LRU, :]) + _dot(o_sb, w_out_ref[D_LRU:, :])

    def mix_residual(idx, o):
        x = x_ref[idx[0], idx[1], :] + mod(idx, 2) * _rms(o, g_post_mix_ref[...]).reshape(hs, hl, D_MODEL)
        h = _rms(x, g_pre_mlp_ref[...] * (1.0 + mod(idx, 4))) + mod(idx, 3)
        return x, h.reshape(hs * hl, D_MODEL).astype(BF16)

    def mlp_chunk(hb, f, c):
        cols = slice(c * ff_chunk, (c + 1) * ff_chunk)
        up = jnp.maximum(_dot(hb, w_up_ref[:, cols]), 0.0)
        return f + _dot((up * up).astype(BF16), w_down_ref[cols, :])

    def mlp_residual(idx, x, f):
        y_ref[idx[0], idx[1], :] = x + mod(idx, 5) * _rms(f, g_post_mlp_ref[...]).reshape(hs, hl, D_MODEL)

    first, second = halves
    zero = jnp.zeros((hs * hl, D_MODEL), F32)
    o1 = attn_out(first)
    o2 = attn_out(second)
    x1, hb1 = mix_residual(first, o1)
    f1 = mlp_chunk(hb1, zero, 0)
    x2, hb2 = mix_residual(second, o2)
    for c in range(1, n_chunks):
        f1 = mlp_chunk(hb1, f1, c)
    f2 = mlp_chunk(hb2, zero, 0)
    mlp_residual(first, x1, f1)
    for c in range(1, n_chunks):
        f2 = mlp_chunk(hb2, f2, c)
    mlp_residual(second, x2, f2)


def _back(x, o_lru, o_sb, mod, g_sb, g_post_mix, g_pre_mlp, g_post_mlp, w_out_bf, w_up_bf, w_down_bf,
          *, n_seq, n_rows, name, ff_chunk=1024):
    B, T, _ = x.shape
    S, L = n_seq, n_rows
    const2 = lambda b, t: (0, 0)
    tile3 = lambda b, t: (b, t, 0)
    seq3 = lambda b, t: (b, 0, 0)
    kernel = functools.partial(_back_kernel, n_seq=S, n_rows=L, ff_chunk=ff_chunk)
    return pl.pallas_call(
        kernel,
        grid=(B // S, T // L),
        in_specs=[
            pl.BlockSpec((S, L, D_MODEL), tile3),
            pl.BlockSpec((S, L, D_LRU), tile3),
            pl.BlockSpec((S, L, D_SB), tile3),
            pl.BlockSpec((S, N_MOD, D_MODEL), seq3),
            pl.BlockSpec((1, D_SB), const2),
            pl.BlockSpec((1, D_MODEL), const2),
            pl.BlockSpec((1, D_MODEL), const2),
            pl.BlockSpec((1, D_MODEL), const2),
            pl.BlockSpec((D_MODEL, D_MODEL), const2),
            pl.BlockSpec((D_MODEL, D_FF), const2),
            pl.BlockSpec((D_FF, D_MODEL), const2),
        ],
        out_specs=pl.BlockSpec((S, L, D_MODEL), tile3),
        out_shape=jax.ShapeDtypeStruct((B, T, D_MODEL), F32),
        compiler_params=pltpu.CompilerParams(
            dimension_semantics=("arbitrary", "arbitrary"), vmem_limit_bytes=VMEM_LIMIT_BYTES),
        name=name,
    )(x, o_lru, o_sb, mod, g_sb, g_post_mix, g_pre_mlp, g_post_mlp, w_out_bf, w_up_bf, w_down_bf)


FRONT_ROWS = 1024
BACK_ROWS = 1024
SAMPLE_SEQS_PER_STEP = 8
ATTN_ROWS = 1024
SAMPLE_ATTN_SEQS = 8


def _block_diag(w):
    n, bi, bj = w.shape
    eye = jnp.eye(n, dtype=w.dtype)
    return (eye[:, None, :, None] * w[:, :, None, :]).reshape(n * bi, n * bj)


def _pad_conv_state(buf):
    return jnp.pad(buf, ((0, 0), (CONV_CARRY_ROWS - (CONV_WIDTH - 1), 0), (0, 0)))


def kernel(x_prompt, x_sample, c_prompt, c_sample, cache_conv, state_lru, cache_k, cache_v, w_ada, b_ada, g_pre_mix, g_post_mix, g_pre_mlp, g_post_mlp, w_in, conv_w, conv_b, w_rg_a, b_rg_a, w_rg_x, b_rg_x, lru_lambda, g_lru_out, g_sb_out, w_out, w_up, w_down):
    depth = w_ada.shape[0]
    bp, seq, _ = x_prompt.shape
    bs, dec_seq, _ = x_sample.shape
    assert depth == 1 and bp == 1, "single-layer, single-prompt configuration"
    past_len = cache_k.shape[2]
    l = 0

    row = lambda a: a.reshape(1, -1)
    q_cols = jnp.arange(D_IN) // D_SB == 2 * D_LRU // D_SB
    w_in_bf = (w_in[l] * jnp.where(q_cols, SB_HEAD_DIM ** -0.5, 1.0)).astype(BF16)
    w_gate_bf = jnp.concatenate([_block_diag(w_rg_a[l]), _block_diag(w_rg_x[l])], axis=1).astype(BF16)
    b_gate = jnp.concatenate([b_rg_a[l].reshape(-1), b_rg_x[l].reshape(-1)]).reshape(1, -1)
    w_out_bf = w_out[l].astype(BF16)
    w_up_bf = w_up[l].astype(BF16)
    w_down_bf = w_down[l].astype(BF16)
    front_w = (row(g_pre_mix[l]), w_in_bf, conv_w[l], row(conv_b[l]), w_gate_bf, b_gate,
               row(lru_lambda[l]), row(g_lru_out[l]))
    back_w = (row(g_sb_out[l]), row(g_post_mix[l]), row(g_pre_mlp[l]), row(g_post_mlp[l]),
              w_out_bf, w_up_bf, w_down_bf)

    mod = _adaln(jnp.concatenate([c_prompt, c_sample], axis=0), w_ada[l], b_ada[l])
    mod = mod.reshape(bp + bs, N_MOD, D_MODEL)
    mod_p, mod_s = mod[:bp], mod[bp:]

    zeros_conv = jnp.zeros((bp, CONV_CARRY_ROWS, D_LRU), F32)
    zeros_h = jnp.zeros((bp, 1, D_LRU), F32)
    olru_p, q_p, kbf_p, vbf_p, k_p, v_p, conv_p, h_p = _front(
        x_prompt, mod_p, zeros_conv, zeros_h, *front_w, n_seq=bp, n_rows=FRONT_ROWS,
        name="front_prompt")
    osb_p = _attention(q_p, kbf_p, vbf_p, None, None, tq=KEY_TILE, n_seq=bp, n_rows=ATTN_ROWS,
                       name="attn_prompt")
    y_p = _back(x_prompt, olru_p, osb_p, mod_p, *back_w, n_seq=1, n_rows=BACK_ROWS, name="back_prompt")

    olru_s, q_s, kbf_s, vbf_s, k_s, v_s, conv_s, h_s = _front(
        x_sample, mod_s, _pad_conv_state(cache_conv[l]), state_lru[l].reshape(bs, 1, D_LRU),
        *front_w, n_seq=SAMPLE_SEQS_PER_STEP, n_rows=dec_seq, name="front_sample")
    osb_s = _attention(q_s, kbf_s, vbf_s, cache_k[l].reshape(bs, past_len, D_SB),
                       cache_v[l].reshape(bs, past_len, D_SB), tq=dec_seq, n_seq=SAMPLE_ATTN_SEQS,
                       n_rows=dec_seq, name="attn_sample")
    y_s = _back(x_sample, olru_s, osb_s, mod_s, *back_w, n_seq=SAMPLE_SEQS_PER_STEP, n_rows=dec_seq,
                name="back_sample")

    heads = lambda a: a.reshape(1, a.shape[0], a.shape[1], N_SB_HEADS, SB_HEAD_DIM)
    return (y_p, y_s, conv_p[None], h_p.reshape(1, bp, D_LRU), heads(k_p), heads(v_p),
            conv_s[None], h_s.reshape(1, bs, D_LRU), heads(k_s), heads(v_s))
```

```python
import functools

import jax
import jax.numpy as jnp
from jax import lax
from jax.experimental import pallas as pl
from jax.experimental.pallas import tpu as pltpu

D_MODEL = 1024
D_LRU = 512
D_SB = 512
N_SB_HEADS = 8
SB_HEAD_DIM = 64
HEAD_PAIR = 2 * SB_HEAD_DIM
N_PAIRS = N_SB_HEADS // 2
CONV_WIDTH = 4
CONV_CARRY_ROWS = 8
RG_C = 8.0
D_FF = 4 * D_MODEL
D_IN = 2 * D_LRU + 3 * D_SB
N_MOD = 6
EPS = 1e-6
LOG2_E = 1.4426950408889634
F32_TINY = 1.1754943508222875e-38

KEY_TILE = 256
LOG_WEIGHT_FLOOR = -110.0
HIDDEN_CARRY = -1e30

VMEM_LIMIT_BYTES = 56 * 1024 * 1024

BF16 = jnp.bfloat16
F32 = jnp.float32


def _rms(x, g):
    return x * lax.rsqrt(jnp.mean(x * x, axis=-1, keepdims=True) + EPS) * g


def _dot(a, b):
    return jnp.dot(a, b, preferred_element_type=F32)


def _adaln_kernel(c_ref, w_ref, b_ref, o_ref):
    c = c_ref[...]
    s = (c * jax.nn.sigmoid(c)).astype(BF16)
    o_ref[...] = _dot(s, w_ref[...].astype(BF16)) + b_ref[...]


def _adaln(c_all, w_ada, b_ada, *, tn=1536):
    n_rows = c_all.shape[0]
    n_out = w_ada.shape[1]
    return pl.pallas_call(
        _adaln_kernel,
        grid=(n_out // tn,),
        in_specs=[
            pl.BlockSpec((n_rows, D_MODEL), lambda j: (0, 0)),
            pl.BlockSpec((D_MODEL, tn), lambda j: (0, j)),
            pl.BlockSpec((1, tn), lambda j: (0, j)),
        ],
        out_specs=pl.BlockSpec((n_rows, tn), lambda j: (0, j)),
        out_shape=jax.ShapeDtypeStruct((n_rows, n_out), F32),
        compiler_params=pltpu.CompilerParams(
            dimension_semantics=("arbitrary",), vmem_limit_bytes=VMEM_LIMIT_BYTES),
        name="adaln",
    )(c_all, w_ada, b_ada.reshape(1, n_out))


def _shift_rows(x, s, fill, row):
    return jnp.where(row >= s, pltpu.roll(x, s, axis=1), fill)


SUBLANES = 8
LANES = 128


def _lru_scan(a, u, h_in, a_scr, b_scr):
    S, L, C = a.shape
    G = L // SUBLANES
    ag = a.reshape(S * G, SUBLANES, C)
    bg = u.reshape(S * G, SUBLANES, C)
    sub = lax.broadcasted_iota(jnp.int32, ag.shape, 1)
    step = 1
    while step < SUBLANES:
        bg = ag * _shift_rows(bg, step, 0.0, sub) + bg
        ag = ag * _shift_rows(ag, step, 1.0, sub)
        step *= 2
    last_rows = pl.ds(SUBLANES - 1, S * G, stride=SUBLANES)

    def group_ends(x, scr):
        x = x.reshape(S * L, C)
        for c in range(C // LANES):
            scr[c] = x[:, c * LANES:(c + 1) * LANES]
        ends = [scr[c, last_rows, :] for c in range(C // LANES)]
        return jnp.concatenate(ends, axis=1).reshape(S, G, C)

    a_sum = group_ends(ag, a_scr)
    b_sum = group_ends(bg, b_scr)
    grp = lax.broadcasted_iota(jnp.int32, a_sum.shape, 1)
    step = 1
    while step < G:
        b_sum = a_sum * _shift_rows(b_sum, step, 0.0, grp) + b_sum
        a_sum = a_sum * _shift_rows(a_sum, step, 1.0, grp)
        step *= 2
    h_end = b_sum + a_sum * h_in
    h_enter = _shift_rows(h_end, 1, h_in, grp)
    h = bg + ag * h_enter.reshape(S * G, 1, C)
    return h.reshape(S, L, C), h_end[:, G - 1:G, :]


def _front_kernel(x_ref, mod_ref, conv0_ref, h0_ref, g_pre_ref, w_in_ref, conv_w_ref, conv_b_ref,
                  w_gate_ref, b_gate_ref, lam_ref, g_lru_ref,
                  olru_ref, q_ref, kbf_ref, vbf_ref, k_ref, v_ref, conv_out_ref, h_out_ref,
                  ext_ref, hc_ref, a_scr, b_scr, *, n_seq, n_rows):
    S, L = n_seq, n_rows

    @pl.when(pl.program_id(1) == 0)
    def _():
        ext_ref[:, 0:CONV_CARRY_ROWS, :] = conv0_ref[...]
        hc_ref[...] = h0_ref[...]

    x = x_ref[...]
    shift = mod_ref[:, 0:1, :]
    scale = mod_ref[:, 1:2, :]
    h = _rms(x, g_pre_ref[...] * (1.0 + scale)) + shift
    hb = h.reshape(S * L, D_MODEL).astype(BF16)

    def project(first_col, n_cols):
        return _dot(hb, w_in_ref[:, first_col:first_col + n_cols])

    lru_in = project(0, 2 * D_LRU)
    xl = lru_in[:, 0:D_LRU].reshape(S, L, D_LRU)
    gl = lru_in[:, D_LRU:].reshape(S, L, D_LRU)

    k = project(2 * D_LRU + D_SB, D_SB)
    k_ref[...] = k.reshape(S, L, D_SB)
    kbf_ref[...] = k.astype(BF16).reshape(S, L, D_SB)

    ext_ref[:, CONV_CARRY_ROWS:, :] = xl
    cw = conv_w_ref[...]
    xc = conv_b_ref[...]
    for j in range(CONV_WIDTH - 1):
        start = CONV_CARRY_ROWS - (CONV_WIDTH - 1 - j)
        xc = xc + ext_ref[:, start:start + L, :] * cw[j:j + 1, :]
    xc = xc + xl * cw[CONV_WIDTH - 1:CONV_WIDTH, :]
    conv_out_ref[...] = ext_ref[:, L + CONV_CARRY_ROWS - (CONV_WIDTH - 1):, :]
    ext_ref[:, 0:CONV_CARRY_ROWS, :] = xl[:, L - CONV_CARRY_ROWS:, :]

    gates = _dot(xc.reshape(S * L, D_LRU).astype(BF16), w_gate_ref[...]) + b_gate_ref[...]
    v = project(2 * D_LRU + 2 * D_SB, D_SB)
    v_ref[...] = v.reshape(S, L, D_SB)
    vbf_ref[...] = v.astype(BF16).reshape(S, L, D_SB)

    r = jax.nn.sigmoid(gates[:, 0:D_LRU]).reshape(S, L, D_LRU)
    i = jax.nn.sigmoid(gates[:, D_LRU:]).reshape(S, L, D_LRU)
    neg_lam = -lam_ref[...]
    decay = RG_C * (jnp.maximum(neg_lam, 0.0) + jnp.log1p(jnp.exp(-jnp.abs(neg_lam))))
    neg_log_a = r * decay
    a = jnp.exp2(r * (decay * -LOG2_E))
    m = jnp.tanh(neg_log_a) * (1.0 + a * a)
    u = (m * lax.rsqrt(jnp.maximum(m, F32_TINY))) * (i * xc)

    q_ref[...] = project(2 * D_LRU, D_SB).astype(BF16).reshape(S, L, D_SB)

    hl, h_last = _lru_scan(a, u, hc_ref[...], a_scr, b_scr)
    hc_ref[...] = h_last
    h_out_ref[...] = h_last

    o = hl * jax.nn.gelu(gl)
    olru_ref[...] = _rms(o, g_lru_ref[...]).astype(BF16)


def _front(x, mod, conv0, h0, g_pre, w_in_bf, conv_w, conv_b, w_gate_bf, b_gate, lam, g_lru,
           *, n_seq, n_rows, name):
    B, T, _ = x.shape
    S, L = n_seq, n_rows
    const2 = lambda s, t: (0, 0)
    seq3 = lambda s, t: (s, 0, 0)
    tile3 = lambda s, t: (s, t, 0)
    kernel = functools.partial(_front_kernel, n_seq=S, n_rows=L)
    return pl.pallas_call(
        kernel,
        grid=(B // S, T // L),
        in_specs=[
            pl.BlockSpec((S, L, D_MODEL), tile3),
            pl.BlockSpec((S, N_MOD, D_MODEL), seq3),
            pl.BlockSpec((S, CONV_CARRY_ROWS, D_LRU), seq3),
            pl.BlockSpec((S, 1, D_LRU), seq3),
            pl.BlockSpec((1, D_MODEL), const2),
            pl.BlockSpec((D_MODEL, D_IN), const2),
            pl.BlockSpec((CONV_WIDTH, D_LRU), const2),
            pl.BlockSpec((1, D_LRU), const2),
            pl.BlockSpec((D_LRU, 2 * D_LRU), const2),
            pl.BlockSpec((1, 2 * D_LRU), const2),
            pl.BlockSpec((1, D_LRU), const2),
            pl.BlockSpec((1, D_LRU), const2),
        ],
        out_specs=[
            pl.BlockSpec((S, L, D_LRU), tile3),
            pl.BlockSpec((S, L, D_SB), tile3),
            pl.BlockSpec((S, L, D_SB), tile3),
            pl.BlockSpec((S, L, D_SB), tile3),
            pl.BlockSpec((S, L, D_SB), tile3),
            pl.BlockSpec((S, L, D_SB), tile3),
            pl.BlockSpec((S, CONV_WIDTH - 1, D_LRU), seq3),
            pl.BlockSpec((S, 1, D_LRU), seq3),
        ],
        out_shape=[
            jax.ShapeDtypeStruct((B, T, D_LRU), BF16),
            jax.ShapeDtypeStruct((B, T, D_SB), BF16),
            jax.ShapeDtypeStruct((B, T, D_SB), BF16),
            jax.ShapeDtypeStruct((B, T, D_SB), BF16),
            jax.ShapeDtypeStruct((B, T, D_SB), F32),
            jax.ShapeDtypeStruct((B, T, D_SB), F32),
            jax.ShapeDtypeStruct((B, CONV_WIDTH - 1, D_LRU), F32),
            jax.ShapeDtypeStruct((B, 1, D_LRU), F32),
        ],
        scratch_shapes=[
            pltpu.VMEM((S, CONV_CARRY_ROWS + L, D_LRU), F32),
            pltpu.VMEM((S, 1, D_LRU), F32),
            pltpu.VMEM((D_LRU // LANES, S * L, LANES), F32),
            pltpu.VMEM((D_LRU // LANES, S * L, LANES), F32),
        ],
        compiler_params=pltpu.CompilerParams(
            dimension_semantics=("arbitrary", "arbitrary"), vmem_limit_bytes=VMEM_LIMIT_BYTES),
        name=name,
    )(x, mod, conv0, h0, g_pre, w_in_bf, conv_w, conv_b, w_gate_bf, b_gate, lam, g_lru)


def _prefix_ones(n):
    j = lax.broadcasted_iota(jnp.int32, (n, n), 0)
    s = lax.broadcasted_iota(jnp.int32, (n, n), 1)
    return jnp.where(j > s, -1.0, 0.0).astype(BF16)


def _stack_heads(x):
    lane = lax.broadcasted_iota(jnp.int32, x.shape, 1)
    zero = jnp.zeros_like(x)
    return jnp.concatenate([jnp.where(lane < SB_HEAD_DIM, x, zero),
                            jnp.where(lane >= SB_HEAD_DIM, x, zero)], axis=0)


def _pair_cols(p):
    return slice(p * HEAD_PAIR, (p + 1) * HEAD_PAIR)


def _sweep_tiles(q_pairs, tiles, state):
    accs, carries = state
    tq = accs[0].shape[0]
    pairs = range(N_PAIRS)
    lane = lax.broadcasted_iota(jnp.int32, (tq, HEAD_PAIR), 1)

    def scores(kt):
        return [lax.dot_general(q_pairs[p], kt[:, _pair_cols(p)], (((1,), (1,)), ((), ())),
                                preferred_element_type=F32) for p in pairs]

    def drop_terms(z, mask):
        log_beta, drop, stacked = [], [], []
        for p in pairs:
            sp = jnp.maximum(z[p], 0.0) + jnp.log(1.0 + jnp.exp2(jnp.abs(z[p]) * -LOG2_E))
            log_beta.append(z[p] - sp)
            if mask is not None:
                sp = jnp.where(mask, sp, 0.0)
            drop.append(sp)
            stacked.append(sp.astype(BF16))
        return log_beta, drop, jnp.concatenate(stacked, axis=0)

    def weigh(log_beta, log_keep_after, carries, mask, vt, accs):
        new_accs = []
        for p in pairs:
            w = jnp.exp(log_beta[p] + log_keep_after[2 * tq * p:2 * tq * (p + 1)] + carries[p])
            if mask is not None:
                w = jnp.where(mask, w, 0.0)
            pv = _dot(w.astype(BF16), vt[:, _pair_cols(p)])
            new_accs.append(accs[p] + jnp.where(lane < SB_HEAD_DIM, pv[:tq], pv[tq:]))
        return new_accs

    z = [scores(tile[0]) for tile in tiles]
    log_beta, log_keep_after, tile_carries = [], [], []
    for n, (_, _, neg_ones, mask, exists) in enumerate(tiles):
        lb, drop, stacked = drop_terms(z[n], mask)
        log_beta.append(lb)
        log_keep_after.append(_dot(stacked, neg_ones))
        if exists is not None:
            carries = tuple(jnp.where(exists, c, HIDDEN_CARRY) for c in carries)
        tile_carries.append(carries)
        carries = tuple(carries[p] - jnp.sum(drop[p], axis=-1, keepdims=True) for p in pairs)
    for n, (_, vt, _, mask, _) in enumerate(tiles):
        accs = weigh(log_beta[n], log_keep_after[n], tile_carries[n], mask, vt, accs)
    return tuple(accs), carries


def _attn_kernel(q_ref, kd_ref, vd_ref, kp_ref, vp_ref, ksrc_ref, vsrc_ref, o_ref,
                 kt_ref, vt_ref, carry_ref, sem, *, tq, n_before):
    n_seq, n_rows, _ = q_ref.shape
    n_sub = n_rows // tq
    i = pl.program_id(1)
    t_idx = lax.broadcasted_iota(jnp.int32, (tq, tq), 0)
    s_idx = lax.broadcasted_iota(jnp.int32, (tq, tq), 1)
    causal = s_idx < t_idx
    causal = jnp.concatenate([causal, causal], axis=0)
    own_ones = _prefix_ones(tq)
    tri = _prefix_ones(KEY_TILE)

    def live(carries):
        return jnp.max(functools.reduce(jnp.maximum, carries))

    swept = []
    for s in range(n_seq):
        for t in range(n_sub):
            rows = slice(t * tq, (t + 1) * tq)
            q = q_ref[s, rows, :]
            q_pairs = [_stack_heads(q[:, _pair_cols(p)]) for p in range(N_PAIRS)]
            own_tile = (kd_ref[s, rows, :], vd_ref[s, rows, :], own_ones, causal, None)
            if t == 0:
                prev_tile = (kp_ref[s].astype(BF16), vp_ref[s].astype(BF16), tri, None,
                             i > 0 if n_before is None else None)
            else:
                prev_rows = slice((t - 1) * tq, t * tq)
                prev_tile = (kd_ref[s, prev_rows, :], vd_ref[s, prev_rows, :], tri, None, None)
            state = (tuple(jnp.zeros((tq, HEAD_PAIR), F32) for _ in range(N_PAIRS)),
                     tuple(jnp.zeros((2 * tq, 1), F32) for _ in range(N_PAIRS)))
            accs, carries = _sweep_tiles(q_pairs, [own_tile, prev_tile], state)
            o_ref[s, rows, :] = jnp.concatenate(accs, axis=1)
            before = (i * n_sub + t) * tq if n_before is None else n_before
            swept.append((s, rows, q_pairs, carries, before))

    def tails(before):
        return before // KEY_TILE - 1

    pending = [jnp.where(tails(before) > 0, functools.reduce(jnp.maximum, carries), HIDDEN_CARRY)
               for _, _, _, carries, before in swept]

    @pl.when(jnp.max(functools.reduce(jnp.maximum, pending)) > LOG_WEIGHT_FLOOR)
    def _():
        for s, rows, q_pairs, carries, before in swept:
            n_tail = tails(before)
            for p in range(N_PAIRS):
                carry_ref[p] = carries[p]
            seq = pl.program_id(0) * n_seq + s

            def cond(c, n_tail=n_tail):
                n, top = c
                return jnp.logical_and(n < n_tail, top > LOG_WEIGHT_FLOOR)

            def body(c, s=s, rows=rows, q_pairs=q_pairs, before=before, seq=seq):
                n, _ = c
                key_rows = pl.ds(pl.multiple_of(before - (n + 2) * KEY_TILE, KEY_TILE), KEY_TILE)
                copies = (pltpu.make_async_copy(ksrc_ref.at[seq, key_rows, :], kt_ref, sem.at[0]),
                          pltpu.make_async_copy(vsrc_ref.at[seq, key_rows, :], vt_ref, sem.at[1]))
                for cp in copies:
                    cp.start()
                for cp in copies:
                    cp.wait()
                tile = (kt_ref[...].astype(BF16), vt_ref[...].astype(BF16), tri, None, None)
                state = (tuple(o_ref[s, rows, _pair_cols(p)] for p in range(N_PAIRS)),
                         tuple(carry_ref[p] for p in range(N_PAIRS)))
                accs, new_carries = _sweep_tiles(q_pairs, [tile], state)
                o_ref[s, rows, :] = jnp.concatenate(accs, axis=1)
                for p in range(N_PAIRS):
                    carry_ref[p] = new_carries[p]
                return n + 1, live(new_carries)

            lax.while_loop(cond, body, (jnp.int32(0), live(carries)))


def _attention(q, k_new, v_new, k_before, v_before, *, tq, n_seq, n_rows, name):
    B, T, _ = q.shape
    S, R = n_seq, n_rows
    if k_before is None:
        assert tq == KEY_TILE
        k_before, v_before, n_before = k_new, v_new, None
        prev_idx = lambda b, i: (b, jnp.maximum(i * (R // KEY_TILE) - 1, 0), 0)
    else:
        assert R == T
        n_before = k_before.shape[1]
        prev_idx = lambda b, i: (b, n_before // KEY_TILE - 1, 0)
    tile = lambda b, i: (b, i, 0)
    new_spec = pl.BlockSpec((S, R, D_SB), tile)
    prev_spec = pl.BlockSpec((S, KEY_TILE, D_SB), prev_idx)
    hbm_spec = pl.BlockSpec(memory_space=pl.ANY)
    kernel = functools.partial(_attn_kernel, tq=tq, n_before=n_before)
    return pl.pallas_call(
        kernel,
        grid=(B // S, T // R),
        in_specs=[new_spec, new_spec, new_spec, prev_spec, prev_spec, hbm_spec, hbm_spec],
        out_specs=pl.BlockSpec((S, R, D_SB), tile),
        out_shape=jax.ShapeDtypeStruct((B, T, D_SB), F32),
        scratch_shapes=[
            pltpu.VMEM((KEY_TILE, D_SB), k_before.dtype),
            pltpu.VMEM((KEY_TILE, D_SB), v_before.dtype),
            pltpu.VMEM((N_PAIRS, 2 * tq, 1), F32),
            pltpu.SemaphoreType.DMA((2,)),
        ],
        compiler_params=pltpu.CompilerParams(
            dimension_semantics=("arbitrary", "arbitrary"), vmem_limit_bytes=VMEM_LIMIT_BYTES),
        name=name,
    )(q, k_new, v_new, k_before, v_before, k_before, v_before)


def _back_kernel(x_ref, olru_ref, osb_ref, mod_ref, g_sb_ref, g_post_mix_ref, g_pre_mlp_ref,
                 g_post_mlp_ref, w_out_ref, w_up_ref, w_down_ref, y_ref, *, n_seq, n_rows, ff_chunk):
    S, L = n_seq, n_rows
    if S > 1:
        halves = [(slice(0, S // 2), slice(None)), (slice(S // 2, S), slice(None))]
        hs, hl = S // 2, L
    else:
        halves = [(slice(None), slice(0, L // 2)), (slice(None), slice(L // 2, L))]
        hs, hl = S, L // 2
    n_chunks = D_FF // ff_chunk

    def mod(idx, row):
        return mod_ref[idx[0], row:row + 1, :]

    def attn_out(idx):
        o_lru = olru_ref[idx[0], idx[1], :].reshape(hs * hl, D_LRU)
        o_sb = _rms(osb_ref[idx[0], idx[1], :], g_sb_ref[...]).reshape(hs * hl, D_SB).astype(BF16)
        return _dot(o_lru, w_out_ref[0:D_LRU, :]) + _dot(o_sb, w_out_ref[D_LRU:, :])

    def mix_residual(idx, o):
        x = x_ref[idx[0], idx[1], :] + mod(idx, 2) * _rms(o, g_post_mix_ref[...]).reshape(hs, hl, D_MODEL)
        h = _rms(x, g_pre_mlp_ref[...] * (1.0 + mod(idx, 4))) + mod(idx, 3)
        return x, h.reshape(hs * hl, D_MODEL).astype(BF16)

    def mlp_chunk(hb, f, c):
        cols = slice(c * ff_chunk, (c + 1) * ff_chunk)
        up = jnp.maximum(_dot(hb, w_up_ref[:, cols]), 0.0)
        return f + _dot((up * up).astype(BF16), w_down_ref[cols, :])

    def mlp_residual(idx, x, f):
        y_ref[idx[0], idx[1], :] = x + mod(idx, 5) * _rms(f, g_post_mlp_ref[...]).reshape(hs, hl, D_MODEL)

    first, second = halves
    zero = jnp.zeros((hs * hl, D_MODEL), F32)
    o1 = attn_out(first)
    o2 = attn_out(second)
    x1, hb1 = mix_residual(first, o1)
    f1 = mlp_chunk(hb1, zero, 0)
    x2, hb2 = mix_residual(second, o2)
    for c in range(1, n_chunks):
        f1 = mlp_chunk(hb1, f1, c)
    f2 = mlp_chunk(hb2, zero, 0)
    mlp_residual(first, x1, f1)
    for c in range(1, n_chunks):
        f2 = mlp_chunk(hb2, f2, c)
    mlp_residual(second, x2, f2)


def _back(x, o_lru, o_sb, mod, g_sb, g_post_mix, g_pre_mlp, g_post_mlp, w_out_bf, w_up_bf, w_down_bf,
          *, n_seq, n_rows, name, ff_chunk=1024):
    B, T, _ = x.shape
    S, L = n_seq, n_rows
    const2 = lambda b, t: (0, 0)
    tile3 = lambda b, t: (b, t, 0)
    seq3 = lambda b, t: (b, 0, 0)
    kernel = functools.partial(_back_kernel, n_seq=S, n_rows=L, ff_chunk=ff_chunk)
    return pl.pallas_call(
        kernel,
        grid=(B // S, T // L),
        in_specs=[
            pl.BlockSpec((S, L, D_MODEL), tile3),
            pl.BlockSpec((S, L, D_LRU), tile3),
            pl.BlockSpec((S, L, D_SB), tile3),
            pl.BlockSpec((S, N_MOD, D_MODEL), seq3),
            pl.BlockSpec((1, D_SB), const2),
            pl.BlockSpec((1, D_MODEL), const2),
            pl.BlockSpec((1, D_MODEL), const2),
            pl.BlockSpec((1, D_MODEL), const2),
            pl.BlockSpec((D_MODEL, D_MODEL), const2),
            pl.BlockSpec((D_MODEL, D_FF), const2),
            pl.BlockSpec((D_FF, D_MODEL), const2),
        ],
        out_specs=pl.BlockSpec((S, L, D_MODEL), tile3),
        out_shape=jax.ShapeDtypeStruct((B, T, D_MODEL), F32),
        compiler_params=pltpu.CompilerParams(
            dimension_semantics=("arbitrary", "arbitrary"), vmem_limit_bytes=VMEM_LIMIT_BYTES),
        name=name,
    )(x, o_lru, o_sb, mod, g_sb, g_post_mix, g_pre_mlp, g_post_mlp, w_out_bf, w_up_bf, w_down_bf)


FRONT_ROWS = 1024
BACK_ROWS = 1024
SAMPLE_SEQS_PER_STEP = 8
ATTN_ROWS = 512
SAMPLE_ATTN_SEQS = 4


def _block_diag(w):
    n, bi, bj = w.shape
    eye = jnp.eye(n, dtype=w.dtype)
    return (eye[:, None, :, None] * w[:, :, None, :]).reshape(n * bi, n * bj)


def _pad_conv_state(buf):
    return jnp.pad(buf, ((0, 0), (CONV_CARRY_ROWS - (CONV_WIDTH - 1), 0), (0, 0)))


def kernel(x_prompt, x_sample, c_prompt, c_sample, cache_conv, state_lru, cache_k, cache_v, w_ada, b_ada, g_pre_mix, g_post_mix, g_pre_mlp, g_post_mlp, w_in, conv_w, conv_b, w_rg_a, b_rg_a, w_rg_x, b_rg_x, lru_lambda, g_lru_out, g_sb_out, w_out, w_up, w_down):
    depth = w_ada.shape[0]
    bp, seq, _ = x_prompt.shape
    bs, dec_seq, _ = x_sample.shape
    assert depth == 1 and bp == 1, "single-layer, single-prompt configuration"
    past_len = cache_k.shape[2]
    l = 0

    row = lambda a: a.reshape(1, -1)
    q_cols = jnp.arange(D_IN) // D_SB == 2 * D_LRU // D_SB
    w_in_bf = (w_in[l] * jnp.where(q_cols, SB_HEAD_DIM ** -0.5, 1.0)).astype(BF16)
    w_gate_bf = jnp.concatenate([_block_diag(w_rg_a[l]), _block_diag(w_rg_x[l])], axis=1).astype(BF16)
    b_gate = jnp.concatenate([b_rg_a[l].reshape(-1), b_rg_x[l].reshape(-1)]).reshape(1, -1)
    w_out_bf = w_out[l].astype(BF16)
    w_up_bf = w_up[l].astype(BF16)
    w_down_bf = w_down[l].astype(BF16)
    front_w = (row(g_pre_mix[l]), w_in_bf, conv_w[l], row(conv_b[l]), w_gate_bf, b_gate,
               row(lru_lambda[l]), row(g_lru_out[l]))
    back_w = (row(g_sb_out[l]), row(g_post_mix[l]), row(g_pre_mlp[l]), row(g_post_mlp[l]),
              w_out_bf, w_up_bf, w_down_bf)

    mod = _adaln(jnp.concatenate([c_prompt, c_sample], axis=0), w_ada[l], b_ada[l])
    mod = mod.reshape(bp + bs, N_MOD, D_MODEL)
    mod_p, mod_s = mod[:bp], mod[bp:]

    zeros_conv = jnp.zeros((bp, CONV_CARRY_ROWS, D_LRU), F32)
    zeros_h = jnp.zeros((bp, 1, D_LRU), F32)
    olru_p, q_p, kbf_p, vbf_p, k_p, v_p, conv_p, h_p = _front(
        x_prompt, mod_p, zeros_conv, zeros_h, *front_w, n_seq=bp, n_rows=FRONT_ROWS,
        name="front_prompt")
    osb_p = _attention(q_p, kbf_p, vbf_p, None, None, tq=KEY_TILE, n_seq=bp, n_rows=ATTN_ROWS,
                       name="attn_prompt")
    y_p = _back(x_prompt, olru_p, osb_p, mod_p, *back_w, n_seq=1, n_rows=BACK_ROWS, name="back_prompt")

    olru_s, q_s, kbf_s, vbf_s, k_s, v_s, conv_s, h_s = _front(
        x_sample, mod_s, _pad_conv_state(cache_conv[l]), state_lru[l].reshape(bs, 1, D_LRU),
        *front_w, n_seq=SAMPLE_SEQS_PER_STEP, n_rows=dec_seq, name="front_sample")
    osb_s = _attention(q_s, kbf_s, vbf_s, cache_k[l].reshape(bs, past_len, D_SB),
                       cache_v[l].reshape(bs, past_len, D_SB), tq=dec_seq, n_seq=SAMPLE_ATTN_SEQS,
                       n_rows=dec_seq, name="attn_sample")
    y_s = _back(x_sample, olru_s, osb_s, mod_s, *back_w, n_seq=SAMPLE_SEQS_PER_STEP, n_rows=dec_seq,
                name="back_sample")

    heads = lambda a: a.reshape(1, a.shape[0], a.shape[1], N_SB_HEADS, SB_HEAD_DIM)
    return (y_p, y_s, conv_p[None], h_p.reshape(1, bp, D_LRU), heads(k_p), heads(v_p),
            conv_s[None], h_s.reshape(1, bs, D_LRU), heads(k_s), heads(v_s))
```
